```python
import math
import jax, jax.numpy as jnp
from jax import lax
import numpy as np

D_MODEL = 1024
BATCH = 32
SEQ = 256
DEPTH = 4
DEC_BATCH = 4
DEC_SEQ = 4096
PAST_LEN = 256

GRID_W = 64
A_HEADS = 4
A_DK = 128
A_DV = 128
A_KW = A_HEADS * A_DK
A_WIDTH = A_HEADS * A_DV
B_HEADS = 16
B_HEADDIM = 64
B_WIDTH = B_HEADS * B_HEADDIM
B_GROUPS = 4
B_STATE = 64
B_CONV_CH = B_WIDTH + 2 * B_GROUPS * B_STATE
CONV_W = 3
D_FF = 2816
A_CHUNK = 32
B_CHUNK = 64
N_MOD = 6
EPS = 1e-6
IN_SIZES = (A_KW, A_WIDTH, A_KW, A_KW, A_WIDTH, B_WIDTH, B_CONV_CH, B_HEADS, B_HEADS, 2 * D_MODEL)
IN_COLS = sum(IN_SIZES)

kernel_name = 'hybrid_hgrn2_ssd_diffusion_step'


def _split_points():
    pts, acc = [], 0
    for s in IN_SIZES[:-1]:
        acc += s
        pts.append(acc)
    return pts


def _rmsnorm(x, gain):
    xf = x.astype(jnp.float32)
    y = xf * lax.rsqrt(jnp.mean(xf * xf, axis=-1, keepdims=True) + EPS)
    return (y * gain.astype(jnp.float32)).astype(x.dtype)


def _group_rmsnorm(x, gain, groups):
    shp = x.shape
    xf = x.astype(jnp.float32).reshape(shp[:-1] + (groups, shp[-1] // groups))
    y = xf * lax.rsqrt(jnp.mean(xf * xf, axis=-1, keepdims=True) + EPS)
    return (y.reshape(shp) * gain.astype(jnp.float32)).astype(x.dtype)


def _flip(t):
    return jnp.flip(t, axis=1)


def _dwconv(x, w, b):
    l = x.shape[1]
    pad = CONV_W // 2
    xp = jnp.pad(x, ((0, 0), (pad, pad), (0, 0)))
    out = b
    for k in range(CONV_W):
        out = out + xp[:, k:k + l] * w[k]
    return out


def _seq_conv(x, w, b, on_grid):
    if on_grid:
        bsz, l, ch = x.shape
        rows = l // GRID_W
        return _dwconv(x.reshape(bsz * rows, GRID_W, ch), w, b).reshape(bsz, l, ch)
    return _dwconv(x, w, b)


def _chunks(t, csize):
    b, l, h, f = t.shape
    return t.reshape(b, l // csize, csize, h, f).transpose(1, 0, 3, 2, 4)


def _unchunk(t):
    n, b, h, c, f = t.shape
    return t.transpose(1, 0, 3, 2, 4).reshape(b, n * c, h, f)


def _hgrn2_scan(q, k, v, log_f, s0):
    f32 = jnp.float32
    mask = jnp.tril(jnp.ones((A_CHUNK, A_CHUNK), bool))[:, :, None]

    def step(S, inp):
        qc, kc, vc, gc = inp
        bcum = jnp.cumsum(gc, axis=2)
        diff = bcum[:, :, :, None, :] - bcum[:, :, None, :, :]
        decay = jnp.where(mask, jnp.exp(jnp.where(mask, diff, 0.0)), 0.0)
        scores = jnp.einsum('bhtk,bhsk,bhtsk->bhts', qc, kc, decay)
        o = (jnp.einsum('bhts,bhsv->bhtv', scores, vc)
             + jnp.einsum('bhtk,bhkv->bhtv', qc * jnp.exp(bcum), S))
        blast = bcum[:, :, -1:, :]
        S = (jnp.exp(bcum[:, :, -1, :])[..., None] * S
             + jnp.einsum('bhsk,bhsv->bhkv', kc * jnp.exp(blast - bcum), vc))
        return S, o

    xs = tuple(_chunks(t.astype(f32), A_CHUNK) for t in (q, k, v, log_f))
    s_fin, o = lax.scan(step, s0.astype(f32), xs)
    return _unchunk(o), s_fin


def _ssd_scan(xs, dt, a, bm, cm, s0):
    f32 = jnp.float32
    b, l, h, p = xs.shape
    g, n = bm.shape[2], bm.shape[3]
    j = h // g
    nc = l // B_CHUNK
    xc = xs.astype(f32).reshape(b, nc, B_CHUNK, g, j, p).transpose(1, 0, 3, 4, 2, 5)
    dtc = dt.astype(f32).reshape(b, nc, B_CHUNK, g, j).transpose(1, 0, 3, 4, 2)
    ac = a.astype(f32).reshape(b, nc, B_CHUNK, g, j).transpose(1, 0, 3, 4, 2)
    bc = bm.astype(f32).reshape(b, nc, B_CHUNK, g, n).transpose(1, 0, 3, 2, 4)
    cc = cm.astype(f32).reshape(b, nc, B_CHUNK, g, n).transpose(1, 0, 3, 2, 4)
    mask = jnp.tril(jnp.ones((B_CHUNK, B_CHUNK), bool))

    def step(S, inp):
        xq, dq, aq, bq, cq = inp
        acum = jnp.cumsum(aq, axis=-1)
        seg = acum[..., :, None] - acum[..., None, :]
        lmat = jnp.where(mask, jnp.exp(jnp.where(mask, seg, 0.0)), 0.0)
        cb = jnp.einsum('bgtn,bgsn->bgts', cq, bq)
        y = jnp.einsum('bgjts,bgjsp->bgjtp', cb[:, :, None] * lmat, xq * dq[..., None])
        y = y + jnp.einsum('bgtn,bgjpn->bgjtp', cq, S) * jnp.exp(acum)[..., None]
        alast = acum[..., -1:]
        wgt = jnp.exp(alast - acum) * dq
        S = jnp.exp(alast)[..., None] * S + jnp.einsum('bgjs,bgsn,bgjsp->bgjpn', wgt, bq, xq)
        return S, y

    s_fin, y = lax.scan(step, s0.astype(f32).reshape(b, g, j, p, n), (xc, dtc, ac, bc, cc))
    y = y.transpose(1, 0, 4, 2, 3, 5).reshape(b, l, h, p)
    return y, s_fin.reshape(b, h, p, n)


def _mixer(u, on_grid, s_a, s_b, lp):
    f32 = jnp.float32
    bsz, l, _ = u.shape
    proj = u @ lp['w_in']
    q, i_in, f_fw, f_bw, g_a, z, xbc, dt_fw, dt_bw, gates = jnp.split(proj, _split_points(), axis=-1)

    qh = jax.nn.silu(q).reshape(bsz, l, A_HEADS, A_DK)
    vh = i_in.reshape(bsz, l, A_HEADS, A_DV)
    outs_a, fin_a = [], []
    for d, (fraw, s0) in enumerate(((f_fw, s_a[0]), (f_bw, s_a[1]))):
        lbd = lp['lb'][d]
        fx = fraw.astype(f32)
        log_f = jnp.logaddexp(jnp.log(lbd), jnp.log1p(-lbd) + jax.nn.log_sigmoid(fx))
        kk = (1.0 - lbd) * jax.nn.sigmoid(-fx)
        args = (qh, kk.reshape(bsz, l, A_HEADS, A_DK), vh, log_f.reshape(bsz, l, A_HEADS, A_DK))
        if d == 1:
            args = tuple(_flip(t) for t in args)
        o, sf = _hgrn2_scan(*args, s0)
        if d == 1:
            o = _flip(o)
        outs_a.append(o)
        fin_a.append(sf)
    o_a = (outs_a[0] + outs_a[1]).reshape(bsz, l, A_WIDTH).astype(u.dtype)
    o_a = _group_rmsnorm(o_a, lp['a_norm'], A_HEADS) * jax.nn.silu(g_a)

    xbc = jax.nn.silu(_seq_conv(xbc, lp['conv_w'], lp['conv_b'], on_grid))
    xs, bm, cm = jnp.split(xbc, [B_WIDTH, B_WIDTH + B_GROUPS * B_STATE], axis=-1)
    xs = xs.reshape(bsz, l, B_HEADS, B_HEADDIM)
    bm = bm.reshape(bsz, l, B_GROUPS, B_STATE)
    cm = cm.reshape(bsz, l, B_GROUPS, B_STATE)
    outs_b, fin_b = [], []
    for d, (dtraw, s0) in enumerate(((dt_fw, s_b[0]), (dt_bw, s_b[1]))):
        dt = jax.nn.softplus(dtraw.astype(f32) + lp['dt_bias'][d].astype(f32))
        a = -jnp.exp(lp['a_log'][d].astype(f32)) * dt
        args = (xs, dt, a, bm, cm)
        if d == 1:
            args = tuple(_flip(t) for t in args)
        y, sf = _ssd_scan(*args, s0)
        if d == 1:
            y = _flip(y)
        outs_b.append(y)
        fin_b.append(sf)
    y_b = outs_b[0] + outs_b[1] + lp['d_skip'].astype(f32)[:, None] * xs.astype(f32)
    y_b = y_b.reshape(bsz, l, B_WIDTH).astype(u.dtype) * jax.nn.silu(z)
    y_b = _group_rmsnorm(y_b, lp['b_norm'], B_GROUPS)

    g_A, g_B = jnp.split(jax.nn.sigmoid(gates), 2, axis=-1)
    merged = g_A * (o_a @ lp['w_br_a']) + g_B * (y_b @ lp['w_br_b'])
    return merged @ lp['w_out'], (fin_a[0], fin_a[1], fin_b[0], fin_b[1])


def _conv_ffn(u, on_grid, lp):
    h = _seq_conv(u @ lp['w_ff_up'], lp['ff_conv_w'], lp['ff_conv_b'], on_grid)
    a, b = jnp.split(h, 2, axis=-1)
    return (jax.nn.silu(a) * b) @ lp['w_ff_down']


def _block(x, mod, on_grid, s_a, s_b, lp):
    sh1, sc1, g1, sh2, sc2, g2 = jnp.split(mod[:, None, :], N_MOD, axis=-1)
    u = _rmsnorm(x, lp['ln1']) * (1.0 + sc1) + sh1
    mix, fin = _mixer(u, on_grid, s_a, s_b, lp)
    x = x + g1 * mix
    u = _rmsnorm(x, lp['ln2']) * (1.0 + sc2) + sh2
    x = x + g2 * _conv_ffn(u, on_grid, lp)
    return x, fin


def setup_inputs(seed: int = 0) -> dict:
    key = jax.random.key(seed)
    ks = jax.random.split(key, 32)
    f32 = jnp.float32

    def nrm(k, shape, scale):
        return scale * jax.random.normal(k, shape, f32)

    dt0 = jnp.exp(jax.random.uniform(ks[16], (DEPTH, 2, B_HEADS), f32, math.log(1e-3), math.log(1e-1)))
    return {
        'x_prompt': nrm(ks[0], (BATCH, SEQ, D_MODEL), 1.0),
        'x_sample': nrm(ks[1], (DEC_BATCH, DEC_SEQ, D_MODEL), 1.0),
        'c': nrm(ks[2], (DEC_BATCH, D_MODEL), 1.0),
        'state_hgrn': nrm(ks[3], (DEC_BATCH, DEPTH, 2, A_HEADS, A_DK, A_DV), 0.5),
        'state_ssd': nrm(ks[4], (DEC_BATCH, DEPTH, 2, B_HEADS, B_HEADDIM, B_STATE), 0.5),
        'c_ctx': nrm(ks[5], (D_MODEL,), 1.0),
        'w_ada': nrm(ks[6], (DEPTH, D_MODEL, N_MOD * D_MODEL), 0.3 * D_MODEL ** -0.5),
        'b_ada': nrm(ks[7], (DEPTH, N_MOD * D_MODEL), 0.02),
        'ln1': 1.0 + nrm(ks[8], (DEPTH, D_MODEL), 0.02),
        'ln2': 1.0 + nrm(ks[9], (DEPTH, D_MODEL), 0.02),
        'ln_f': 1.0 + nrm(ks[10], (D_MODEL,), 0.02),
        'w_in': nrm(ks[11], (DEPTH, D_MODEL, IN_COLS), D_MODEL ** -0.5),
        'lb_logits': nrm(ks[12], (DEPTH, 2, A_KW), 1.0),
        'a_norm': 1.0 + nrm(ks[13], (DEPTH, A_WIDTH), 0.02),
        'conv_w': nrm(ks[14], (DEPTH, CONV_W, B_CONV_CH), CONV_W ** -0.5),
        'conv_b': nrm(ks[15], (DEPTH, B_CONV_CH), 0.02),
        'dt_bias': dt0 + jnp.log(-jnp.expm1(-dt0)),
        'a_log': jnp.log(jax.random.uniform(ks[17], (DEPTH, 2, B_HEADS), f32, 1.0, 16.0)),
        'd_skip': 1.0 + nrm(ks[18], (DEPTH, B_HEADS), 0.1),
        'b_norm': 1.0 + nrm(ks[19], (DEPTH, B_WIDTH), 0.02),
        'w_br_a': nrm(ks[20], (DEPTH, A_WIDTH, D_MODEL), A_WIDTH ** -0.5),
        'w_br_b': nrm(ks[21], (DEPTH, B_WIDTH, D_MODEL), B_WIDTH ** -0.5),
        'w_out': nrm(ks[22], (DEPTH, D_MODEL, D_MODEL), D_MODEL ** -0.5),
        'w_ff_up': nrm(ks[23], (DEPTH, D_MODEL, 2 * D_FF), D_MODEL ** -0.5),
        'ff_conv_w': nrm(ks[24], (DEPTH, CONV_W, 2 * D_FF), CONV_W ** -0.5),
        'ff_conv_b': nrm(ks[25], (DEPTH, 2 * D_FF), 0.02),
        'w_ff_down': nrm(ks[26], (DEPTH, D_FF, D_MODEL), D_FF ** -0.5),
    }


def reference(x_prompt, x_sample, c, state_hgrn, state_ssd, c_ctx, w_ada, b_ada, ln1, ln2, ln_f,
              w_in, lb_logits, a_norm, conv_w, conv_b, dt_bias, a_log, d_skip, b_norm,
              w_br_a, w_br_b, w_out, w_ff_up, ff_conv_w, ff_conv_b, w_ff_down):
    f32 = jnp.float32
    lb_all = jnp.cumsum(jax.nn.softmax(lb_logits.astype(f32), axis=0), axis=0)
    lb_all = jnp.maximum(lb_all - lb_all[0:1], 0.0)
    layers = [dict(ln1=ln1[l], ln2=ln2[l], w_in=w_in[l], lb=lb_all[l], a_norm=a_norm[l],
                   conv_w=conv_w[l], conv_b=conv_b[l], dt_bias=dt_bias[l], a_log=a_log[l],
                   d_skip=d_skip[l], b_norm=b_norm[l], w_br_a=w_br_a[l], w_br_b=w_br_b[l],
                   w_out=w_out[l], w_ff_up=w_ff_up[l], ff_conv_w=ff_conv_w[l],
                   ff_conv_b=ff_conv_b[l], w_ff_down=w_ff_down[l]) for l in range(DEPTH)]

    bp = x_prompt.shape[0]
    za = jnp.zeros((bp, A_HEADS, A_DK, A_DV), f32)
    zb = jnp.zeros((bp, B_HEADS, B_HEADDIM, B_STATE), f32)
    h = x_prompt
    st_a, st_b = [], []
    for l in range(DEPTH):
        mod = (jax.nn.silu(c_ctx) @ w_ada[l] + b_ada[l])[None]
        h, fin = _block(h, mod, False, (za, za), (zb, zb), layers[l])
        st_a.append(jnp.stack(fin[:2], axis=1))
        st_b.append(jnp.stack(fin[2:], axis=1))
    y_prompt = _rmsnorm(h, ln_f)
    new_state_hgrn = jnp.stack(st_a, axis=1).astype(x_prompt.dtype)
    new_state_ssd = jnp.stack(st_b, axis=1).astype(x_prompt.dtype)

    h = x_sample
    for l in range(DEPTH):
        mod = jax.nn.silu(c) @ w_ada[l] + b_ada[l]
        sa = state_hgrn[:, l]
        sb = state_ssd[:, l]
        h, _ = _block(h, mod, True, (sa[:, 0], sa[:, 1]), (sb[:, 0], sb[:, 1]), layers[l])
    y_sample = _rmsnorm(h, ln_f)
    return (y_prompt, y_sample, new_state_hgrn, new_state_ssd)
```

```python
import functools

import jax
import jax.numpy as jnp
from jax import lax
from jax.experimental import pallas as pl
from jax.experimental.pallas import tpu as pltpu

F32 = jnp.float32
BF16 = jnp.bfloat16

D_MODEL = 1024
DEPTH = 4
GRID_W = 64
A_HEADS = 4
A_DK = 128
A_DV = 128
A_KW = A_HEADS * A_DK
A_WIDTH = A_HEADS * A_DV
B_HEADS = 16
B_HEADDIM = 64
B_WIDTH = B_HEADS * B_HEADDIM
B_GROUPS = 4
B_STATE = 64
B_GN = B_GROUPS * B_STATE
B_CONV_CH = B_WIDTH + 2 * B_GN
D_FF = 2816
N_MOD = 6
EPS = 1e-6

LANES = 128
TILE = 256
MOD_ROWS = 8
MOD_COLS = N_MOD * D_MODEL
MOD_BLOCK = 1536
FF_BLOCK = 256
VMEM_LIMIT = 56 * 1024 * 1024

C_Q = 0
C_I = C_Q + A_KW
C_FFW = C_I + A_WIDTH
C_FBW = C_FFW + A_KW
C_GA = C_FBW + A_KW
C_Z = C_GA + A_WIDTH
C_XBC = C_Z + B_WIDTH
C_GATES = C_XBC + B_CONV_CH
C_DTFW = C_GATES + 2 * D_MODEL
C_DTBW = C_DTFW + LANES
IN_PACKED = C_DTBW + LANES

NT_DIMS = (((1,), (1,)), ((), ()))
TN_DIMS = (((0,), (0,)), ((), ()))


def _mm(a, b):
    return jnp.dot(a, b, preferred_element_type=F32)


def _mm_nt(a, b):
    return lax.dot_general(a, b, NT_DIMS, preferred_element_type=F32)


def _mm_tn(a, b):
    return lax.dot_general(a, b, TN_DIMS, preferred_element_type=F32)


def _sigmoid(x):
    return 1.0 / (1.0 + jnp.exp(-x))


def _silu(x):
    return x * _sigmoid(x)


def _softplus(x):
    return jnp.maximum(x, 0.0) + jnp.log1p(jnp.exp(-jnp.abs(x)))


def _rms(x):
    return x * lax.rsqrt(jnp.mean(x * x, axis=-1, keepdims=True) + EPS)


def _iota(shape, dim):
    return lax.broadcasted_iota(jnp.int32, shape, dim)


def _scan_rows(g, reverse):
    n = g.shape[0]
    r = _iota(g.shape, 0)
    y = g
    d = 1
    while d < n:
        if reverse:
            y = y + jnp.where(r < n - d, pltpu.roll(y, n - d, axis=0), 0.0)
        else:
            y = y + jnp.where(r >= d, pltpu.roll(y, d, axis=0), 0.0)
        d *= 2
    return y


def _level_ref(c, c_ref, m, off):
    n, w = c.shape
    if 2 * m < 8:
        rk = _iota(c.shape, 0) & (2 * m - 1)
        out = c
        for k in range(2 * m):
            if k != off:
                out = jnp.where(rk == k, pltpu.roll(c, (k - off) % n, axis=0), out)
        return out
    pieces = [jnp.broadcast_to(c_ref[pl.ds(j * 2 * m + off, 1), :], (2 * m, w)) for j in range(n // (2 * m))]
    return pieces[0] if len(pieces) == 1 else jnp.concatenate(pieces, axis=0)


def _hgrn_gates(fx, lb):
    log_sig = jnp.minimum(fx, 0.0) - jnp.log1p(jnp.exp(-jnp.abs(fx)))
    a = jnp.log(lb)
    b = jnp.log1p(-lb) + log_sig
    log_f = jnp.maximum(a, b) + jnp.log1p(jnp.exp(-jnp.abs(a - b)))
    return log_f, (1.0 - lb) * _sigmoid(-fx)


def _lower_bound(logits, layer):
    e = jnp.exp(logits - jnp.max(logits, axis=0, keepdims=True))
    sm = e / jnp.sum(e, axis=0, keepdims=True)
    acc = jnp.zeros_like(sm[0:1])
    for j in range(1, layer + 1):
        acc = acc + sm[j:j + 1]
    return jnp.maximum(acc, 0.0)


def _hgrn_dir(q, v, log_f, k, st_ref, c_ref, reverse):
    n = q.shape[0]
    levels = n.bit_length() - 1
    row_w = _iota((n, LANES), 0)
    row_s = _iota((n, n), 0)
    col_s = _iota((n, n), 1)
    ones = jnp.ones((LANES, LANES), BF16)
    outs = []
    for h in range(A_HEADS):
        sl = slice(h * LANES, (h + 1) * LANES)
        qh, vh, kh = q[:, sl], v[:, sl], k[:, sl]
        c = _scan_rows(log_f[:, sl], reverse)
        c_ref[...] = c
        scores = None
        for l in range(levels):
            m = 1 << l
            ref = _level_ref(c, c_ref, m, m if reverse else m - 1)
            w = jnp.exp(-jnp.abs(c - ref))
            upper = ((row_w >> l) & 1) == 1
            q_side = jnp.logical_not(upper) if reverse else upper
            qt = jnp.where(q_side, qh * w, 0.0).astype(BF16)
            kt = jnp.where(q_side, 0.0, kh * w).astype(BF16)
            s = _mm_nt(qt, kt)
            if 2 * m < n:
                s = jnp.where((row_s >> (l + 1)) == (col_s >> (l + 1)), s, 0.0)
            scores = s if scores is None else scores + s
        c_end = c_ref[pl.ds(0 if reverse else n - 1, 1), :]
        vb = vh.astype(BF16)
        st = st_ref[h]
        diag = _mm((qh * kh).astype(BF16), ones)
        o = _mm(scores.astype(BF16), vb) + diag * vh
        o = o + _mm_nt((qh * jnp.exp(c)).astype(BF16), st.astype(BF16))
        st_ref[h] = st * jnp.exp(c_end) + _mm_tn(vb, (kh * jnp.exp(c_end - c)).astype(BF16))
        outs.append(o)
    return jnp.concatenate(outs, axis=1)


def _expand_heads(a):
    r = a.shape[0]
    lane = _iota((r, LANES), 1)
    slabs = []
    for j in range(B_HEADS // 2):
        lo = jnp.broadcast_to(a[:, 2 * j:2 * j + 1], (r, LANES))
        hi = jnp.broadcast_to(a[:, 2 * j + 1:2 * j + 2], (r, LANES))
        slabs.append(jnp.where(lane < B_HEADDIM, lo, hi))
    return jnp.concatenate(slabs, axis=1)


def _ssd_dir(xs, bm, cm, dt_raw, dt_bias, a_log, st_ref, reverse):
    n = xs.shape[0]
    dt = _softplus(dt_raw + dt_bias)
    ac = _scan_rows(-jnp.exp(a_log) * dt, reverse)
    ac_t = ac.T
    dt_t = dt.T
    row_s = _iota((n, n), 0)
    col_s = _iota((n, n), 1)
    tri = (col_s >= row_s) if reverse else (col_s <= row_s)
    lane_gn = _iota((n, B_GN), 1)
    lane = _iota((n, LANES), 1)
    xb = xs.astype(BF16)
    bb = bm.astype(BF16)
    slabs = []
    for g in range(B_GROUPS):
        cb = _mm_nt(jnp.where((lane_gn >> 6) == g, cm, 0.0).astype(BF16), bb)
        for jp in range(2):
            slab = xb[:, (2 * g + jp) * LANES:(2 * g + jp + 1) * LANES]
            pair = []
            for e in range(2):
                h = 4 * g + 2 * jp + e
                seg = ac[:, h:h + 1] - ac_t[h:h + 1, :]
                w = jnp.exp(jnp.where(tri, seg, -1e30)) * dt_t[h:h + 1, :] * cb
                pair.append(_mm(w.astype(BF16), slab))
            slabs.append(jnp.where(lane < B_HEADDIM, pair[0], pair[1]))
    y = jnp.concatenate(slabs, axis=1)
    st = st_ref[...]
    y = y + _mm(cm.astype(BF16), st.astype(BF16)) * _expand_heads(jnp.exp(ac))
    a_end = ac[0:1, :] if reverse else ac[n - 1:n, :]
    wgt = _expand_heads(jnp.exp(a_end - ac) * dt)
    upd = _mm_tn(bb, (xs * wgt).astype(BF16))
    block = (_iota(st.shape, 0) >> 6) == (_iota(st.shape, 1) >> 8)
    st_ref[...] = jnp.where(block, st * _expand_heads(jnp.exp(a_end)) + upd, 0.0)
    return y


def _load_states(sa_in, sb_in, sa_ref, sb_ref):
    if sa_in is None:
        sa_ref[...] = jnp.zeros_like(sa_ref)
        sb_ref[...] = jnp.zeros_like(sb_ref)
    else:
        sa_ref[...] = sa_in[0]
        sb_ref[...] = jnp.zeros_like(sb_ref)
        for g in range(B_GROUPS):
            sb_ref[g * B_STATE:(g + 1) * B_STATE, g * 256:(g + 1) * 256] = sb_in[0, g]


def _store_states(sa_out, sb_out, sa_ref, sb_ref):
    sa_out[0] = sa_ref[...]
    for g in range(B_GROUPS):
        sb_out[0, g] = sb_ref[g * B_STATE:(g + 1) * B_STATE, g * 256:(g + 1) * 256]


def _mod_kernel(c_ref, w_ref, b_ref, o_ref):
    o_ref[0] = jnp.dot(_silu(c_ref[...]), w_ref[0], preferred_element_type=F32,
                       precision=lax.Precision.HIGHEST) + b_ref[0]


def _modulation(cvec, w_ada, b_ada):
    depth = w_ada.shape[0]
    return pl.pallas_call(
        _mod_kernel,
        grid=(depth, MOD_COLS // MOD_BLOCK),
        in_specs=[
            pl.BlockSpec((MOD_ROWS, D_MODEL), lambda l, j: (0, 0)),
            pl.BlockSpec((1, D_MODEL, MOD_BLOCK), lambda l, j: (l, 0, j)),
            pl.BlockSpec((1, 1, MOD_BLOCK), lambda l, j: (l, 0, j)),
        ],
        out_specs=pl.BlockSpec((1, MOD_ROWS, MOD_BLOCK), lambda l, j: (l, 0, j)),
        out_shape=jax.ShapeDtypeStruct((depth, MOD_ROWS, MOD_COLS), F32),
        compiler_params=pltpu.CompilerParams(dimension_semantics=("arbitrary", "arbitrary"),
                                             vmem_limit_bytes=VMEM_LIMIT),
        name="adaln_mod",
    )(cvec, w_ada, b_ada.reshape(depth, 1, MOD_COLS))


def _fwd_kernel(*refs, layer, row_block, has_init, emit_state):
    (x_ref, mod_ref, ln_ref, w_ref, lbl_ref, cw_ref, cb_ref, dtb_ref, alog_ref, dskip_ref), refs = refs[:10], refs[10:]
    if has_init:
        (sa_in, sb_in), refs = refs[:2], refs[2:]
    else:
        sa_in = sb_in = None
    (q_out, v_out, fbw_out, xbc_out, dtbw_out, ga_out, z_out, gates_out, oa_out, yb_out), refs = refs[:10], refs[10:]
    if emit_state:
        (sa_out, sb_out), refs = refs[:2], refs[2:]
    sa_ref, sb_ref, c_ref = refs

    i = pl.program_id(1)

    @pl.when(i == 0)
    def _():
        _load_states(sa_in, sb_in, sa_ref, sb_ref)

    x = x_ref[0]
    n = x.shape[0]
    mod = mod_ref[0]
    shift, scale = mod[:, 0:D_MODEL], mod[:, D_MODEL:2 * D_MODEL]
    u = ((_rms(x) * ln_ref[...]) * (1.0 + scale) + shift).astype(BF16)

    def proj(lo, hi):
        return _mm(u, w_ref[:, lo:hi])

    ga_out[0] = proj(C_GA, C_Z)
    z_out[0] = proj(C_Z, C_XBC)
    gates_out[0] = proj(C_GATES, C_DTFW)
    fbw_out[0] = proj(C_FBW, C_GA)
    dtbw_out[0] = proj(C_DTBW, IN_PACKED)

    xbc = proj(C_XBC, C_GATES)
    r = _iota(xbc.shape, 0) & (row_block - 1)
    prev = jnp.where(r == 0, 0.0, pltpu.roll(xbc, 1, axis=0))
    nxt = jnp.where(r == row_block - 1, 0.0, pltpu.roll(xbc, n - 1, axis=0))
    cw = cw_ref[...]
    xbc = _silu(cb_ref[...] + prev * cw[0:1] + xbc * cw[1:2] + nxt * cw[2:3])
    xbc_out[0] = xbc

    q = _silu(proj(C_Q, C_I))
    v = proj(C_I, C_FFW)
    q_out[0] = q
    v_out[0] = v
    log_f, k = _hgrn_gates(proj(C_FFW, C_FBW), _lower_bound(lbl_ref[...], layer))
    oa_out[0] = _hgrn_dir(q, v, log_f, k, sa_ref, c_ref, reverse=False)

    xs = xbc[:, 0:B_WIDTH]
    y = _ssd_dir(xs, xbc[:, B_WIDTH:B_WIDTH + B_GN], xbc[:, B_WIDTH + B_GN:B_CONV_CH],
                 proj(C_DTFW, C_DTBW), dtb_ref[...], alog_ref[...], sb_ref, reverse=False)
    yb_out[0] = y + dskip_ref[...] * xs

    if emit_state:
        @pl.when(i == pl.num_programs(1) - 1)
        def _():
            _store_states(sa_out, sb_out, sa_ref, sb_ref)


def _const_spec(shape):
    nd = len(shape)
    return pl.BlockSpec(shape, lambda b, i: (0,) * nd, pipeline_mode=pl.Buffered(1))


def _tok_spec(width, nt, reverse):
    if reverse:
        return pl.BlockSpec((1, TILE, width), lambda b, i: (b, nt - 1 - i, 0))
    return pl.BlockSpec((1, TILE, width), lambda b, i: (b, i, 0))


def _state_specs():
    return [pl.BlockSpec((1, A_HEADS, A_DV, A_DK), lambda b, i: (b, 0, 0, 0)),
            pl.BlockSpec((1, B_GROUPS, B_STATE, 256), lambda b, i: (b, 0, 0, 0))]


def _state_shapes(bsz):
    return [jax.ShapeDtypeStruct((bsz, A_HEADS, A_DV, A_DK), F32),
            jax.ShapeDtypeStruct((bsz, B_GROUPS, B_STATE, 256), F32)]


def _state_scratch():
    return [pltpu.VMEM((A_HEADS, A_DV, A_DK), F32), pltpu.VMEM((B_GN, B_WIDTH), F32), pltpu.VMEM((TILE, LANES), F32)]


def _mod_spec(mod_row0, mod_stride):
    return pl.BlockSpec((1, 1, MOD_COLS), lambda b, i: (mod_row0 + mod_stride * b, 0, 0))


def _fwd_call(x, mod, lp, layer, row_block, mod_row0, mod_stride, init, emit_state):
    bsz, length, _ = x.shape
    nt = length // TILE
    tok = functools.partial(_tok_spec, nt=nt, reverse=False)
    in_specs = [tok(D_MODEL), _mod_spec(mod_row0, mod_stride), _const_spec((1, D_MODEL)),
                _const_spec((D_MODEL, IN_PACKED)), _const_spec((DEPTH, A_KW)), _const_spec((3, B_CONV_CH)),
                _const_spec((1, B_CONV_CH)), _const_spec((1, LANES)), _const_spec((1, LANES)),
                _const_spec((1, B_WIDTH))]
    args = [x, mod, lp['ln1'], lp['w_in'], lp['lbl'][0], lp['conv_w'], lp['conv_b'], lp['dt_bias'][0],
            lp['a_log'][0], lp['d_skip']]
    if init is not None:
        in_specs += _state_specs()
        args += list(init)
    widths = (A_KW, A_WIDTH, A_KW, B_CONV_CH, LANES, A_WIDTH, B_WIDTH, 2 * D_MODEL, A_WIDTH, B_WIDTH)
    out_specs = [tok(w) for w in widths]
    out_shape = [jax.ShapeDtypeStruct((bsz, length, w), F32) for w in widths]
    if emit_state:
        out_specs += _state_specs()
        out_shape += _state_shapes(bsz)
    return pl.pallas_call(
        functools.partial(_fwd_kernel, layer=layer, row_block=row_block, has_init=init is not None,
                          emit_state=emit_state),
        grid=(bsz, nt), in_specs=in_specs, out_specs=out_specs, out_shape=out_shape,
        scratch_shapes=_state_scratch(),
        compiler_params=pltpu.CompilerParams(dimension_semantics=("arbitrary", "arbitrary"),
                                             vmem_limit_bytes=VMEM_LIMIT),
        name="mixer_fwd",
    )(*args)


def _bwd_kernel(*refs, layer, has_init, emit_state):
    (x_ref, mod_ref, q_ref, v_ref, fbw_ref, xbc_ref, dtbw_ref, ga_ref, z_ref, gates_ref, oa_ref, yb_ref,
     lbl_ref, dtb_ref, alog_ref, an_ref, bn_ref, wa_ref, wb_ref, wo_ref), refs = refs[:20], refs[20:]
    if has_init:
        (sa_in, sb_in), refs = refs[:2], refs[2:]
    else:
        sa_in = sb_in = None
    x_out, refs = refs[0], refs[1:]
    if emit_state:
        (sa_out, sb_out), refs = refs[:2], refs[2:]
    sa_ref, sb_ref, c_ref = refs

    i = pl.program_id(1)

    @pl.when(i == 0)
    def _():
        _load_states(sa_in, sb_in, sa_ref, sb_ref)

    q = q_ref[0]
    v = v_ref[0]
    log_f, k = _hgrn_gates(fbw_ref[0], _lower_bound(lbl_ref[...], layer))
    o_a = oa_ref[0] + _hgrn_dir(q, v, log_f, k, sa_ref, c_ref, reverse=True)
    an = an_ref[...]
    o_a = jnp.concatenate(
        [_rms(o_a[:, h * A_DV:(h + 1) * A_DV]) * an[:, h * A_DV:(h + 1) * A_DV] for h in range(A_HEADS)], axis=1)
    o_a = o_a * _silu(ga_ref[0])

    xbc = xbc_ref[0]
    y_b = yb_ref[0] + _ssd_dir(xbc[:, 0:B_WIDTH], xbc[:, B_WIDTH:B_WIDTH + B_GN], xbc[:, B_WIDTH + B_GN:B_CONV_CH],
                               dtbw_ref[0], dtb_ref[...], alog_ref[...], sb_ref, reverse=True)
    y_b = y_b * _silu(z_ref[0])
    bn = bn_ref[...]
    gw = B_WIDTH // B_GROUPS
    y_b = jnp.concatenate(
        [_rms(y_b[:, g * gw:(g + 1) * gw]) * bn[:, g * gw:(g + 1) * gw] for g in range(B_GROUPS)], axis=1)

    gates = _sigmoid(gates_ref[0])
    merged = (gates[:, 0:D_MODEL] * _mm(o_a.astype(BF16), wa_ref[...])
              + gates[:, D_MODEL:2 * D_MODEL] * _mm(y_b.astype(BF16), wb_ref[...]))
    mix = _mm(merged.astype(BF16), wo_ref[...])
    gate1 = mod_ref[0][:, 2 * D_MODEL:3 * D_MODEL]
    x_out[0] = x_ref[0] + gate1 * mix

    if emit_state:
        @pl.when(i == pl.num_programs(1) - 1)
        def _():
            _store_states(sa_out, sb_out, sa_ref, sb_ref)


def _bwd_call(x, mod, fwd_outs, lp, layer, mod_row0, mod_stride, init, emit_state):
    bsz, length, _ = x.shape
    nt = length // TILE
    tok = functools.partial(_tok_spec, nt=nt, reverse=True)
    widths = (A_KW, A_WIDTH, A_KW, B_CONV_CH, LANES, A_WIDTH, B_WIDTH, 2 * D_MODEL, A_WIDTH, B_WIDTH)
    in_specs = ([tok(D_MODEL), _mod_spec(mod_row0, mod_stride)] + [tok(w) for w in widths]
                + [_const_spec((DEPTH, A_KW)), _const_spec((1, LANES)), _const_spec((1, LANES)),
                   _const_spec((1, A_WIDTH)), _const_spec((1, B_WIDTH)), _const_spec((A_WIDTH, D_MODEL)),
                   _const_spec((B_WIDTH, D_MODEL)), _const_spec((D_MODEL, D_MODEL))])
    args = ([x, mod] + list(fwd_outs)
            + [lp['lbl'][1], lp['dt_bias'][1], lp['a_log'][1], lp['a_norm'], lp['b_norm'], lp['w_br_a'],
               lp['w_br_b'], lp['w_out']])
    if init is not None:
        in_specs += _state_specs()
        args += list(init)
    out_specs = [tok(D_MODEL)]
    out_shape = [jax.ShapeDtypeStruct((bsz, length, D_MODEL), F32)]
    if emit_state:
        out_specs += _state_specs()
        out_shape += _state_shapes(bsz)
    return pl.pallas_call(
        functools.partial(_bwd_kernel, layer=layer, has_init=init is not None, emit_state=emit_state),
        grid=(bsz, nt), in_specs=in_specs, out_specs=out_specs, out_shape=out_shape,
        scratch_shapes=_state_scratch(),
        compiler_params=pltpu.CompilerParams(dimension_semantics=("arbitrary", "arbitrary"),
                                             vmem_limit_bytes=VMEM_LIMIT),
        name="mixer_bwd",
    )(*args)


def _ffn_kernel(x_ref, mod_ref, ln_ref, wu_ref, cw_ref, cb_ref, wd_ref, lnf_ref, o_ref, *, row_block, final_norm):
    x = x_ref[0]
    n = x.shape[0]
    mod = mod_ref[0]
    shift, scale, gate = (mod[:, 3 * D_MODEL:4 * D_MODEL], mod[:, 4 * D_MODEL:5 * D_MODEL],
                          mod[:, 5 * D_MODEL:6 * D_MODEL])
    u = ((_rms(x) * ln_ref[...]) * (1.0 + scale) + shift).astype(BF16)
    r = _iota((n, FF_BLOCK), 0) & (row_block - 1)
    first = r == 0
    last = r == row_block - 1

    def conv(lo):
        h = _mm(u, wu_ref[:, lo:lo + FF_BLOCK])
        cw = cw_ref[:, lo:lo + FF_BLOCK]
        prev = jnp.where(first, 0.0, pltpu.roll(h, 1, axis=0))
        nxt = jnp.where(last, 0.0, pltpu.roll(h, n - 1, axis=0))
        return cb_ref[:, lo:lo + FF_BLOCK] + prev * cw[0:1] + h * cw[1:2] + nxt * cw[2:3]

    acc = jnp.zeros((n, D_MODEL), F32)
    for j in range(D_FF // FF_BLOCK):
        act = _silu(conv(j * FF_BLOCK)) * conv(D_FF + j * FF_BLOCK)
        acc = acc + _mm(act.astype(BF16), wd_ref[j * FF_BLOCK:(j + 1) * FF_BLOCK, :])
    y = x + gate * acc
    if final_norm:
        y = _rms(y) * lnf_ref[...]
    o_ref[0] = y


def _ffn_call(x, mod, lp, ln_f, row_block, mod_row0, mod_stride, final_norm):
    bsz, length, _ = x.shape
    nt = length // TILE
    tok = _tok_spec(D_MODEL, nt, False)
    return pl.pallas_call(
        functools.partial(_ffn_kernel, row_block=row_block, final_norm=final_norm),
        grid=(bsz, nt),
        in_specs=[tok, _mod_spec(mod_row0, mod_stride), _const_spec((1, D_MODEL)), _const_spec((D_MODEL, 2 * D_FF)),
                  _const_spec((3, 2 * D_FF)), _const_spec((1, 2 * D_FF)), _const_spec((D_FF, D_MODEL)),
                  _const_spec((1, D_MODEL))],
        out_specs=tok,
        out_shape=jax.ShapeDtypeStruct((bsz, length, D_MODEL), F32),
        compiler_params=pltpu.CompilerParams(dimension_semantics=("arbitrary", "arbitrary"),
                                             vmem_limit_bytes=VMEM_LIMIT),
        name="conv_ffn",
    )(x, mod, lp['ln2'], lp['w_ff_up'], lp['ff_conv_w'], lp['ff_conv_b'], lp['w_ff_down'], ln_f)


def _pad_heads(a):
    return jnp.pad(a.astype(F32), [(0, 0)] * (a.ndim - 1) + [(0, LANES - a.shape[-1])])


def _pack_w_in(w):
    src_gates = C_XBC + B_CONV_CH + 2 * B_HEADS
    pad = jnp.zeros((w.shape[0], LANES - B_HEADS), w.dtype)
    return jnp.concatenate([w[:, :C_GATES], w[:, src_gates:], w[:, C_GATES:C_GATES + B_HEADS], pad,
                            w[:, C_GATES + B_HEADS:src_gates], pad], axis=1).astype(BF16)


def _hgrn_state_in(s):
    return jnp.swapaxes(s.astype(F32), -1, -2)


def _ssd_state_in(s):
    b = s.shape[0]
    s = s.astype(F32).reshape(b, B_GROUPS, B_HEADS // B_GROUPS, B_HEADDIM, B_STATE)
    return s.transpose(0, 1, 4, 2, 3).reshape(b, B_GROUPS, B_STATE, 256)


def _ssd_state_out(s):
    b = s.shape[0]
    s = s.reshape(b, B_GROUPS, B_STATE, B_HEADS // B_GROUPS, B_HEADDIM)
    return s.transpose(0, 1, 3, 4, 2).reshape(b, B_HEADS, B_HEADDIM, B_STATE)


def kernel(x_prompt, x_sample, c, state_hgrn, state_ssd, c_ctx, w_ada, b_ada, ln1, ln2, ln_f, w_in, lb_logits,
           a_norm, conv_w, conv_b, dt_bias, a_log, d_skip, b_norm, w_br_a, w_br_b, w_out, w_ff_up, ff_conv_w,
           ff_conv_b, w_ff_down):
    depth = w_in.shape[0]
    dec_b = x_sample.shape[0]
    assert depth == DEPTH and dec_b + 1 <= MOD_ROWS
    assert x_prompt.shape[1] == TILE and x_sample.shape[1] % TILE == 0 and TILE % GRID_W == 0

    cvec = jnp.concatenate([c_ctx[None].astype(F32), c.astype(F32),
                            jnp.zeros((MOD_ROWS - 1 - dec_b, D_MODEL), F32)], axis=0)
    mod_all = _modulation(cvec, w_ada.astype(F32), b_ada.astype(F32))

    layers = []
    for l in range(depth):
        layers.append(dict(
            ln1=ln1[l][None].astype(F32), ln2=ln2[l][None].astype(F32),
            w_in=_pack_w_in(w_in[l]),
            lbl=(lb_logits[:, 0, :].astype(F32), lb_logits[:, 1, :].astype(F32)),
            a_norm=a_norm[l][None].astype(F32), b_norm=b_norm[l][None].astype(F32),
            conv_w=conv_w[l].astype(F32), conv_b=conv_b[l][None].astype(F32),
            dt_bias=(_pad_heads(dt_bias[l, 0])[None], _pad_heads(dt_bias[l, 1])[None]),
            a_log=(_pad_heads(a_log[l, 0])[None], _pad_heads(a_log[l, 1])[None]),
            d_skip=jnp.repeat(d_skip[l].astype(F32), B_HEADDIM)[None],
            w_br_a=w_br_a[l].astype(BF16), w_br_b=w_br_b[l].astype(BF16), w_out=w_out[l].astype(BF16),
            w_ff_up=w_ff_up[l].astype(BF16), ff_conv_w=ff_conv_w[l].astype(F32),
            ff_conv_b=ff_conv_b[l][None].astype(F32), w_ff_down=w_ff_down[l].astype(BF16)))
    ln_f2 = ln_f[None].astype(F32)

    def run(x, row_block, mod_row0, mod_stride, states, emit_state):
        fins = []
        for l in range(depth):
            lp = layers[l]
            mod = mod_all[l].reshape(MOD_ROWS, 1, MOD_COLS)
            init_f = init_b = None
            if states is not None:
                init_f = (_hgrn_state_in(states[0][:, l, 0]), _ssd_state_in(states[1][:, l, 0]))
                init_b = (_hgrn_state_in(states[0][:, l, 1]), _ssd_state_in(states[1][:, l, 1]))
            fo = _fwd_call(x, mod, lp, l, row_block, mod_row0, mod_stride, init_f, emit_state)
            bo = _bwd_call(x, mod, fo[:10], lp, l, mod_row0, mod_stride, init_b, emit_state)
            if emit_state:
                fins.append((fo[10], fo[11], bo[1], bo[2]))
            x = _ffn_call(bo[0], mod, lp, ln_f2, row_block, mod_row0, mod_stride, l == depth - 1)
        return x, fins

    y_prompt, fins = run(x_prompt.astype(F32), x_prompt.shape[1], 0, 0, None, True)
    y_sample, _ = run(x_sample.astype(F32), GRID_W, 1, 1, (state_hgrn, state_ssd), False)

    new_hgrn = jnp.stack([jnp.stack([jnp.swapaxes(f[0], -1, -2), jnp.swapaxes(f[2], -1, -2)], axis=1)
                          for f in fins], axis=1)
    new_ssd = jnp.stack([jnp.stack([_ssd_state_out(f[1]), _ssd_state_out(f[3])], axis=1) for f in fins], axis=1)
    return (y_prompt.astype(x_prompt.dtype), y_sample.astype(x_sample.dtype),
            new_hgrn.astype(x_prompt.dtype), new_ssd.astype(x_prompt.dtype))
```

```python
import functools

import jax
import jax.numpy as jnp
from jax import lax
from jax.experimental import pallas as pl
from jax.experimental.pallas import tpu as pltpu

F32 = jnp.float32
BF16 = jnp.bfloat16

D_MODEL = 1024
DEPTH = 4
GRID_W = 64
A_HEADS = 4
A_DK = 128
A_DV = 128
A_KW = A_HEADS * A_DK
A_WIDTH = A_HEADS * A_DV
B_HEADS = 16
B_HEADDIM = 64
B_WIDTH = B_HEADS * B_HEADDIM
B_GROUPS = 4
B_STATE = 64
B_GN = B_GROUPS * B_STATE
B_CONV_CH = B_WIDTH + 2 * B_GN
D_FF = 2816
N_MOD = 6
EPS = 1e-6
LOG2E = 1.4426950408889634

LANES = 128
TILE = 256
MOD_ROWS = 8
MOD_COLS = N_MOD * D_MODEL
MOD_BLOCK = 1536
FF_BLOCK = 256
FILL_COLS = 512
VMEM_LIMIT = 56 * 1024 * 1024

C_Q = 0
C_I = C_Q + A_KW
C_FFW = C_I + A_WIDTH
C_FBW = C_FFW + A_KW
C_GA = C_FBW + A_KW
C_Z = C_GA + A_WIDTH
C_XBC = C_Z + B_WIDTH
C_GATES = C_XBC + B_CONV_CH
C_DTFW = C_GATES + 2 * D_MODEL
C_DTBW = C_DTFW + LANES
IN_PACKED = C_DTBW + LANES

NT_DIMS = (((1,), (1,)), ((), ()))
TN_DIMS = (((0,), (0,)), ((), ()))


def _mm(a, b):
    return jnp.dot(a, b, preferred_element_type=F32)


def _mm_nt(a, b):
    return lax.dot_general(a, b, NT_DIMS, preferred_element_type=F32)


def _mm_tn(a, b):
    return lax.dot_general(a, b, TN_DIMS, preferred_element_type=F32)


def _sigmoid(x):
    return 1.0 / (1.0 + jnp.exp(-x))


def _silu(x):
    return x * _sigmoid(x)


def _softplus(x):
    return jnp.maximum(x, 0.0) + jnp.log1p(jnp.exp(-jnp.abs(x)))


def _rms(x):
    return x * lax.rsqrt(jnp.mean(x * x, axis=-1, keepdims=True) + EPS)


def _iota(shape, dim):
    return lax.broadcasted_iota(jnp.int32, shape, dim)


def _scan_rows(g, reverse):
    n = g.shape[0]
    r = _iota(g.shape, 0)
    y = g
    d = 1
    while d < n:
        if reverse:
            y = y + jnp.where(r < n - d, pltpu.roll(y, n - d, axis=0), 0.0)
        else:
            y = y + jnp.where(r >= d, pltpu.roll(y, d, axis=0), 0.0)
        d *= 2
    return y


def _level_ref(c, read_row, m, off):
    n, w = c.shape
    if 2 * m < 8:
        rk = _iota(c.shape, 0) & (2 * m - 1)
        out = c
        for k in range(2 * m):
            if k != off:
                out = jnp.where(rk == k, pltpu.roll(c, (k - off) % n, axis=0), out)
        return out
    pieces = [jnp.broadcast_to(read_row(j * 2 * m + off), (2 * m, w)) for j in range(n // (2 * m))]
    return pieces[0] if len(pieces) == 1 else jnp.concatenate(pieces, axis=0)


def _neg_abs(x):
    bits = lax.bitcast_convert_type(x, jnp.uint32) | jnp.uint32(0x80000000)
    return lax.bitcast_convert_type(bits, F32)


def _pair_level(n, reverse):
    t = _iota((n, n), 0)
    s = _iota((n, n), 1)
    x = t ^ s
    lvl = jnp.full((n, n), -1, jnp.int32)
    for l in range(n.bit_length() - 1):
        lvl = lvl + (x >= (1 << l)).astype(jnp.int32)
    return jnp.where((s > t) if reverse else (s < t), lvl, -1)


def _hgrn_gates(fx, lb):
    log_sig = jnp.minimum(fx, 0.0) - jnp.log1p(jnp.exp(-jnp.abs(fx)))
    a = jnp.log(lb)
    b = jnp.log1p(-lb) + log_sig
    log_f = jnp.maximum(a, b) + jnp.log1p(jnp.exp(-jnp.abs(a - b)))
    return log_f, (1.0 - lb) * _sigmoid(-fx)


def _lower_bound(logits, layer):
    e = jnp.exp(logits - jnp.max(logits, axis=0, keepdims=True))
    sm = e / jnp.sum(e, axis=0, keepdims=True)
    acc = jnp.zeros_like(sm[0:1])
    for j in range(1, layer + 1):
        acc = acc + sm[j:j + 1]
    return jnp.maximum(acc, 0.0)


def _hgrn_dir(q, v, log2_f, k, st_ref, c_ref, reverse, fill):
    n = q.shape[0]
    hb = n // 2
    top = n.bit_length() - 2
    lvl = _pair_level(hb, reverse)
    ones = jnp.ones((LANES, LANES), BF16)
    outs = []
    for h in range(A_HEADS):
        sl = slice(h * LANES, (h + 1) * LANES)
        qh, vh, kh = q[:, sl], v[:, sl], k[:, sl]
        c = _scan_rows(log2_f[:, sl], reverse)
        c_ref[h] = c
        read_row = lambda r, h=h: c_ref[h, pl.ds(r, 1), :]
        s_lo = s_hi = None
        for l in range(top + 1):
            m = 1 << l
            w = jnp.exp2(_neg_abs(c - _level_ref(c, read_row, m, m if reverse else m - 1)))
            qt = (qh * w).astype(BF16)
            kt = (kh * w).astype(BF16)
            if l < top:
                a = _mm_nt(qt[:hb], kt[:hb])
                b = _mm_nt(qt[hb:], kt[hb:])
                s_lo = jnp.where(lvl == l, a, 0.0 if s_lo is None else s_lo)
                s_hi = jnp.where(lvl == l, b, 0.0 if s_hi is None else s_hi)
            elif reverse:
                s_off = _mm_nt(qt[:hb], kt[hb:])
            else:
                s_off = _mm_nt(qt[hb:], kt[:hb])
        vb = vh.astype(BF16)
        if reverse:
            o = jnp.concatenate([_mm(jnp.concatenate([s_lo, s_off], axis=1).astype(BF16), vb),
                                 _mm(s_hi.astype(BF16), vb[hb:])], axis=0)
        else:
            o = jnp.concatenate([_mm(s_lo.astype(BF16), vb[:hb]),
                                 _mm(jnp.concatenate([s_off, s_hi], axis=1).astype(BF16), vb)], axis=0)
        c_end = read_row(0 if reverse else n - 1)
        st = st_ref[h]
        o = o + _mm((qh * kh).astype(BF16), ones) * vh
        o = o + _mm_nt((qh * jnp.exp2(c)).astype(BF16), st.astype(BF16))
        st_ref[h] = st * jnp.exp2(c_end) + _mm_tn(vb, (kh * jnp.exp2(c_end - c)).astype(BF16))
        outs.append(o)
        fill()
    return jnp.concatenate(outs, axis=1)


def _expand_heads(a):
    r = a.shape[0]
    lane = _iota((r, LANES), 1)
    slabs = []
    for j in range(B_HEADS // 2):
        lo = jnp.broadcast_to(a[:, 2 * j:2 * j + 1], (r, LANES))
        hi = jnp.broadcast_to(a[:, 2 * j + 1:2 * j + 2], (r, LANES))
        slabs.append(jnp.where(lane < B_HEADDIM, lo, hi))
    return jnp.concatenate(slabs, axis=1)


def _ssd_dir(xs, bm, cm, dt_raw, dt_bias, a_log, st_ref, reverse, fill):
    n = xs.shape[0]
    dt = _softplus(dt_raw + dt_bias)
    ac = _scan_rows((-LOG2E) * jnp.exp(a_log) * dt, reverse)
    ac_t = ac.T
    dt_t = dt.T
    row_s = _iota((n, n), 0)
    col_s = _iota((n, n), 1)
    tri = (col_s >= row_s) if reverse else (col_s <= row_s)
    lane_gn = _iota((n, B_GN), 1)
    lane = _iota((n, LANES), 1)
    xb = xs.astype(BF16)
    bb = bm.astype(BF16)
    slabs = []
    for g in range(B_GROUPS):
        cb = _mm_nt(jnp.where((lane_gn >> 6) == g, cm, 0.0).astype(BF16), bb)
        for jp in range(2):
            slab = xb[:, (2 * g + jp) * LANES:(2 * g + jp + 1) * LANES]
            pair = []
            for e in range(2):
                h = 4 * g + 2 * jp + e
                seg = ac[:, h:h + 1] - ac_t[h:h + 1, :]
                w = jnp.exp2(jnp.where(tri, seg, -1e30)) * dt_t[h:h + 1, :] * cb
                pair.append(_mm(w.astype(BF16), slab))
            slabs.append(jnp.where(lane < B_HEADDIM, pair[0], pair[1]))
        fill()
    y = jnp.concatenate(slabs, axis=1)
    st = st_ref[...]
    y = y + _mm(cm.astype(BF16), st.astype(BF16)) * _expand_heads(jnp.exp2(ac))
    a_end = ac[0:1, :] if reverse else ac[n - 1:n, :]
    wgt = _expand_heads(jnp.exp2(a_end - ac) * dt)
    upd = _mm_tn(bb, (xs * wgt).astype(BF16))
    block = (_iota(st.shape, 0) >> 6) == (_iota(st.shape, 1) >> 8)
    st_ref[...] = jnp.where(block, st * _expand_heads(jnp.exp2(a_end)) + upd, 0.0)
    return y


def _load_states(sa_in, sb_in, sa_ref, sb_ref):
    if sa_in is None:
        sa_ref[...] = jnp.zeros_like(sa_ref)
        sb_ref[...] = jnp.zeros_like(sb_ref)
    else:
        sa_ref[...] = sa_in[0]
        sb_ref[...] = jnp.zeros_like(sb_ref)
        for g in range(B_GROUPS):
            sb_ref[g * B_STATE:(g + 1) * B_STATE, g * 256:(g + 1) * 256] = sb_in[0, g]


def _store_states(sa_out, sb_out, sa_ref, sb_ref):
    sa_out[0] = sa_ref[...]
    for g in range(B_GROUPS):
        sb_out[0, g] = sb_ref[g * B_STATE:(g + 1) * B_STATE, g * 256:(g + 1) * 256]


def _mod_kernel(c_ref, w_ref, b_ref, o_ref):
    o_ref[0] = jnp.dot(_silu(c_ref[...]), w_ref[0], preferred_element_type=F32,
                       precision=lax.Precision.HIGHEST) + b_ref[0]


def _modulation(cvec, w_ada, b_ada):
    depth = w_ada.shape[0]
    return pl.pallas_call(
        _mod_kernel,
        grid=(depth, MOD_COLS // MOD_BLOCK),
        in_specs=[
            pl.BlockSpec((MOD_ROWS, D_MODEL), lambda l, j: (0, 0)),
            pl.BlockSpec((1, D_MODEL, MOD_BLOCK), lambda l, j: (l, 0, j)),
            pl.BlockSpec((1, 1, MOD_BLOCK), lambda l, j: (l, 0, j)),
        ],
        out_specs=pl.BlockSpec((1, MOD_ROWS, MOD_BLOCK), lambda l, j: (l, 0, j)),
        out_shape=jax.ShapeDtypeStruct((depth, MOD_ROWS, MOD_COLS), F32),
        compiler_params=pltpu.CompilerParams(dimension_semantics=("arbitrary", "arbitrary"),
                                             vmem_limit_bytes=VMEM_LIMIT),
        name="adaln_mod",
    )(cvec, w_ada, b_ada.reshape(depth, 1, MOD_COLS))


def _fwd_kernel(*refs, layer, row_block, has_init, emit_state):
    (x_ref, mod_ref, ln_ref, w_ref, lbl_ref, cw_ref, cb_ref, dtb_ref, alog_ref, dskip_ref), refs = refs[:10], refs[10:]
    if has_init:
        (sa_in, sb_in), refs = refs[:2], refs[2:]
    else:
        sa_in = sb_in = None
    (q_out, v_out, fbw_out, xbc_out, dtbw_out, ga_out, z_out, gates_out, oa_out, yb_out), refs = refs[:10], refs[10:]
    if emit_state:
        (sa_out, sb_out), refs = refs[:2], refs[2:]
    sa_ref, sb_ref, c_ref = refs

    i = pl.program_id(1)

    @pl.when(i == 0)
    def _():
        _load_states(sa_in, sb_in, sa_ref, sb_ref)

    x = x_ref[0]
    n = x.shape[0]
    mod = mod_ref[0]
    shift, scale = mod[:, 0:D_MODEL], mod[:, D_MODEL:2 * D_MODEL]
    u = ((_rms(x) * ln_ref[...]) * (1.0 + scale) + shift).astype(BF16)

    def proj(lo, hi):
        return _mm(u, w_ref[:, lo:hi])

    pending = [(ga_out, 0, C_GA, A_WIDTH), (fbw_out, 0, C_FBW, A_KW)]
    pending += [(z_out, o, C_Z + o, FILL_COLS) for o in range(0, B_WIDTH, FILL_COLS)]
    pending += [(gates_out, o, C_GATES + o, FILL_COLS) for o in range(0, 2 * D_MODEL, FILL_COLS)]
    pending.append((dtbw_out, 0, C_DTBW, LANES))
    pending.reverse()

    def fill():
        if pending:
            ref, dst, src, width = pending.pop()
            ref[0, :, dst:dst + width] = proj(src, src + width)

    q = proj(C_Q, C_I)
    v = proj(C_I, C_FFW)
    fx = proj(C_FFW, C_FBW)
    xbc = proj(C_XBC, C_GATES)
    dt_raw = proj(C_DTFW, C_DTBW)
    fill()
    q = _silu(q)
    q_out[0] = q
    v_out[0] = v
    log_f, k = _hgrn_gates(fx, _lower_bound(lbl_ref[...], layer))
    fill()

    r = _iota(xbc.shape, 0) & (row_block - 1)
    prev = jnp.where(r == 0, 0.0, pltpu.roll(xbc, 1, axis=0))
    nxt = jnp.where(r == row_block - 1, 0.0, pltpu.roll(xbc, n - 1, axis=0))
    cw = cw_ref[...]
    xbc = _silu(cb_ref[...] + prev * cw[0:1] + xbc * cw[1:2] + nxt * cw[2:3])
    xbc_out[0] = xbc

    oa_out[0] = _hgrn_dir(q, v, log_f * LOG2E, k, sa_ref, c_ref, False, fill)

    xs = xbc[:, 0:B_WIDTH]
    y = _ssd_dir(xs, xbc[:, B_WIDTH:B_WIDTH + B_GN], xbc[:, B_WIDTH + B_GN:B_CONV_CH],
                 dt_raw, dtb_ref[...], alog_ref[...], sb_ref, False, fill)
    yb_out[0] = y + dskip_ref[...] * xs
    while pending:
        fill()

    if emit_state:
        @pl.when(i == pl.num_programs(1) - 1)
        def _():
            _store_states(sa_out, sb_out, sa_ref, sb_ref)


def _const_spec(shape):
    nd = len(shape)
    return pl.BlockSpec(shape, lambda b, i: (0,) * nd, pipeline_mode=pl.Buffered(1))


def _tok_spec(width, nt, reverse):
    if reverse:
        return pl.BlockSpec((1, TILE, width), lambda b, i: (b, nt - 1 - i, 0))
    return pl.BlockSpec((1, TILE, width), lambda b, i: (b, i, 0))


def _state_specs():
    return [pl.BlockSpec((1, A_HEADS, A_DV, A_DK), lambda b, i: (b, 0, 0, 0)),
            pl.BlockSpec((1, B_GROUPS, B_STATE, 256), lambda b, i: (b, 0, 0, 0))]


def _state_shapes(bsz):
    return [jax.ShapeDtypeStruct((bsz, A_HEADS, A_DV, A_DK), F32),
            jax.ShapeDtypeStruct((bsz, B_GROUPS, B_STATE, 256), F32)]


def _state_scratch():
    return [pltpu.VMEM((A_HEADS, A_DV, A_DK), F32), pltpu.VMEM((B_GN, B_WIDTH), F32),
            pltpu.VMEM((A_HEADS, TILE, LANES), F32)]


def _mod_spec(mod_row0, mod_stride):
    return pl.BlockSpec((1, 1, MOD_COLS), lambda b, i: (mod_row0 + mod_stride * b, 0, 0))


def _fwd_call(x, mod, lp, layer, row_block, mod_row0, mod_stride, init, emit_state):
    bsz, length, _ = x.shape
    nt = length // TILE
    tok = functools.partial(_tok_spec, nt=nt, reverse=False)
    in_specs = [tok(D_MODEL), _mod_spec(mod_row0, mod_stride), _const_spec((1, D_MODEL)),
                _const_spec((D_MODEL, IN_PACKED)), _const_spec((DEPTH, A_KW)), _const_spec((3, B_CONV_CH)),
                _const_spec((1, B_CONV_CH)), _const_spec((1, LANES)), _const_spec((1, LANES)),
                _const_spec((1, B_WIDTH))]
    args = [x, mod, lp['ln1'], lp['w_in'], lp['lbl'][0], lp['conv_w'], lp['conv_b'], lp['dt_bias'][0],
            lp['a_log'][0], lp['d_skip']]
    if init is not None:
        in_specs += _state_specs()
        args += list(init)
    widths = (A_KW, A_WIDTH, A_KW, B_CONV_CH, LANES, A_WIDTH, B_WIDTH, 2 * D_MODEL, A_WIDTH, B_WIDTH)
    out_specs = [tok(w) for w in widths]
    out_shape = [jax.ShapeDtypeStruct((bsz, length, w), F32) for w in widths]
    if emit_state:
        out_specs += _state_specs()
        out_shape += _state_shapes(bsz)
    return pl.pallas_call(
        functools.partial(_fwd_kernel, layer=layer, row_block=row_block, has_init=init is not None,
                          emit_state=emit_state),
        grid=(bsz, nt), in_specs=in_specs, out_specs=out_specs, out_shape=out_shape,
        scratch_shapes=_state_scratch(),
        compiler_params=pltpu.CompilerParams(dimension_semantics=("arbitrary", "arbitrary"),
                                             vmem_limit_bytes=VMEM_LIMIT),
        name="mixer_fwd",
    )(*args)


def _bwd_kernel(*refs, layer, has_init, emit_state):
    (x_ref, mod_ref, q_ref, v_ref, fbw_ref, xbc_ref, dtbw_ref, ga_ref, z_ref, gates_ref, oa_ref, yb_ref,
     lbl_ref, dtb_ref, alog_ref, an_ref, bn_ref, wa_ref, wb_ref, wo_ref), refs = refs[:20], refs[20:]
    if has_init:
        (sa_in, sb_in), refs = refs[:2], refs[2:]
    else:
        sa_in = sb_in = None
    x_out, refs = refs[0], refs[1:]
    if emit_state:
        (sa_out, sb_out), refs = refs[:2], refs[2:]
    sa_ref, sb_ref, c_ref = refs

    i = pl.program_id(1)

    @pl.when(i == 0)
    def _():
        _load_states(sa_in, sb_in, sa_ref, sb_ref)

    q = q_ref[0]
    v = v_ref[0]
    log_f, k = _hgrn_gates(fbw_ref[0], _lower_bound(lbl_ref[...], layer))
    no_fill = lambda: None
    o_a = oa_ref[0] + _hgrn_dir(q, v, log_f * LOG2E, k, sa_ref, c_ref, True, no_fill)
    an = an_ref[...]
    o_a = jnp.concatenate(
        [_rms(o_a[:, h * A_DV:(h + 1) * A_DV]) * an[:, h * A_DV:(h + 1) * A_DV] for h in range(A_HEADS)], axis=1)
    o_a = o_a * _silu(ga_ref[0])
    gates = _sigmoid(gates_ref[0])
    merged_a = gates[:, 0:D_MODEL] * _mm(o_a.astype(BF16), wa_ref[...])

    xbc = xbc_ref[0]
    y_b = yb_ref[0] + _ssd_dir(xbc[:, 0:B_WIDTH], xbc[:, B_WIDTH:B_WIDTH + B_GN], xbc[:, B_WIDTH + B_GN:B_CONV_CH],
                               dtbw_ref[0], dtb_ref[...], alog_ref[...], sb_ref, True, no_fill)
    y_b = y_b * _silu(z_ref[0])
    bn = bn_ref[...]
    gw = B_WIDTH // B_GROUPS
    y_b = jnp.concatenate(
        [_rms(y_b[:, g * gw:(g + 1) * gw]) * bn[:, g * gw:(g + 1) * gw] for g in range(B_GROUPS)], axis=1)

    merged = merged_a + gates[:, D_MODEL:2 * D_MODEL] * _mm(y_b.astype(BF16), wb_ref[...])
    mix = _mm(merged.astype(BF16), wo_ref[...])
    gate1 = mod_ref[0][:, 2 * D_MODEL:3 * D_MODEL]
    x_out[0] = x_ref[0] + gate1 * mix

    if emit_state:
        @pl.when(i == pl.num_programs(1) - 1)
        def _():
            _store_states(sa_out, sb_out, sa_ref, sb_ref)


def _bwd_call(x, mod, fwd_outs, lp, layer, mod_row0, mod_stride, init, emit_state):
    bsz, length, _ = x.shape
    nt = length // TILE
    tok = functools.partial(_tok_spec, nt=nt, reverse=True)
    widths = (A_KW, A_WIDTH, A_KW, B_CONV_CH, LANES, A_WIDTH, B_WIDTH, 2 * D_MODEL, A_WIDTH, B_WIDTH)
    in_specs = ([tok(D_MODEL), _mod_spec(mod_row0, mod_stride)] + [tok(w) for w in widths]
                + [_const_spec((DEPTH, A_KW)), _const_spec((1, LANES)), _const_spec((1, LANES)),
                   _const_spec((1, A_WIDTH)), _const_spec((1, B_WIDTH)), _const_spec((A_WIDTH, D_MODEL)),
                   _const_spec((B_WIDTH, D_MODEL)), _const_spec((D_MODEL, D_MODEL))])
    args = ([x, mod] + list(fwd_outs)
            + [lp['lbl'][1], lp['dt_bias'][1], lp['a_log'][1], lp['a_norm'], lp['b_norm'], lp['w_br_a'],
               lp['w_br_b'], lp['w_out']])
    if init is not None:
        in_specs += _state_specs()
        args += list(init)
    out_specs = [tok(D_MODEL)]
    out_shape = [jax.ShapeDtypeStruct((bsz, length, D_MODEL), F32)]
    if emit_state:
        out_specs += _state_specs()
        out_shape += _state_shapes(bsz)
    return pl.pallas_call(
        functools.partial(_bwd_kernel, layer=layer, has_init=init is not None, emit_state=emit_state),
        grid=(bsz, nt), in_specs=in_specs, out_specs=out_specs, out_shape=out_shape,
        scratch_shapes=_state_scratch(),
        compiler_params=pltpu.CompilerParams(dimension_semantics=("arbitrary", "arbitrary"),
                                             vmem_limit_bytes=VMEM_LIMIT),
        name="mixer_bwd",
    )(*args)


def _ffn_kernel(x_ref, mod_ref, ln_ref, wu_ref, cw_ref, cb_ref, wd_ref, lnf_ref, o_ref, *, row_block, final_norm):
    x = x_ref[0]
    n = x.shape[0]
    mod = mod_ref[0]
    shift, scale, gate = (mod[:, 3 * D_MODEL:4 * D_MODEL], mod[:, 4 * D_MODEL:5 * D_MODEL],
                          mod[:, 5 * D_MODEL:6 * D_MODEL])
    u = ((_rms(x) * ln_ref[...]) * (1.0 + scale) + shift).astype(BF16)
    r = _iota((n, FF_BLOCK), 0) & (row_block - 1)
    first = r == 0
    last = r == row_block - 1

    def up(j):
        return (_mm(u, wu_ref[:, j * FF_BLOCK:(j + 1) * FF_BLOCK]),
                _mm(u, wu_ref[:, D_FF + j * FF_BLOCK:D_FF + (j + 1) * FF_BLOCK]))

    def conv(h, lo):
        cw = cw_ref[:, lo:lo + FF_BLOCK]
        prev = jnp.where(first, 0.0, pltpu.roll(h, 1, axis=0))
        nxt = jnp.where(last, 0.0, pltpu.roll(h, n - 1, axis=0))
        return cb_ref[:, lo:lo + FF_BLOCK] + prev * cw[0:1] + h * cw[1:2] + nxt * cw[2:3]

    steps = D_FF // FF_BLOCK
    acc = None
    h = up(0)
    for j in range(steps):
        h_next = up(j + 1) if j + 1 < steps else None
        act = _silu(conv(h[0], j * FF_BLOCK)) * conv(h[1], D_FF + j * FF_BLOCK)
        part = _mm(act.astype(BF16), wd_ref[j * FF_BLOCK:(j + 1) * FF_BLOCK, :])
        acc = part if acc is None else acc + part
        h = h_next
    y = x + gate * acc
    if final_norm:
        y = _rms(y) * lnf_ref[...]
    o_ref[0] = y


def _ffn_call(x, mod, lp, ln_f, row_block, mod_row0, mod_stride, final_norm):
    bsz, length, _ = x.shape
    nt = length // TILE
    tok = _tok_spec(D_MODEL, nt, False)
    return pl.pallas_call(
        functools.partial(_ffn_kernel, row_block=row_block, final_norm=final_norm),
        grid=(bsz, nt),
        in_specs=[tok, _mod_spec(mod_row0, mod_stride), _const_spec((1, D_MODEL)), _const_spec((D_MODEL, 2 * D_FF)),
                  _const_spec((3, 2 * D_FF)), _const_spec((1, 2 * D_FF)), _const_spec((D_FF, D_MODEL)),
                  _const_spec((1, D_MODEL))],
        out_specs=tok,
        out_shape=jax.ShapeDtypeStruct((bsz, length, D_MODEL), F32),
        compiler_params=pltpu.CompilerParams(dimension_semantics=("arbitrary", "arbitrary"),
                                             vmem_limit_bytes=VMEM_LIMIT),
        name="conv_ffn",
    )(x, mod, lp['ln2'], lp['w_ff_up'], lp['ff_conv_w'], lp['ff_conv_b'], lp['w_ff_down'], ln_f)


def _pad_heads(a):
    return jnp.pad(a.astype(F32), [(0, 0)] * (a.ndim - 1) + [(0, LANES - a.shape[-1])])


def _pack_w_in(w):
    src_gates = C_XBC + B_CONV_CH + 2 * B_HEADS
    pad = jnp.zeros((w.shape[0], LANES - B_HEADS), w.dtype)
    return jnp.concatenate([w[:, :C_GATES], w[:, src_gates:], w[:, C_GATES:C_GATES + B_HEADS], pad,
                            w[:, C_GATES + B_HEADS:src_gates], pad], axis=1).astype(BF16)


def _hgrn_state_in(s):
    return jnp.swapaxes(s.astype(F32), -1, -2)


def _ssd_state_in(s):
    b = s.shape[0]
    s = s.astype(F32).reshape(b, B_GROUPS, B_HEADS // B_GROUPS, B_HEADDIM, B_STATE)
    return s.transpose(0, 1, 4, 2, 3).reshape(b, B_GROUPS, B_STATE, 256)


def _ssd_state_out(s):
    b = s.shape[0]
    s = s.reshape(b, B_GROUPS, B_STATE, B_HEADS // B_GROUPS, B_HEADDIM)
    return s.transpose(0, 1, 3, 4, 2).reshape(b, B_HEADS, B_HEADDIM, B_STATE)


def kernel(x_prompt, x_sample, c, state_hgrn, state_ssd, c_ctx, w_ada, b_ada, ln1, ln2, ln_f, w_in, lb_logits,
           a_norm, conv_w, conv_b, dt_bias, a_log, d_skip, b_norm, w_br_a, w_br_b, w_out, w_ff_up, ff_conv_w,
           ff_conv_b, w_ff_down):
    depth = w_in.shape[0]
    dec_b = x_sample.shape[0]
    assert depth == DEPTH and dec_b + 1 <= MOD_ROWS
    assert x_prompt.shape[1] == TILE and x_sample.shape[1] % TILE == 0 and TILE % GRID_W == 0

    cvec = jnp.concatenate([c_ctx[None].astype(F32), c.astype(F32),
                            jnp.zeros((MOD_ROWS - 1 - dec_b, D_MODEL), F32)], axis=0)
    mod_all = _modulation(cvec, w_ada.astype(F32), b_ada.astype(F32))

    layers = []
    for l in range(depth):
        layers.append(dict(
            ln1=ln1[l][None].astype(F32), ln2=ln2[l][None].astype(F32),
            w_in=_pack_w_in(w_in[l]),
            lbl=(lb_logits[:, 0, :].astype(F32), lb_logits[:, 1, :].astype(F32)),
            a_norm=a_norm[l][None].astype(F32), b_norm=b_norm[l][None].astype(F32),
            conv_w=conv_w[l].astype(F32), conv_b=conv_b[l][None].astype(F32),
            dt_bias=(_pad_heads(dt_bias[l, 0])[None], _pad_heads(dt_bias[l, 1])[None]),
            a_log=(_pad_heads(a_log[l, 0])[None], _pad_heads(a_log[l, 1])[None]),
            d_skip=jnp.repeat(d_skip[l].astype(F32), B_HEADDIM)[None],
            w_br_a=w_br_a[l].astype(BF16), w_br_b=w_br_b[l].astype(BF16), w_out=w_out[l].astype(BF16),
            w_ff_up=w_ff_up[l].astype(BF16), ff_conv_w=ff_conv_w[l].astype(F32),
            ff_conv_b=ff_conv_b[l][None].astype(F32), w_ff_down=w_ff_down[l].astype(BF16)))
    ln_f2 = ln_f[None].astype(F32)

    def run(x, row_block, mod_row0, mod_stride, states, emit_state):
        fins = []
        for l in range(depth):
            lp = layers[l]
            mod = mod_all[l].reshape(MOD_ROWS, 1, MOD_COLS)
            init_f = init_b = None
            if states is not None:
                init_f = (_hgrn_state_in(states[0][:, l, 0]), _ssd_state_in(states[1][:, l, 0]))
                init_b = (_hgrn_state_in(states[0][:, l, 1]), _ssd_state_in(states[1][:, l, 1]))
            fo = _fwd_call(x, mod, lp, l, row_block, mod_row0, mod_stride, init_f, emit_state)
            bo = _bwd_call(x, mod, fo[:10], lp, l, mod_row0, mod_stride, init_b, emit_state)
            if emit_state:
                fins.append((fo[10], fo[11], bo[1], bo[2]))
            x = _ffn_call(bo[0], mod, lp, ln_f2, row_block, mod_row0, mod_stride, l == depth - 1)
        return x, fins

    y_prompt, fins = run(x_prompt.astype(F32), x_prompt.shape[1], 0, 0, None, True)
    y_sample, _ = run(x_sample.astype(F32), GRID_W, 1, 1, (state_hgrn, state_ssd), False)

    new_hgrn = jnp.stack([jnp.stack([jnp.swapaxes(f[0], -1, -2), jnp.swapaxes(f[2], -1, -2)], axis=1)
                          for f in fins], axis=1)
    new_ssd = jnp.stack([jnp.stack([_ssd_state_out(f[1]), _ssd_state_out(f[3])], axis=1) for f in fins], axis=1)
    return (y_prompt.astype(x_prompt.dtype), y_sample.astype(x_sample.dtype),
            new_hgrn.astype(x_prompt.dtype), new_ssd.astype(x_prompt.dtype))
```

```python
import functools

import jax
import jax.numpy as jnp
from jax import lax
from jax.experimental import pallas as pl
from jax.experimental.pallas import tpu as pltpu

F32 = jnp.float32
BF16 = jnp.bfloat16

D_MODEL = 1024
DEPTH = 4
GRID_W = 64
A_HEADS = 4
A_DK = 128
A_DV = 128
A_KW = A_HEADS * A_DK
A_WIDTH = A_HEADS * A_DV
B_HEADS = 16
B_HEADDIM = 64
B_WIDTH = B_HEADS * B_HEADDIM
B_GROUPS = 4
B_STATE = 64
B_GN = B_GROUPS * B_STATE
B_CONV_CH = B_WIDTH + 2 * B_GN
D_FF = 2816
N_MOD = 6
EPS = 1e-6
LOG2E = 1.4426950408889634

LANES = 128
TILE = 256
MOD_ROWS = 8
MOD_COLS = N_MOD * D_MODEL
MOD_BLOCK = 1536
FF_BLOCK = 256
FILL_COLS = 512
VMEM_LIMIT = 56 * 1024 * 1024

C_Q = 0
C_I = C_Q + A_KW
C_FFW = C_I + A_WIDTH
C_FBW = C_FFW + A_KW
C_GA = C_FBW + A_KW
C_Z = C_GA + A_WIDTH
C_XBC = C_Z + B_WIDTH
C_GATES = C_XBC + B_CONV_CH
C_DTFW = C_GATES + 2 * D_MODEL
C_DTBW = C_DTFW + LANES
IN_PACKED = C_DTBW + LANES

NT_DIMS = (((1,), (1,)), ((), ()))
TN_DIMS = (((0,), (0,)), ((), ()))


def _mm(a, b):
    return jnp.dot(a, b, preferred_element_type=F32)


def _mm_nt(a, b):
    return lax.dot_general(a, b, NT_DIMS, preferred_element_type=F32)


def _mm_tn(a, b):
    return lax.dot_general(a, b, TN_DIMS, preferred_element_type=F32)


def _sigmoid(x):
    return 0.5 * jnp.tanh(0.5 * x) + 0.5


def _silu(x):
    return x * _sigmoid(x)


def _softplus(x):
    return jnp.maximum(x, 0.0) + jnp.log1p(jnp.exp(-jnp.abs(x)))


def _rms(x):
    return x * lax.rsqrt(jnp.mean(x * x, axis=-1, keepdims=True) + EPS)


def _iota(shape, dim):
    return lax.broadcasted_iota(jnp.int32, shape, dim)


def _split_bf16(x, parts):
    out = []
    for i in range(parts):
        p = x.astype(BF16)
        out.append(p)
        if i + 1 < parts:
            x = x - p.astype(F32)
    return out


def _scan_rows(g, reverse):
    n = g.shape[0]
    row = _iota((n, n), 0)
    col = _iota((n, n), 1)
    tri = jnp.where((col >= row) if reverse else (col <= row), 1.0, 0.0).astype(BF16)
    return _mm(jnp.concatenate([tri, tri, tri], axis=1), jnp.concatenate(_split_bf16(g, 3), axis=0))


def _level_ref(c, read_row, m, off):
    n, w = c.shape
    if 2 * m < 8:
        rk = _iota(c.shape, 0) & (2 * m - 1)
        out = c
        for k in range(2 * m):
            if k != off:
                out = jnp.where(rk == k, pltpu.roll(c, (k - off) % n, axis=0), out)
        return out
    pieces = [jnp.broadcast_to(read_row(j * 2 * m + off), (2 * m, w)) for j in range(n // (2 * m))]
    return pieces[0] if len(pieces) == 1 else jnp.concatenate(pieces, axis=0)


def _neg_abs(x):
    bits = lax.bitcast_convert_type(x, jnp.uint32) | jnp.uint32(0x80000000)
    return lax.bitcast_convert_type(bits, F32)


def _pair_level(n, reverse):
    t = _iota((n, n), 0)
    s = _iota((n, n), 1)
    x = t ^ s
    lvl = jnp.full((n, n), -1, jnp.int32)
    for l in range(n.bit_length() - 1):
        lvl = lvl + (x >= (1 << l)).astype(jnp.int32)
    return jnp.where((s > t) if reverse else (s < t), lvl, -1)


def _hgrn_gates(fx, lb):
    log_sig = jnp.minimum(fx, 0.0) - jnp.log(1.0 + jnp.exp(-jnp.abs(fx)))
    a = jnp.log(lb)
    b = jnp.log1p(-lb) + log_sig
    log_f = jnp.maximum(a, b) + jnp.log(1.0 + jnp.exp(-jnp.abs(a - b)))
    return log_f, (1.0 - lb) * _sigmoid(-fx)


def _lower_bound(logits, layer):
    e = jnp.exp(logits - jnp.max(logits, axis=0, keepdims=True))
    sm = e / jnp.sum(e, axis=0, keepdims=True)
    acc = jnp.zeros_like(sm[0:1])
    for j in range(1, layer + 1):
        acc = acc + sm[j:j + 1]
    return jnp.maximum(acc, 0.0)


def _interleave(*gens):
    results = [None] * len(gens)
    live = list(range(len(gens)))
    while live:
        for idx in list(live):
            try:
                next(gens[idx])
            except StopIteration as stop:
                results[idx] = stop.value
                live.remove(idx)
    return results


def _hgrn_dir(q, v, log2_f, k, st_ref, c_ref, reverse):
    n = q.shape[0]
    hb = n // 2
    top = n.bit_length() - 2
    lvl = _pair_level(hb, reverse)
    ones = jnp.ones((LANES, LANES), BF16)
    c_all = _scan_rows(log2_f, reverse)
    yield
    outs = []
    for h in range(A_HEADS):
        sl = slice(h * LANES, (h + 1) * LANES)
        qh, vh, kh = q[:, sl], v[:, sl], k[:, sl]
        q16, k16 = qh.astype(BF16), kh.astype(BF16)
        c = c_all[:, sl]
        c_ref[h] = c
        read_row = lambda r, h=h: c_ref[h, pl.ds(r, 1), :]
        s_lo = s_hi = None
        for l in range(top + 1):
            m = 1 << l
            w = jnp.exp2(_neg_abs(c - _level_ref(c, read_row, m, m if reverse else m - 1))).astype(BF16)
            qt = q16 * w
            kt = k16 * w
            if l < top:
                a = _mm_nt(qt[:hb], kt[:hb])
                b = _mm_nt(qt[hb:], kt[hb:])
                s_lo = jnp.where(lvl == l, a, 0.0 if s_lo is None else s_lo)
                s_hi = jnp.where(lvl == l, b, 0.0 if s_hi is None else s_hi)
            elif reverse:
                s_off = _mm_nt(qt[:hb], kt[hb:])
            else:
                s_off = _mm_nt(qt[hb:], kt[:hb])
            yield
        vb = vh.astype(BF16)
        if reverse:
            o = jnp.concatenate([_mm(jnp.concatenate([s_lo, s_off], axis=1).astype(BF16), vb),
                                 _mm(s_hi.astype(BF16), vb[hb:])], axis=0)
        else:
            o = jnp.concatenate([_mm(s_lo.astype(BF16), vb[:hb]),
                                 _mm(jnp.concatenate([s_off, s_hi], axis=1).astype(BF16), vb)], axis=0)
        c_end = read_row(0 if reverse else n - 1)
        st = st_ref[h]
        o = o + _mm((qh * kh).astype(BF16), ones) * vh
        o = o + _mm_nt((qh * jnp.exp2(c)).astype(BF16), st.astype(BF16))
        st_ref[h] = st * jnp.exp2(c_end) + _mm_tn(vb, (kh * jnp.exp2(c_end - c)).astype(BF16))
        outs.append(o)
        yield
    return jnp.concatenate(outs, axis=1)


def _expand_heads(a):
    r = a.shape[0]
    lane = _iota((r, LANES), 1)
    slabs = []
    for j in range(B_HEADS // 2):
        lo = jnp.broadcast_to(a[:, 2 * j:2 * j + 1], (r, LANES))
        hi = jnp.broadcast_to(a[:, 2 * j + 1:2 * j + 2], (r, LANES))
        slabs.append(jnp.where(lane < B_HEADDIM, lo, hi))
    return jnp.concatenate(slabs, axis=1)


def _expand_heads_mxu(a, e2):
    a = jnp.where(_iota(a.shape, 1) < B_HEADS, a, 0.0)
    return _mm(jnp.concatenate(_split_bf16(a, 2), axis=1), e2)


def _ssd_dir(xs, bm, cm, dt_raw, dt_bias, a_log, e2, st_ref, reverse):
    n = xs.shape[0]
    dt = _softplus(dt_raw + dt_bias)
    ac = _scan_rows((-LOG2E) * jnp.exp(a_log) * dt, reverse)
    ac_t = ac.T
    dt_t = dt.T
    hb = n // 2
    row_s = _iota((hb, hb), 0)
    col_s = _iota((hb, hb), 1)
    tri = (col_s >= row_s) if reverse else (col_s <= row_s)
    lane_gn = _iota((n, B_GN), 1)
    lane = _iota((n, LANES), 1)
    xb = xs.astype(BF16)
    bb = bm.astype(BF16)
    lo, hi = slice(0, hb), slice(hb, n)
    yield
    slabs = []
    for g in range(B_GROUPS):
        cb = _mm_nt(jnp.where((lane_gn >> 6) == g, cm, 0.0).astype(BF16), bb)
        yield
        for jp in range(2):
            slab = xb[:, (2 * g + jp) * LANES:(2 * g + jp + 1) * LANES]
            pair = []
            for e in range(2):
                h = 4 * g + 2 * jp + e
                col = ac[:, h:h + 1]
                row = ac_t[h:h + 1, :]
                dt_row = dt_t[h:h + 1, :]

                def quad(rs, cs, masked):
                    seg = col[rs] - row[:, cs]
                    if masked:
                        seg = jnp.where(tri, seg, -1e30)
                    return (jnp.exp2(seg) * dt_row[:, cs] * cb[rs, cs]).astype(BF16)

                if reverse:
                    y_lo = _mm(jnp.concatenate([quad(lo, lo, True), quad(lo, hi, False)], axis=1), slab)
                    y_hi = _mm(quad(hi, hi, True), slab[hi])
                else:
                    y_lo = _mm(quad(lo, lo, True), slab[lo])
                    y_hi = _mm(jnp.concatenate([quad(hi, lo, False), quad(hi, hi, True)], axis=1), slab)
                pair.append(jnp.concatenate([y_lo, y_hi], axis=0))
                yield
            slabs.append(jnp.where(lane < B_HEADDIM, pair[0], pair[1]))
    y = jnp.concatenate(slabs, axis=1)
    st = st_ref[...]
    y = y + _mm(cm.astype(BF16), st.astype(BF16)) * _expand_heads_mxu(jnp.exp2(ac), e2)
    yield
    a_end = ac[0:1, :] if reverse else ac[n - 1:n, :]
    wgt = _expand_heads_mxu(jnp.exp2(a_end - ac) * dt, e2)
    upd = _mm_tn(bb, (xs * wgt).astype(BF16))
    block = (_iota(st.shape, 0) >> 6) == (_iota(st.shape, 1) >> 8)
    st_ref[...] = jnp.where(block, st * _expand_heads(jnp.exp2(a_end)) + upd, 0.0)
    return y


def _load_states(sa_in, sb_in, sa_ref, sb_ref):
    if sa_in is None:
        sa_ref[...] = jnp.zeros_like(sa_ref)
        sb_ref[...] = jnp.zeros_like(sb_ref)
    else:
        sa_ref[...] = sa_in[0]
        sb_ref[...] = jnp.zeros_like(sb_ref)
        for g in range(B_GROUPS):
            sb_ref[g * B_STATE:(g + 1) * B_STATE, g * 256:(g + 1) * 256] = sb_in[0, g]


def _store_states(sa_out, sb_out, sa_ref, sb_ref):
    sa_out[0] = sa_ref[...]
    for g in range(B_GROUPS):
        sb_out[0, g] = sb_ref[g * B_STATE:(g + 1) * B_STATE, g * 256:(g + 1) * 256]


def _mod_kernel(c_ref, w_ref, b_ref, o_ref):
    o_ref[0] = jnp.dot(_silu(c_ref[...]), w_ref[0], preferred_element_type=F32,
                       precision=lax.Precision.HIGHEST) + b_ref[0]


def _modulation(cvec, w_ada, b_ada):
    depth = w_ada.shape[0]
    return pl.pallas_call(
        _mod_kernel,
        grid=(depth, MOD_COLS // MOD_BLOCK),
        in_specs=[
            pl.BlockSpec((MOD_ROWS, D_MODEL), lambda l, j: (0, 0)),
            pl.BlockSpec((1, D_MODEL, MOD_BLOCK), lambda l, j: (l, 0, j)),
            pl.BlockSpec((1, 1, MOD_BLOCK), lambda l, j: (l, 0, j)),
        ],
        out_specs=pl.BlockSpec((1, MOD_ROWS, MOD_BLOCK), lambda l, j: (l, 0, j)),
        out_shape=jax.ShapeDtypeStruct((depth, MOD_ROWS, MOD_COLS), F32),
        compiler_params=pltpu.CompilerParams(dimension_semantics=("arbitrary", "arbitrary"),
                                             vmem_limit_bytes=VMEM_LIMIT),
        name="adaln_mod",
    )(cvec, w_ada, b_ada.reshape(depth, 1, MOD_COLS))


def _fwd_kernel(*refs, layer, row_block, has_init, emit_state):
    (x_ref, mod_ref, ln_ref, w_ref, lbl_ref, cw_ref, cb_ref, dtb_ref, alog_ref, dskip_ref,
     e2_ref), refs = refs[:11], refs[11:]
    if has_init:
        (sa_in, sb_in), refs = refs[:2], refs[2:]
    else:
        sa_in = sb_in = None
    (q_out, v_out, fbw_out, xbc_out, dtbw_out, ga_out, z_out, gates_out, oa_out, yb_out), refs = refs[:10], refs[10:]
    if emit_state:
        (sa_out, sb_out), refs = refs[:2], refs[2:]
    sa_ref, sb_ref, c_ref = refs

    i = pl.program_id(1)

    @pl.when(i == 0)
    def _():
        _load_states(sa_in, sb_in, sa_ref, sb_ref)

    x = x_ref[0]
    n = x.shape[0]
    mod = mod_ref[0]
    shift, scale = mod[:, 0:D_MODEL], mod[:, D_MODEL:2 * D_MODEL]
    u = ((_rms(x) * ln_ref[...]) * (1.0 + scale) + shift).astype(BF16)

    def proj(lo, hi):
        return _mm(u, w_ref[:, lo:hi])

    pending = [(ga_out, 0, C_GA, A_WIDTH), (fbw_out, 0, C_FBW, A_KW)]
    pending += [(z_out, o, C_Z + o, FILL_COLS) for o in range(0, B_WIDTH, FILL_COLS)]
    pending += [(gates_out, o, C_GATES + o, FILL_COLS) for o in range(0, 2 * D_MODEL, FILL_COLS)]
    pending.append((dtbw_out, 0, C_DTBW, LANES))

    def deferred():
        for ref, dst, src, width in pending:
            ref[0, :, dst:dst + width] = proj(src, src + width)
            yield
            yield
            yield

    fills = deferred()
    q = proj(C_Q, C_I)
    v = proj(C_I, C_FFW)
    fx = proj(C_FFW, C_FBW)
    xbc = proj(C_XBC, C_GATES)
    dt_raw = proj(C_DTFW, C_DTBW)
    next(fills)
    q = _silu(q)
    q_out[0] = q
    v_out[0] = v
    log_f, k = _hgrn_gates(fx, _lower_bound(lbl_ref[...], layer))
    next(fills), next(fills), next(fills)

    r = _iota(xbc.shape, 0) & (row_block - 1)
    prev = jnp.where(r == 0, 0.0, pltpu.roll(xbc, 1, axis=0))
    nxt = jnp.where(r == row_block - 1, 0.0, pltpu.roll(xbc, n - 1, axis=0))
    cw = cw_ref[...]
    xbc = _silu(cb_ref[...] + prev * cw[0:1] + xbc * cw[1:2] + nxt * cw[2:3])
    xbc_out[0] = xbc

    xs = xbc[:, 0:B_WIDTH]
    o_a, y, _ = _interleave(
        _hgrn_dir(q, v, log_f * LOG2E, k, sa_ref, c_ref, False),
        _ssd_dir(xs, xbc[:, B_WIDTH:B_WIDTH + B_GN], xbc[:, B_WIDTH + B_GN:B_CONV_CH],
                 dt_raw, dtb_ref[...], alog_ref[...], e2_ref[...], sb_ref, False),
        fills)
    oa_out[0] = o_a
    yb_out[0] = y + dskip_ref[...] * xs

    if emit_state:
        @pl.when(i == pl.num_programs(1) - 1)
        def _():
            _store_states(sa_out, sb_out, sa_ref, sb_ref)


def _const_spec(shape):
    nd = len(shape)
    return pl.BlockSpec(shape, lambda b, i: (0,) * nd, pipeline_mode=pl.Buffered(1))


def _tok_spec(width, nt, reverse):
    if reverse:
        return pl.BlockSpec((1, TILE, width), lambda b, i: (b, nt - 1 - i, 0))
    return pl.BlockSpec((1, TILE, width), lambda b, i: (b, i, 0))


def _state_specs():
    return [pl.BlockSpec((1, A_HEADS, A_DV, A_DK), lambda b, i: (b, 0, 0, 0)),
            pl.BlockSpec((1, B_GROUPS, B_STATE, 256), lambda b, i: (b, 0, 0, 0))]


def _state_shapes(bsz):
    return [jax.ShapeDtypeStruct((bsz, A_HEADS, A_DV, A_DK), F32),
            jax.ShapeDtypeStruct((bsz, B_GROUPS, B_STATE, 256), F32)]


def _state_scratch():
    return [pltpu.VMEM((A_HEADS, A_DV, A_DK), F32), pltpu.VMEM((B_GN, B_WIDTH), F32),
            pltpu.VMEM((A_HEADS, TILE, LANES), F32)]


def _mod_spec(mod_row0, mod_stride):
    return pl.BlockSpec((1, 1, MOD_COLS), lambda b, i: (mod_row0 + mod_stride * b, 0, 0))


def _fwd_call(x, mod, lp, layer, row_block, mod_row0, mod_stride, init, emit_state):
    bsz, length, _ = x.shape
    nt = length // TILE
    tok = functools.partial(_tok_spec, nt=nt, reverse=False)
    in_specs = [tok(D_MODEL), _mod_spec(mod_row0, mod_stride), _const_spec((1, D_MODEL)),
                _const_spec((D_MODEL, IN_PACKED)), _const_spec((DEPTH, A_KW)), _const_spec((3, B_CONV_CH)),
                _const_spec((1, B_CONV_CH)), _const_spec((1, LANES)), _const_spec((1, LANES)),
                _const_spec((1, B_WIDTH)), _const_spec((2 * LANES, B_WIDTH))]
    args = [x, mod, lp['ln1'], lp['w_in'], lp['lbl'][0], lp['conv_w'], lp['conv_b'], lp['dt_bias'][0],
            lp['a_log'][0], lp['d_skip'], lp['expand']]
    if init is not None:
        in_specs += _state_specs()
        args += list(init)
    widths = (A_KW, A_WIDTH, A_KW, B_CONV_CH, LANES, A_WIDTH, B_WIDTH, 2 * D_MODEL, A_WIDTH, B_WIDTH)
    out_specs = [tok(w) for w in widths]
    out_shape = [jax.ShapeDtypeStruct((bsz, length, w), F32) for w in widths]
    if emit_state:
        out_specs += _state_specs()
        out_shape += _state_shapes(bsz)
    return pl.pallas_call(
        functools.partial(_fwd_kernel, layer=layer, row_block=row_block, has_init=init is not None,
                          emit_state=emit_state),
        grid=(bsz, nt), in_specs=in_specs, out_specs=out_specs, out_shape=out_shape,
        scratch_shapes=_state_scratch(),
        compiler_params=pltpu.CompilerParams(dimension_semantics=("arbitrary", "arbitrary"),
                                             vmem_limit_bytes=VMEM_LIMIT),
        name="mixer_fwd",
    )(*args)


def _bwd_kernel(*refs, layer, has_init, emit_state):
    (x_ref, mod_ref, q_ref, v_ref, fbw_ref, xbc_ref, dtbw_ref, ga_ref, z_ref, gates_ref, oa_ref, yb_ref,
     lbl_ref, dtb_ref, alog_ref, an_ref, bn_ref, wa_ref, wb_ref, wo_ref, e2_ref), refs = refs[:21], refs[21:]
    if has_init:
        (sa_in, sb_in), refs = refs[:2], refs[2:]
    else:
        sa_in = sb_in = None
    x_out, refs = refs[0], refs[1:]
    if emit_state:
        (sa_out, sb_out), refs = refs[:2], refs[2:]
    sa_ref, sb_ref, c_ref = refs

    i = pl.program_id(1)

    @pl.when(i == 0)
    def _():
        _load_states(sa_in, sb_in, sa_ref, sb_ref)

    q = q_ref[0]
    v = v_ref[0]
    log_f, k = _hgrn_gates(fbw_ref[0], _lower_bound(lbl_ref[...], layer))
    gates = _sigmoid(gates_ref[0])

    def branch_a():
        o_a = yield from _hgrn_dir(q, v, log_f * LOG2E, k, sa_ref, c_ref, True)
        o_a = oa_ref[0] + o_a
        an = an_ref[...]
        o_a = jnp.concatenate(
            [_rms(o_a[:, h * A_DV:(h + 1) * A_DV]) * an[:, h * A_DV:(h + 1) * A_DV] for h in range(A_HEADS)],
            axis=1)
        o_a = o_a * _silu(ga_ref[0])
        yield
        return gates[:, 0:D_MODEL] * _mm(o_a.astype(BF16), wa_ref[...])

    xbc = xbc_ref[0]
    merged_a, y_b = _interleave(
        branch_a(),
        _ssd_dir(xbc[:, 0:B_WIDTH], xbc[:, B_WIDTH:B_WIDTH + B_GN], xbc[:, B_WIDTH + B_GN:B_CONV_CH],
                 dtbw_ref[0], dtb_ref[...], alog_ref[...], e2_ref[...], sb_ref, True))
    y_b = (yb_ref[0] + y_b) * _silu(z_ref[0])
    bn = bn_ref[...]
    gw = B_WIDTH // B_GROUPS
    y_b = jnp.concatenate(
        [_rms(y_b[:, g * gw:(g + 1) * gw]) * bn[:, g * gw:(g + 1) * gw] for g in range(B_GROUPS)], axis=1)

    merged = merged_a + gates[:, D_MODEL:2 * D_MODEL] * _mm(y_b.astype(BF16), wb_ref[...])
    mix = _mm(merged.astype(BF16), wo_ref[...])
    gate1 = mod_ref[0][:, 2 * D_MODEL:3 * D_MODEL]
    x_out[0] = x_ref[0] + gate1 * mix

    if emit_state:
        @pl.when(i == pl.num_programs(1) - 1)
        def _():
            _store_states(sa_out, sb_out, sa_ref, sb_ref)


def _bwd_call(x, mod, fwd_outs, lp, layer, mod_row0, mod_stride, init, emit_state):
    bsz, length, _ = x.shape
    nt = length // TILE
    tok = functools.partial(_tok_spec, nt=nt, reverse=True)
    widths = (A_KW, A_WIDTH, A_KW, B_CONV_CH, LANES, A_WIDTH, B_WIDTH, 2 * D_MODEL, A_WIDTH, B_WIDTH)
    in_specs = ([tok(D_MODEL), _mod_spec(mod_row0, mod_stride)] + [tok(w) for w in widths]
                + [_const_spec((DEPTH, A_KW)), _const_spec((1, LANES)), _const_spec((1, LANES)),
                   _const_spec((1, A_WIDTH)), _const_spec((1, B_WIDTH)), _const_spec((A_WIDTH, D_MODEL)),
                   _const_spec((B_WIDTH, D_MODEL)), _const_spec((D_MODEL, D_MODEL)),
                   _const_spec((2 * LANES, B_WIDTH))])
    args = ([x, mod] + list(fwd_outs)
            + [lp['lbl'][1], lp['dt_bias'][1], lp['a_log'][1], lp['a_norm'], lp['b_norm'], lp['w_br_a'],
               lp['w_br_b'], lp['w_out'], lp['expand']])
    if init is not None:
        in_specs += _state_specs()
        args += list(init)
    out_specs = [tok(D_MODEL)]
    out_shape = [jax.ShapeDtypeStruct((bsz, length, D_MODEL), F32)]
    if emit_state:
        out_specs += _state_specs()
        out_shape += _state_shapes(bsz)
    return pl.pallas_call(
        functools.partial(_bwd_kernel, layer=layer, has_init=init is not None, emit_state=emit_state),
        grid=(bsz, nt), in_specs=in_specs, out_specs=out_specs, out_shape=out_shape,
        scratch_shapes=_state_scratch(),
        compiler_params=pltpu.CompilerParams(dimension_semantics=("arbitrary", "arbitrary"),
                                             vmem_limit_bytes=VMEM_LIMIT),
        name="mixer_bwd",
    )(*args)


def _ffn_kernel(x_ref, mod_ref, ln_ref, wu_ref, cw_ref, cb_ref, wd_ref, lnf_ref, o_ref, *, row_block, final_norm):
    x = x_ref[0]
    n = x.shape[0]
    mod = mod_ref[0]
    shift, scale, gate = (mod[:, 3 * D_MODEL:4 * D_MODEL], mod[:, 4 * D_MODEL:5 * D_MODEL],
                          mod[:, 5 * D_MODEL:6 * D_MODEL])
    u = ((_rms(x) * ln_ref[...]) * (1.0 + scale) + shift).astype(BF16)
    r = _iota((n, FF_BLOCK), 0) & (row_block - 1)
    first = r == 0
    last = r == row_block - 1

    def up(j):
        return (_mm(u, wu_ref[:, j * FF_BLOCK:(j + 1) * FF_BLOCK]),
                _mm(u, wu_ref[:, D_FF + j * FF_BLOCK:D_FF + (j + 1) * FF_BLOCK]))

    def conv(h, lo):
        cw = cw_ref[:, lo:lo + FF_BLOCK]
        prev = jnp.where(first, 0.0, pltpu.roll(h, 1, axis=0))
        nxt = jnp.where(last, 0.0, pltpu.roll(h, n - 1, axis=0))
        return cb_ref[:, lo:lo + FF_BLOCK] + prev * cw[0:1] + h * cw[1:2] + nxt * cw[2:3]

    steps = D_FF // FF_BLOCK
    acc = None
    h = up(0)
    for j in range(steps):
        h_next = up(j + 1) if j + 1 < steps else None
        act = _silu(conv(h[0], j * FF_BLOCK)) * conv(h[1], D_FF + j * FF_BLOCK)
        part = _mm(act.astype(BF16), wd_ref[j * FF_BLOCK:(j + 1) * FF_BLOCK, :])
        acc = part if acc is None else acc + part
        h = h_next
    y = x + gate * acc
    if final_norm:
        y = _rms(y) * lnf_ref[...]
    o_ref[0] = y


def _ffn_call(x, mod, lp, ln_f, row_block, mod_row0, mod_stride, final_norm):
    bsz, length, _ = x.shape
    nt = length // TILE
    tok = _tok_spec(D_MODEL, nt, False)
    return pl.pallas_call(
        functools.partial(_ffn_kernel, row_block=row_block, final_norm=final_norm),
        grid=(bsz, nt),
        in_specs=[tok, _mod_spec(mod_row0, mod_stride), _const_spec((1, D_MODEL)), _const_spec((D_MODEL, 2 * D_FF)),
                  _const_spec((3, 2 * D_FF)), _const_spec((1, 2 * D_FF)), _const_spec((D_FF, D_MODEL)),
                  _const_spec((1, D_MODEL))],
        out_specs=tok,
        out_shape=jax.ShapeDtypeStruct((bsz, length, D_MODEL), F32),
        compiler_params=pltpu.CompilerParams(dimension_semantics=("arbitrary", "arbitrary"),
                                             vmem_limit_bytes=VMEM_LIMIT),
        name="conv_ffn",
    )(x, mod, lp['ln2'], lp['w_ff_up'], lp['ff_conv_w'], lp['ff_conv_b'], lp['w_ff_down'], ln_f)


def _pad_heads(a):
    return jnp.pad(a.astype(F32), [(0, 0)] * (a.ndim - 1) + [(0, LANES - a.shape[-1])])


def _pack_w_in(w):
    src_gates = C_XBC + B_CONV_CH + 2 * B_HEADS
    pad = jnp.zeros((w.shape[0], LANES - B_HEADS), w.dtype)
    return jnp.concatenate([w[:, :C_GATES], w[:, src_gates:], w[:, C_GATES:C_GATES + B_HEADS], pad,
                            w[:, C_GATES + B_HEADS:src_gates], pad], axis=1).astype(BF16)


def _hgrn_state_in(s):
    return jnp.swapaxes(s.astype(F32), -1, -2)


def _ssd_state_in(s):
    b = s.shape[0]
    s = s.astype(F32).reshape(b, B_GROUPS, B_HEADS // B_GROUPS, B_HEADDIM, B_STATE)
    return s.transpose(0, 1, 4, 2, 3).reshape(b, B_GROUPS, B_STATE, 256)


def _ssd_state_out(s):
    b = s.shape[0]
    s = s.reshape(b, B_GROUPS, B_STATE, B_HEADS // B_GROUPS, B_HEADDIM)
    return s.transpose(0, 1, 3, 4, 2).reshape(b, B_HEADS, B_HEADDIM, B_STATE)


def kernel(x_prompt, x_sample, c, state_hgrn, state_ssd, c_ctx, w_ada, b_ada, ln1, ln2, ln_f, w_in, lb_logits,
           a_norm, conv_w, conv_b, dt_bias, a_log, d_skip, b_norm, w_br_a, w_br_b, w_out, w_ff_up, ff_conv_w,
           ff_conv_b, w_ff_down):
    depth = w_in.shape[0]
    dec_b = x_sample.shape[0]
    assert depth == DEPTH and dec_b + 1 <= MOD_ROWS
    assert x_prompt.shape[1] == TILE and x_sample.shape[1] % TILE == 0 and TILE % GRID_W == 0

    cvec = jnp.concatenate([c_ctx[None].astype(F32), c.astype(F32),
                            jnp.zeros((MOD_ROWS - 1 - dec_b, D_MODEL), F32)], axis=0)
    mod_all = _modulation(cvec, w_ada.astype(F32), b_ada.astype(F32))

    expand = (jnp.arange(2 * LANES)[:, None] % LANES == jnp.arange(B_WIDTH)[None, :] // B_HEADDIM).astype(BF16)
    layers = []
    for l in range(depth):
        layers.append(dict(
            ln1=ln1[l][None].astype(F32), ln2=ln2[l][None].astype(F32),
            w_in=_pack_w_in(w_in[l]),
            lbl=(lb_logits[:, 0, :].astype(F32), lb_logits[:, 1, :].astype(F32)),
            a_norm=a_norm[l][None].astype(F32), b_norm=b_norm[l][None].astype(F32),
            conv_w=conv_w[l].astype(F32), conv_b=conv_b[l][None].astype(F32),
            dt_bias=(_pad_heads(dt_bias[l, 0])[None], _pad_heads(dt_bias[l, 1])[None]),
            a_log=(_pad_heads(a_log[l, 0])[None], _pad_heads(a_log[l, 1])[None]),
            d_skip=jnp.repeat(d_skip[l].astype(F32), B_HEADDIM)[None], expand=expand,
            w_br_a=w_br_a[l].astype(BF16), w_br_b=w_br_b[l].astype(BF16), w_out=w_out[l].astype(BF16),
            w_ff_up=w_ff_up[l].astype(BF16), ff_conv_w=ff_conv_w[l].astype(F32),
            ff_conv_b=ff_conv_b[l][None].astype(F32), w_ff_down=w_ff_down[l].astype(BF16)))
    ln_f2 = ln_f[None].astype(F32)

    def run(x, row_block, mod_row0, mod_stride, states, emit_state):
        fins = []
        for l in range(depth):
            lp = layers[l]
            mod = mod_all[l].reshape(MOD_ROWS, 1, MOD_COLS)
            init_f = init_b = None
            if states is not None:
                init_f = (_hgrn_state_in(states[0][:, l, 0]), _ssd_state_in(states[1][:, l, 0]))
                init_b = (_hgrn_state_in(states[0][:, l, 1]), _ssd_state_in(states[1][:, l, 1]))
            fo = _fwd_call(x, mod, lp, l, row_block, mod_row0, mod_stride, init_f, emit_state)
            bo = _bwd_call(x, mod, fo[:10], lp, l, mod_row0, mod_stride, init_b, emit_state)
            if emit_state:
                fins.append((fo[10], fo[11], bo[1], bo[2]))
            x = _ffn_call(bo[0], mod, lp, ln_f2, row_block, mod_row0, mod_stride, l == depth - 1)
        return x, fins

    y_prompt, fins = run(x_prompt.astype(F32), x_prompt.shape[1], 0, 0, None, True)
    y_sample, _ = run(x_sample.astype(F32), GRID_W, 1, 1, (state_hgrn, state_ssd), False)

    new_hgrn = jnp.stack([jnp.stack([jnp.swapaxes(f[0], -1, -2), jnp.swapaxes(f[2], -1, -2)], axis=1)
                          for f in fins], axis=1)
    new_ssd = jnp.stack([jnp.stack([_ssd_state_out(f[1]), _ssd_state_out(f[3])], axis=1) for f in fins], axis=1)
    return (y_prompt.astype(x_prompt.dtype), y_sample.astype(x_sample.dtype),
            new_hgrn.astype(x_prompt.dtype), new_ssd.astype(x_prompt.dtype))
```

```python
import functools

import jax
import jax.numpy as jnp
from jax import lax
from jax.experimental import pallas as pl
from jax.experimental.pallas import tpu as pltpu

F32 = jnp.float32
BF16 = jnp.bfloat16

D_MODEL = 1024
DEPTH = 4
GRID_W = 64
A_HEADS = 4
A_DK = 128
A_DV = 128
A_KW = A_HEADS * A_DK
A_WIDTH = A_HEADS * A_DV
B_HEADS = 16
B_HEADDIM = 64
B_WIDTH = B_HEADS * B_HEADDIM
B_GROUPS = 4
B_STATE = 64
B_GN = B_GROUPS * B_STATE
B_CONV_CH = B_WIDTH + 2 * B_GN
D_FF = 2816
N_MOD = 6
EPS = 1e-6
LOG2E = 1.4426950408889634

LANES = 128
TILE = 256
MOD_ROWS = 8
MOD_COLS = N_MOD * D_MODEL
MOD_BLOCK = 1536
FF_BLOCK = 256
FILL_COLS = 512
FF_DOWN_BLOCKS = 2
SUBLANES = 8
VMEM_LIMIT = 56 * 1024 * 1024

C_Q = 0
C_I = C_Q + A_KW
C_FFW = C_I + A_WIDTH
C_FBW = C_FFW + A_KW
C_GA = C_FBW + A_KW
C_Z = C_GA + A_WIDTH
C_XBC = C_Z + B_WIDTH
C_GATES = C_XBC + B_CONV_CH
C_DTFW = C_GATES + 2 * D_MODEL
C_DTBW = C_DTFW + LANES
IN_PACKED = C_DTBW + LANES

NT_DIMS = (((1,), (1,)), ((), ()))
TN_DIMS = (((0,), (0,)), ((), ()))


def _mm(a, b):
    return jnp.dot(a, b, preferred_element_type=F32)


def _mm_nt(a, b):
    return lax.dot_general(a, b, NT_DIMS, preferred_element_type=F32)


def _mm_tn(a, b):
    return lax.dot_general(a, b, TN_DIMS, preferred_element_type=F32)


def _sigmoid(x):
    return 0.5 * jnp.tanh(0.5 * x) + 0.5


def _silu(x):
    return x * _sigmoid(x)


def _softplus(x):
    return jnp.maximum(x, 0.0) + jnp.log1p(jnp.exp(-jnp.abs(x)))


def _rms(x):
    return x * lax.rsqrt(jnp.mean(x * x, axis=-1, keepdims=True) + EPS)


def _iota(shape, dim):
    return lax.broadcasted_iota(jnp.int32, shape, dim)


def _zero_rows(a, period, offset):
    sub = _iota((SUBLANES, a.shape[1]), 0)
    groups = []
    for g in range(a.shape[0] // SUBLANES):
        blk = a[g * SUBLANES:(g + 1) * SUBLANES]
        if (g * SUBLANES) % period == offset - offset % SUBLANES:
            blk = jnp.where(sub == offset % SUBLANES, 0.0, blk)
        groups.append(blk)
    return jnp.concatenate(groups, axis=0)


def _row_neighbours(h, row_block):
    n = h.shape[0]
    return (_zero_rows(pltpu.roll(h, 1, axis=0), row_block, 0),
            _zero_rows(pltpu.roll(h, n - 1, axis=0), row_block, row_block - 1))


def _split_bf16(x, parts):
    out = []
    for i in range(parts):
        p = x.astype(BF16)
        out.append(p)
        if i + 1 < parts:
            x = x - p.astype(F32)
    return out


def _scan_rows(g, reverse):
    n = g.shape[0]
    row = _iota((n, n), 0)
    col = _iota((n, n), 1)
    tri = jnp.where((col >= row) if reverse else (col <= row), 1.0, 0.0).astype(BF16)
    return _mm(jnp.concatenate([tri, tri, tri], axis=1), jnp.concatenate(_split_bf16(g, 3), axis=0))


def _level_ref(c, read_row, m, off):
    n, w = c.shape
    if 2 * m < 8:
        rk = _iota(c.shape, 0) & (2 * m - 1)
        out = c
        for k in range(2 * m):
            if k != off:
                out = jnp.where(rk == k, pltpu.roll(c, (k - off) % n, axis=0), out)
        return out
    pieces = [jnp.broadcast_to(read_row(j * 2 * m + off), (2 * m, w)) for j in range(n // (2 * m))]
    return pieces[0] if len(pieces) == 1 else jnp.concatenate(pieces, axis=0)


def _neg_abs(x):
    bits = lax.bitcast_convert_type(x, jnp.uint32) | jnp.uint32(0x80000000)
    return lax.bitcast_convert_type(bits, F32)


def _pair_level(n, reverse):
    t = _iota((n, n), 0)
    s = _iota((n, n), 1)
    x = t ^ s
    lvl = jnp.full((n, n), -1, jnp.int32)
    for l in range(n.bit_length() - 1):
        lvl = lvl + (x >= (1 << l)).astype(jnp.int32)
    return jnp.where((s > t) if reverse else (s < t), lvl, -1)


def _hgrn_gates(fx, lb):
    log_sig = jnp.minimum(fx, 0.0) - jnp.log(1.0 + jnp.exp(-jnp.abs(fx)))
    a = jnp.log(lb)
    b = jnp.log1p(-lb) + log_sig
    log_f = jnp.maximum(a, b) + jnp.log(1.0 + jnp.exp(-jnp.abs(a - b)))
    return log_f, (1.0 - lb) * _sigmoid(-fx)


def _lower_bound(logits, layer):
    e = jnp.exp(logits - jnp.max(logits, axis=0, keepdims=True))
    sm = e / jnp.sum(e, axis=0, keepdims=True)
    acc = jnp.zeros_like(sm[0:1])
    for j in range(1, layer + 1):
        acc = acc + sm[j:j + 1]
    return jnp.maximum(acc, 0.0)


def _interleave(*gens):
    results = [None] * len(gens)
    live = list(range(len(gens)))
    while live:
        for idx in list(live):
            try:
                next(gens[idx])
            except StopIteration as stop:
                results[idx] = stop.value
                live.remove(idx)
    return results


def _hgrn_dir(q, v, log2_f, k, st_ref, c_ref, reverse):
    n = q.shape[0]
    hb = n // 2
    top = n.bit_length() - 2
    lvl = _pair_level(hb, reverse)
    ones = jnp.ones((LANES, LANES), BF16)
    c_all = _scan_rows(log2_f, reverse)
    yield
    outs = []
    for h in range(A_HEADS):
        sl = slice(h * LANES, (h + 1) * LANES)
        qh, vh, kh = q[:, sl], v[:, sl], k[:, sl]
        q16, k16 = qh.astype(BF16), kh.astype(BF16)
        c = c_all[:, sl]
        c_ref[h] = c
        read_row = lambda r, h=h: c_ref[h, pl.ds(r, 1), :]
        s_lo = s_hi = None
        for l in range(top + 1):
            m = 1 << l
            w = jnp.exp2(_neg_abs(c - _level_ref(c, read_row, m, m if reverse else m - 1))).astype(BF16)
            qt = q16 * w
            kt = k16 * w
            if l < top:
                a = _mm_nt(qt[:hb], kt[:hb])
                b = _mm_nt(qt[hb:], kt[hb:])
                s_lo = jnp.where(lvl == l, a, 0.0 if s_lo is None else s_lo)
                s_hi = jnp.where(lvl == l, b, 0.0 if s_hi is None else s_hi)
            elif reverse:
                s_off = _mm_nt(qt[:hb], kt[hb:])
            else:
                s_off = _mm_nt(qt[hb:], kt[:hb])
            yield
        vb = vh.astype(BF16)
        if reverse:
            o = jnp.concatenate([_mm(jnp.concatenate([s_lo, s_off], axis=1).astype(BF16), vb),
                                 _mm(s_hi.astype(BF16), vb[hb:])], axis=0)
        else:
            o = jnp.concatenate([_mm(s_lo.astype(BF16), vb[:hb]),
                                 _mm(jnp.concatenate([s_off, s_hi], axis=1).astype(BF16), vb)], axis=0)
        c_end = read_row(0 if reverse else n - 1)
        st = st_ref[h]
        o = o + _mm((qh * kh).astype(BF16), ones) * vh
        o = o + _mm_nt((qh * jnp.exp2(c)).astype(BF16), st.astype(BF16))
        st_ref[h] = st * jnp.exp2(c_end) + _mm_tn(vb, (kh * jnp.exp2(c_end - c)).astype(BF16))
        outs.append(o)
        yield
    return jnp.concatenate(outs, axis=1)


def _expand_heads(a):
    r = a.shape[0]
    lane = _iota((r, LANES), 1)
    slabs = []
    for j in range(B_HEADS // 2):
        lo = jnp.broadcast_to(a[:, 2 * j:2 * j + 1], (r, LANES))
        hi = jnp.broadcast_to(a[:, 2 * j + 1:2 * j + 2], (r, LANES))
        slabs.append(jnp.where(lane < B_HEADDIM, lo, hi))
    return jnp.concatenate(slabs, axis=1)


def _expand_heads_mxu(a, e2):
    a = jnp.where(_iota(a.shape, 1) < B_HEADS, a, 0.0)
    return _mm(jnp.concatenate(_split_bf16(a, 2), axis=1), e2)


def _ssd_dir(xs, bm, cm, dt_raw, dt_bias, a_log, e2, st_ref, reverse):
    n = xs.shape[0]
    dt = _softplus(dt_raw + dt_bias)
    ac = _scan_rows((-LOG2E) * jnp.exp(a_log) * dt, reverse)
    ac_t = ac.T
    dt_t = dt.T
    hb = n // 2
    row_s = _iota((hb, hb), 0)
    col_s = _iota((hb, hb), 1)
    tri = (col_s >= row_s) if reverse else (col_s <= row_s)
    lane_gn = _iota((n, B_GN), 1)
    lane = _iota((n, LANES), 1)
    xb = xs.astype(BF16)
    bb = bm.astype(BF16)
    lo, hi = slice(0, hb), slice(hb, n)
    yield
    slabs = []
    for g in range(B_GROUPS):
        cb = _mm_nt(jnp.where((lane_gn >> 6) == g, cm, 0.0).astype(BF16), bb)
        yield
        for jp in range(2):
            slab = xb[:, (2 * g + jp) * LANES:(2 * g + jp + 1) * LANES]
            pair = []
            for e in range(2):
                h = 4 * g + 2 * jp + e
                col = ac[:, h:h + 1]
                row = ac_t[h:h + 1, :]
                dt_row = dt_t[h:h + 1, :]

                def quad(rs, cs, masked):
                    seg = col[rs] - row[:, cs]
                    if masked:
                        seg = jnp.where(tri, seg, -1e30)
                    return (jnp.exp2(seg) * dt_row[:, cs] * cb[rs, cs]).astype(BF16)

                if reverse:
                    y_lo = _mm(jnp.concatenate([quad(lo, lo, True), quad(lo, hi, False)], axis=1), slab)
                    y_hi = _mm(quad(hi, hi, True), slab[hi])
                else:
                    y_lo = _mm(quad(lo, lo, True), slab[lo])
                    y_hi = _mm(jnp.concatenate([quad(hi, lo, False), quad(hi, hi, True)], axis=1), slab)
                pair.append(jnp.concatenate([y_lo, y_hi], axis=0))
                yield
            slabs.append(jnp.where(lane < B_HEADDIM, pair[0], pair[1]))
    y = jnp.concatenate(slabs, axis=1)
    st = st_ref[...]
    y = y + _mm(cm.astype(BF16), st.astype(BF16)) * _expand_heads_mxu(jnp.exp2(ac), e2)
    yield
    a_end = ac[0:1, :] if reverse else ac[n - 1:n, :]
    wgt = _expand_heads_mxu(jnp.exp2(a_end - ac) * dt, e2)
    upd = _mm_tn(bb, (xs * wgt).astype(BF16))
    block = (_iota(st.shape, 0) >> 6) == (_iota(st.shape, 1) >> 8)
    st_ref[...] = jnp.where(block, st * _expand_heads(jnp.exp2(a_end)) + upd, 0.0)
    return y


def _load_states(sa_in, sb_in, sa_ref, sb_ref):
    if sa_in is None:
        sa_ref[...] = jnp.zeros_like(sa_ref)
        sb_ref[...] = jnp.zeros_like(sb_ref)
    else:
        sa_ref[...] = sa_in[0]
        sb_ref[...] = jnp.zeros_like(sb_ref)
        for g in range(B_GROUPS):
            sb_ref[g * B_STATE:(g + 1) * B_STATE, g * 256:(g + 1) * 256] = sb_in[0, g]


def _store_states(sa_out, sb_out, sa_ref, sb_ref):
    for h in range(A_HEADS):
        sa_out[0, 0, 0, h] = sa_ref[h].T
    per = B_HEADS // B_GROUPS
    for pair in range(B_GROUPS // 2):
        t = sb_ref[pair * 2 * B_STATE:(pair + 1) * 2 * B_STATE, pair * 2 * 256:(pair + 1) * 2 * 256].T
        for gl in range(2):
            for j in range(per):
                r0 = gl * 256 + j * B_HEADDIM
                sb_out[0, 0, 0, (2 * pair + gl) * per + j] = t[r0:r0 + B_HEADDIM, gl * B_STATE:(gl + 1) * B_STATE]


def _mod_kernel(c_ref, w_ref, b_ref, o_ref):
    o_ref[0] = jnp.dot(_silu(c_ref[...]), w_ref[0], preferred_element_type=F32,
                       precision=lax.Precision.HIGHEST) + b_ref[0]


def _modulation(cvec, w_ada, b_ada):
    depth = w_ada.shape[0]
    return pl.pallas_call(
        _mod_kernel,
        grid=(depth, MOD_COLS // MOD_BLOCK),
        in_specs=[
            pl.BlockSpec((MOD_ROWS, D_MODEL), lambda l, j: (0, 0)),
            pl.BlockSpec((1, D_MODEL, MOD_BLOCK), lambda l, j: (l, 0, j)),
            pl.BlockSpec((1, 1, MOD_BLOCK), lambda l, j: (l, 0, j)),
        ],
        out_specs=pl.BlockSpec((1, MOD_ROWS, MOD_BLOCK), lambda l, j: (l, 0, j)),
        out_shape=jax.ShapeDtypeStruct((depth, MOD_ROWS, MOD_COLS), F32),
        compiler_params=pltpu.CompilerParams(dimension_semantics=("arbitrary", "arbitrary"),
                                             vmem_limit_bytes=VMEM_LIMIT),
        name="adaln_mod",
    )(cvec, w_ada, b_ada.reshape(depth, 1, MOD_COLS))


def _fwd_kernel(*refs, layer, row_block, has_init, emit_state, n_alias):
    (x_ref, mod_ref, ln_ref, w_ref, lbl_ref, cw_ref, cb_ref, dtb_ref, alog_ref, dskip_ref,
     e2_ref), refs = refs[:11], refs[11:]
    if has_init:
        (sa_in, sb_in), refs = refs[:2], refs[2:]
    else:
        sa_in = sb_in = None
    refs = refs[n_alias:]
    (q_out, v_out, fbw_out, xbc_out, dtbw_out, ga_out, z_out, gates_out, oa_out, yb_out), refs = refs[:10], refs[10:]
    if emit_state:
        (sa_out, sb_out), refs = refs[:2], refs[2:]
    sa_ref, sb_ref, c_ref = refs

    i = pl.program_id(1)

    @pl.when(i == 0)
    def _():
        _load_states(sa_in, sb_in, sa_ref, sb_ref)

    x = x_ref[0]
    mod = mod_ref[0]
    shift, scale = mod[:, 0:D_MODEL], mod[:, D_MODEL:2 * D_MODEL]
    u = ((_rms(x) * ln_ref[...]) * (1.0 + scale) + shift).astype(BF16)

    def proj(lo, hi):
        return _mm(u, w_ref[:, lo:hi])

    pending = [(ga_out, 0, C_GA, A_WIDTH), (fbw_out, 0, C_FBW, A_KW)]
    pending += [(z_out, o, C_Z + o, FILL_COLS) for o in range(0, B_WIDTH, FILL_COLS)]
    pending += [(gates_out, o, C_GATES + o, FILL_COLS) for o in range(0, 2 * D_MODEL, FILL_COLS)]
    pending.append((dtbw_out, 0, C_DTBW, LANES))

    def deferred():
        for ref, dst, src, width in pending:
            ref[0, :, dst:dst + width] = proj(src, src + width)
            yield
            yield
            yield

    fills = deferred()
    q = proj(C_Q, C_I)
    v = proj(C_I, C_FFW)
    fx = proj(C_FFW, C_FBW)
    xbc = proj(C_XBC, C_GATES)
    dt_raw = proj(C_DTFW, C_DTBW)
    next(fills)
    q = _silu(q)
    q_out[0] = q
    v_out[0] = v
    log_f, k = _hgrn_gates(fx, _lower_bound(lbl_ref[...], layer))
    next(fills), next(fills), next(fills)

    prev, nxt = _row_neighbours(xbc, row_block)
    cw = cw_ref[...]
    xbc = _silu(cb_ref[...] + prev * cw[0:1] + xbc * cw[1:2] + nxt * cw[2:3])
    xbc_out[0] = xbc

    xs = xbc[:, 0:B_WIDTH]
    o_a, y, _ = _interleave(
        _hgrn_dir(q, v, log_f * LOG2E, k, sa_ref, c_ref, False),
        _ssd_dir(xs, xbc[:, B_WIDTH:B_WIDTH + B_GN], xbc[:, B_WIDTH + B_GN:B_CONV_CH],
                 dt_raw, dtb_ref[...], alog_ref[...], e2_ref[...], sb_ref, False),
        fills)
    oa_out[0] = o_a
    yb_out[0] = y + dskip_ref[...] * xs

    if emit_state:
        @pl.when(i == pl.num_programs(1) - 1)
        def _():
            _store_states(sa_out, sb_out, sa_ref, sb_ref)


def _const_spec(shape):
    nd = len(shape)
    return pl.BlockSpec(shape, lambda b, i: (0,) * nd, pipeline_mode=pl.Buffered(1))


def _tok_spec(width, nt, reverse):
    if reverse:
        return pl.BlockSpec((1, TILE, width), lambda b, i: (b, nt - 1 - i, 0))
    return pl.BlockSpec((1, TILE, width), lambda b, i: (b, i, 0))


def _state_specs():
    return [pl.BlockSpec((1, A_HEADS, A_DV, A_DK), lambda b, i: (b, 0, 0, 0)),
            pl.BlockSpec((1, B_GROUPS, B_STATE, 256), lambda b, i: (b, 0, 0, 0))]


def _emit_state_io(bsz, layer, emit, n_in, n_out):
    d = emit['direction']
    out_specs = [pl.BlockSpec((1, 1, 1, A_HEADS, A_DK, A_DV), lambda b, i: (b, layer, d, 0, 0, 0)),
                 pl.BlockSpec((1, 1, 1, B_HEADS, B_HEADDIM, B_STATE), lambda b, i: (b, layer, d, 0, 0, 0))]
    out_shape = [jax.ShapeDtypeStruct((bsz, DEPTH, 2, A_HEADS, A_DK, A_DV), F32),
                 jax.ShapeDtypeStruct((bsz, DEPTH, 2, B_HEADS, B_HEADDIM, B_STATE), F32)]
    if emit['acc'] is None:
        return [], [], out_specs, out_shape, {}, 0
    in_specs = [pl.BlockSpec(memory_space=pl.ANY), pl.BlockSpec(memory_space=pl.ANY)]
    return in_specs, list(emit['acc']), out_specs, out_shape, {n_in: n_out, n_in + 1: n_out + 1}, 2


def _state_scratch():
    return [pltpu.VMEM((A_HEADS, A_DV, A_DK), F32), pltpu.VMEM((B_GN, B_WIDTH), F32),
            pltpu.VMEM((A_HEADS, TILE, LANES), F32)]


def _mod_spec(mod_row0, mod_stride):
    return pl.BlockSpec((1, 1, MOD_COLS), lambda b, i: (mod_row0 + mod_stride * b, 0, 0))


def _fwd_call(x, mod, lp, layer, row_block, mod_row0, mod_stride, init, emit):
    bsz, length, _ = x.shape
    nt = length // TILE
    tok = functools.partial(_tok_spec, nt=nt, reverse=False)
    in_specs = [tok(D_MODEL), _mod_spec(mod_row0, mod_stride), _const_spec((1, D_MODEL)),
                _const_spec((D_MODEL, IN_PACKED)), _const_spec((DEPTH, A_KW)), _const_spec((3, B_CONV_CH)),
                _const_spec((1, B_CONV_CH)), _const_spec((1, LANES)), _const_spec((1, LANES)),
                _const_spec((1, B_WIDTH)), _const_spec((2 * LANES, B_WIDTH))]
    args = [x, mod, lp['ln1'], lp['w_in'], lp['lbl'][0], lp['conv_w'], lp['conv_b'], lp['dt_bias'][0],
            lp['a_log'][0], lp['d_skip'], lp['expand']]
    if init is not None:
        in_specs += _state_specs()
        args += list(init)
    widths = (A_KW, A_WIDTH, A_KW, B_CONV_CH, LANES, A_WIDTH, B_WIDTH, 2 * D_MODEL, A_WIDTH, B_WIDTH)
    out_specs = [tok(w) for w in widths]
    out_shape = [jax.ShapeDtypeStruct((bsz, length, w), F32) for w in widths]
    aliases, n_alias = {}, 0
    if emit is not None:
        e_in, e_args, e_out, e_shape, aliases, n_alias = _emit_state_io(bsz, layer, emit, len(args), len(out_specs))
        in_specs += e_in
        args += e_args
        out_specs += e_out
        out_shape += e_shape
    return pl.pallas_call(
        functools.partial(_fwd_kernel, layer=layer, row_block=row_block, has_init=init is not None,
                          emit_state=emit is not None, n_alias=n_alias),
        grid=(bsz, nt), in_specs=in_specs, out_specs=out_specs, out_shape=out_shape,
        input_output_aliases=aliases, scratch_shapes=_state_scratch(),
        compiler_params=pltpu.CompilerParams(dimension_semantics=("arbitrary", "arbitrary"),
                                             vmem_limit_bytes=VMEM_LIMIT),
        name="mixer_fwd",
    )(*args)


def _bwd_kernel(*refs, layer, has_init, emit_state, n_alias):
    (x_ref, mod_ref, q_ref, v_ref, fbw_ref, xbc_ref, dtbw_ref, ga_ref, z_ref, gates_ref, oa_ref, yb_ref,
     lbl_ref, dtb_ref, alog_ref, an_ref, bn_ref, wa_ref, wb_ref, wo_ref, e2_ref), refs = refs[:21], refs[21:]
    if has_init:
        (sa_in, sb_in), refs = refs[:2], refs[2:]
    else:
        sa_in = sb_in = None
    refs = refs[n_alias:]
    x_out, refs = refs[0], refs[1:]
    if emit_state:
        (sa_out, sb_out), refs = refs[:2], refs[2:]
    sa_ref, sb_ref, c_ref = refs

    i = pl.program_id(1)

    @pl.when(i == 0)
    def _():
        _load_states(sa_in, sb_in, sa_ref, sb_ref)

    q = q_ref[0]
    v = v_ref[0]
    log_f, k = _hgrn_gates(fbw_ref[0], _lower_bound(lbl_ref[...], layer))
    gates = _sigmoid(gates_ref[0])

    def branch_a():
        o_a = yield from _hgrn_dir(q, v, log_f * LOG2E, k, sa_ref, c_ref, True)
        o_a = oa_ref[0] + o_a
        an = an_ref[...]
        o_a = jnp.concatenate(
            [_rms(o_a[:, h * A_DV:(h + 1) * A_DV]) * an[:, h * A_DV:(h + 1) * A_DV] for h in range(A_HEADS)],
            axis=1)
        o_a = o_a * _silu(ga_ref[0])
        yield
        return gates[:, 0:D_MODEL] * _mm(o_a.astype(BF16), wa_ref[...])

    xbc = xbc_ref[0]
    merged_a, y_b = _interleave(
        branch_a(),
        _ssd_dir(xbc[:, 0:B_WIDTH], xbc[:, B_WIDTH:B_WIDTH + B_GN], xbc[:, B_WIDTH + B_GN:B_CONV_CH],
                 dtbw_ref[0], dtb_ref[...], alog_ref[...], e2_ref[...], sb_ref, True))
    y_b = (yb_ref[0] + y_b) * _silu(z_ref[0])
    bn = bn_ref[...]
    gw = B_WIDTH // B_GROUPS
    y_b = jnp.concatenate(
        [_rms(y_b[:, g * gw:(g + 1) * gw]) * bn[:, g * gw:(g + 1) * gw] for g in range(B_GROUPS)], axis=1)

    merged = merged_a + gates[:, D_MODEL:2 * D_MODEL] * _mm(y_b.astype(BF16), wb_ref[...])
    mix = _mm(merged.astype(BF16), wo_ref[...])
    gate1 = mod_ref[0][:, 2 * D_MODEL:3 * D_MODEL]
    x_out[0] = x_ref[0] + gate1 * mix

    if emit_state:
        @pl.when(i == pl.num_programs(1) - 1)
        def _():
            _store_states(sa_out, sb_out, sa_ref, sb_ref)


def _bwd_call(x, mod, fwd_outs, lp, layer, mod_row0, mod_stride, init, emit):
    bsz, length, _ = x.shape
    nt = length // TILE
    tok = functools.partial(_tok_spec, nt=nt, reverse=True)
    widths = (A_KW, A_WIDTH, A_KW, B_CONV_CH, LANES, A_WIDTH, B_WIDTH, 2 * D_MODEL, A_WIDTH, B_WIDTH)
    in_specs = ([tok(D_MODEL), _mod_spec(mod_row0, mod_stride)] + [tok(w) for w in widths]
                + [_const_spec((DEPTH, A_KW)), _const_spec((1, LANES)), _const_spec((1, LANES)),
                   _const_spec((1, A_WIDTH)), _const_spec((1, B_WIDTH)), _const_spec((A_WIDTH, D_MODEL)),
                   _const_spec((B_WIDTH, D_MODEL)), _const_spec((D_MODEL, D_MODEL)),
                   _const_spec((2 * LANES, B_WIDTH))])
    args = ([x, mod] + list(fwd_outs)
            + [lp['lbl'][1], lp['dt_bias'][1], lp['a_log'][1], lp['a_norm'], lp['b_norm'], lp['w_br_a'],
               lp['w_br_b'], lp['w_out'], lp['expand']])
    if init is not None:
        in_specs += _state_specs()
        args += list(init)
    out_specs = [tok(D_MODEL)]
    out_shape = [jax.ShapeDtypeStruct((bsz, length, D_MODEL), F32)]
    aliases, n_alias = {}, 0
    if emit is not None:
        e_in, e_args, e_out, e_shape, aliases, n_alias = _emit_state_io(bsz, layer, emit, len(args), len(out_specs))
        in_specs += e_in
        args += e_args
        out_specs += e_out
        out_shape += e_shape
    return pl.pallas_call(
        functools.partial(_bwd_kernel, layer=layer, has_init=init is not None, emit_state=emit is not None,
                          n_alias=n_alias),
        grid=(bsz, nt), in_specs=in_specs, out_specs=out_specs, out_shape=out_shape,
        input_output_aliases=aliases, scratch_shapes=_state_scratch(),
        compiler_params=pltpu.CompilerParams(dimension_semantics=("arbitrary", "arbitrary"),
                                             vmem_limit_bytes=VMEM_LIMIT),
        name="mixer_bwd",
    )(*args)


def _ffn_kernel(x_ref, mod_ref, ln_ref, wu_ref, cw_ref, cb_ref, wd_ref, lnf_ref, o_ref, *, row_block, final_norm):
    x = x_ref[0]
    mod = mod_ref[0]
    shift, scale, gate = (mod[:, 3 * D_MODEL:4 * D_MODEL], mod[:, 4 * D_MODEL:5 * D_MODEL],
                          mod[:, 5 * D_MODEL:6 * D_MODEL])
    u = ((_rms(x) * ln_ref[...]) * (1.0 + scale) + shift).astype(BF16)

    def up(j):
        return (_mm(u, wu_ref[:, j * FF_BLOCK:(j + 1) * FF_BLOCK]),
                _mm(u, wu_ref[:, D_FF + j * FF_BLOCK:D_FF + (j + 1) * FF_BLOCK]))

    def conv(h, lo):
        cw = cw_ref[:, lo:lo + FF_BLOCK]
        prev, nxt = _row_neighbours(h, row_block)
        return cb_ref[:, lo:lo + FF_BLOCK] + prev * cw[0:1] + h * cw[1:2] + nxt * cw[2:3]

    steps = D_FF // FF_BLOCK
    acc = None
    acts = []
    h = up(0)
    for j in range(steps):
        h_next = up(j + 1) if j + 1 < steps else None
        acts.append((_silu(conv(h[0], j * FF_BLOCK)) * conv(h[1], D_FF + j * FF_BLOCK)).astype(BF16))
        if len(acts) == FF_DOWN_BLOCKS or j + 1 == steps:
            lo = (j + 1 - len(acts)) * FF_BLOCK
            part = _mm(acts[0] if len(acts) == 1 else jnp.concatenate(acts, axis=1),
                       wd_ref[lo:(j + 1) * FF_BLOCK, :])
            acc = part if acc is None else acc + part
            acts = []
        h = h_next
    y = x + gate * acc
    if final_norm:
        y = _rms(y) * lnf_ref[...]
    o_ref[0] = y


def _ffn_call(x, mod, lp, ln_f, row_block, mod_row0, mod_stride, final_norm):
    bsz, length, _ = x.shape
    nt = length // TILE
    tok = _tok_spec(D_MODEL, nt, False)
    return pl.pallas_call(
        functools.partial(_ffn_kernel, row_block=row_block, final_norm=final_norm),
        grid=(bsz, nt),
        in_specs=[tok, _mod_spec(mod_row0, mod_stride), _const_spec((1, D_MODEL)), _const_spec((D_MODEL, 2 * D_FF)),
                  _const_spec((3, 2 * D_FF)), _const_spec((1, 2 * D_FF)), _const_spec((D_FF, D_MODEL)),
                  _const_spec((1, D_MODEL))],
        out_specs=tok,
        out_shape=jax.ShapeDtypeStruct((bsz, length, D_MODEL), F32),
        compiler_params=pltpu.CompilerParams(dimension_semantics=("arbitrary", "arbitrary"),
                                             vmem_limit_bytes=VMEM_LIMIT),
        name="conv_ffn",
    )(x, mod, lp['ln2'], lp['w_ff_up'], lp['ff_conv_w'], lp['ff_conv_b'], lp['w_ff_down'], ln_f)


def _pad_heads(a):
    return jnp.pad(a.astype(F32), [(0, 0)] * (a.ndim - 1) + [(0, LANES - a.shape[-1])])


def _pack_w_in(w):
    src_gates = C_XBC + B_CONV_CH + 2 * B_HEADS
    pad = jnp.zeros((w.shape[0], LANES - B_HEADS), w.dtype)
    return jnp.concatenate([w[:, :C_GATES], w[:, src_gates:], w[:, C_GATES:C_GATES + B_HEADS], pad,
                            w[:, C_GATES + B_HEADS:src_gates], pad], axis=1).astype(BF16)


def _hgrn_state_in(s):
    return jnp.swapaxes(s.astype(F32), -1, -2)


def _ssd_state_in(s):
    b = s.shape[0]
    s = s.astype(F32).reshape(b, B_GROUPS, B_HEADS // B_GROUPS, B_HEADDIM, B_STATE)
    return s.transpose(0, 1, 4, 2, 3).reshape(b, B_GROUPS, B_STATE, 256)


def kernel(x_prompt, x_sample, c, state_hgrn, state_ssd, c_ctx, w_ada, b_ada, ln1, ln2, ln_f, w_in, lb_logits,
           a_norm, conv_w, conv_b, dt_bias, a_log, d_skip, b_norm, w_br_a, w_br_b, w_out, w_ff_up, ff_conv_w,
           ff_conv_b, w_ff_down):
    depth = w_in.shape[0]
    dec_b = x_sample.shape[0]
    assert depth == DEPTH and dec_b + 1 <= MOD_ROWS
    assert x_prompt.shape[1] == TILE and x_sample.shape[1] % TILE == 0 and TILE % GRID_W == 0

    cvec = jnp.concatenate([c_ctx[None].astype(F32), c.astype(F32),
                            jnp.zeros((MOD_ROWS - 1 - dec_b, D_MODEL), F32)], axis=0)
    mod_all = _modulation(cvec, w_ada.astype(F32), b_ada.astype(F32))

    expand = (jnp.arange(2 * LANES)[:, None] % LANES == jnp.arange(B_WIDTH)[None, :] // B_HEADDIM).astype(BF16)
    layers = []
    for l in range(depth):
        layers.append(dict(
            ln1=ln1[l][None].astype(F32), ln2=ln2[l][None].astype(F32),
            w_in=_pack_w_in(w_in[l]),
            lbl=(lb_logits[:, 0, :].astype(F32), lb_logits[:, 1, :].astype(F32)),
            a_norm=a_norm[l][None].astype(F32), b_norm=b_norm[l][None].astype(F32),
            conv_w=conv_w[l].astype(F32), conv_b=conv_b[l][None].astype(F32),
            dt_bias=(_pad_heads(dt_bias[l, 0])[None], _pad_heads(dt_bias[l, 1])[None]),
            a_log=(_pad_heads(a_log[l, 0])[None], _pad_heads(a_log[l, 1])[None]),
            d_skip=jnp.repeat(d_skip[l].astype(F32), B_HEADDIM)[None], expand=expand,
            w_br_a=w_br_a[l].astype(BF16), w_br_b=w_br_b[l].astype(BF16), w_out=w_out[l].astype(BF16),
            w_ff_up=w_ff_up[l].astype(BF16), ff_conv_w=ff_conv_w[l].astype(F32),
            ff_conv_b=ff_conv_b[l][None].astype(F32), w_ff_down=w_ff_down[l].astype(BF16)))
    ln_f2 = ln_f[None].astype(F32)

    def run(x, row_block, mod_row0, mod_stride, states, emit_state):
        acc = None
        for l in range(depth):
            lp = layers[l]
            mod = mod_all[l].reshape(MOD_ROWS, 1, MOD_COLS)
            init_f = init_b = None
            if states is not None:
                init_f = (_hgrn_state_in(states[0][:, l, 0]), _ssd_state_in(states[1][:, l, 0]))
                init_b = (_hgrn_state_in(states[0][:, l, 1]), _ssd_state_in(states[1][:, l, 1]))
            fo = _fwd_call(x, mod, lp, l, row_block, mod_row0, mod_stride, init_f,
                           dict(direction=0, acc=acc) if emit_state else None)
            if emit_state:
                acc = (fo[10], fo[11])
            bo = _bwd_call(x, mod, fo[:10], lp, l, mod_row0, mod_stride, init_b,
                           dict(direction=1, acc=acc) if emit_state else None)
            if emit_state:
                acc = (bo[1], bo[2])
            x = _ffn_call(bo[0], mod, lp, ln_f2, row_block, mod_row0, mod_stride, l == depth - 1)
        return x, acc

    y_prompt, (new_hgrn, new_ssd) = run(x_prompt.astype(F32), x_prompt.shape[1], 0, 0, None, True)
    y_sample, _ = run(x_sample.astype(F32), GRID_W, 1, 1, (state_hgrn, state_ssd), False)
    return (y_prompt.astype(x_prompt.dtype), y_sample.astype(x_sample.dtype),
            new_hgrn.astype(x_prompt.dtype), new_ssd.astype(x_prompt.dtype))
```

```python
import functools

import jax
import jax.numpy as jnp
from jax import lax
from jax.experimental import pallas as pl
from jax.experimental.pallas import tpu as pltpu

F32 = jnp.float32
BF16 = jnp.bfloat16

D_MODEL = 1024
DEPTH = 4
GRID_W = 64
A_HEADS = 4
A_DK = 128
A_DV = 128
A_KW = A_HEADS * A_DK
A_WIDTH = A_HEADS * A_DV
B_HEADS = 16
B_HEADDIM = 64
B_WIDTH = B_HEADS * B_HEADDIM
B_GROUPS = 4
B_STATE = 64
B_GN = B_GROUPS * B_STATE
B_CONV_CH = B_WIDTH + 2 * B_GN
D_FF = 2816
N_MOD = 6
EPS = 1e-6
LOG2E = 1.4426950408889634

LANES = 128
SUBLANES = 8
TILE = 256
MOD_ROWS = 8
MOD_COLS = N_MOD * D_MODEL
MOD_BLOCK = 1536
FF_BLOCK = 256
FF_DOWN_BLOCKS = 2
FILL_COLS = 512
VMEM_LIMIT = 60 * 1024 * 1024

C_Q = 0
C_I = C_Q + A_KW
C_FFW = C_I + A_WIDTH
C_FBW = C_FFW + A_KW
C_GA = C_FBW + A_KW
C_Z = C_GA + A_WIDTH
C_XBC = C_Z + B_WIDTH
C_GATES = C_XBC + B_CONV_CH
C_DTFW = C_GATES + 2 * D_MODEL
C_DTBW = C_DTFW + LANES
IN_PACKED = C_DTBW + LANES

FWD_OUT = ((A_KW, BF16), (A_WIDTH, BF16), (A_KW, F32), (B_CONV_CH, BF16), (LANES, F32), (A_WIDTH, BF16),
           (B_WIDTH, BF16), (2 * D_MODEL, BF16), (A_WIDTH, F32), (B_WIDTH, F32))

NT_DIMS = (((1,), (1,)), ((), ()))
TN_DIMS = (((0,), (0,)), ((), ()))


def _mm(a, b):
    return jnp.dot(a, b, preferred_element_type=F32)


def _mm_nt(a, b):
    return lax.dot_general(a, b, NT_DIMS, preferred_element_type=F32)


def _mm_tn(a, b):
    return lax.dot_general(a, b, TN_DIMS, preferred_element_type=F32)


def _sigmoid(x):
    return 0.5 * jnp.tanh(0.5 * x) + 0.5


def _silu(x):
    return x * _sigmoid(x)


def _softplus(x):
    return jnp.maximum(x, 0.0) + jnp.log1p(jnp.exp(-jnp.abs(x)))


def _rms(x):
    return x * lax.rsqrt(jnp.mean(x * x, axis=-1, keepdims=True) + EPS)


def _iota(shape, dim):
    return lax.broadcasted_iota(jnp.int32, shape, dim)


def _zero_rows(a, period, offset):
    sub = _iota((SUBLANES, a.shape[1]), 0)
    groups = []
    for g in range(a.shape[0] // SUBLANES):
        blk = a[g * SUBLANES:(g + 1) * SUBLANES]
        if (g * SUBLANES) % period == offset - offset % SUBLANES:
            blk = jnp.where(sub == offset % SUBLANES, 0.0, blk)
        groups.append(blk)
    return jnp.concatenate(groups, axis=0)


def _row_neighbours(h, row_block):
    n = h.shape[0]
    return (_zero_rows(pltpu.roll(h, 1, axis=0), row_block, 0),
            _zero_rows(pltpu.roll(h, n - 1, axis=0), row_block, row_block - 1))


def _split_bf16(x, parts):
    out = []
    for i in range(parts):
        if i + 1 < parts:
            p = lax.bitcast_convert_type(lax.bitcast_convert_type(x, jnp.uint32) & jnp.uint32(0xFFFF0000), F32)
            out.append(p.astype(BF16))
            x = x - p
        else:
            out.append(x.astype(BF16))
    return out


def _scan_rows(g, reverse):
    n = g.shape[0]
    row = _iota((n, n), 0)
    col = _iota((n, n), 1)
    tri = jnp.where((col >= row) if reverse else (col <= row), 1.0, 0.0).astype(BF16)
    return _mm(jnp.concatenate([tri, tri, tri], axis=1), jnp.concatenate(_split_bf16(g, 3), axis=0))


def _level_ref(c, read_row, m, off):
    n, w = c.shape
    if 2 * m < 8:
        rk = _iota(c.shape, 0) & (2 * m - 1)
        out = c
        for k in range(2 * m):
            if k != off:
                out = jnp.where(rk == k, pltpu.roll(c, (k - off) % n, axis=0), out)
        return out
    pieces = [jnp.broadcast_to(read_row(j * 2 * m + off), (2 * m, w)) for j in range(n // (2 * m))]
    return pieces[0] if len(pieces) == 1 else jnp.concatenate(pieces, axis=0)


def _neg_abs(x):
    bits = lax.bitcast_convert_type(x, jnp.uint32) | jnp.uint32(0x80000000)
    return lax.bitcast_convert_type(bits, F32)


def _pair_level(n, reverse):
    t = _iota((n, n), 0)
    s = _iota((n, n), 1)
    x = t ^ s
    lvl = jnp.full((n, n), -1, jnp.int32)
    for l in range(n.bit_length() - 1):
        lvl = lvl + (x >= (1 << l)).astype(jnp.int32)
    return jnp.where((s > t) if reverse else (s < t), lvl, -1)


def _hgrn_gates(fx, lb):
    log_sig = jnp.minimum(fx, 0.0) - jnp.log(1.0 + jnp.exp(-jnp.abs(fx)))
    a = jnp.log(lb)
    b = jnp.log1p(-lb) + log_sig
    log_f = jnp.maximum(a, b) + jnp.log(1.0 + jnp.exp(-jnp.abs(a - b)))
    return log_f, (1.0 - lb) * _sigmoid(-fx)


def _lower_bound(logits, layer):
    e = jnp.exp(logits - jnp.max(logits, axis=0, keepdims=True))
    sm = e / jnp.sum(e, axis=0, keepdims=True)
    acc = jnp.zeros_like(sm[0:1])
    for j in range(1, layer + 1):
        acc = acc + sm[j:j + 1]
    return jnp.maximum(acc, 0.0)


def _interleave(gens):
    results = [None] * len(gens)
    live = list(range(len(gens)))
    while live:
        for idx in list(live):
            try:
                next(gens[idx])
            except StopIteration as stop:
                results[idx] = stop.value
                live.remove(idx)
    return results


def _hgrn_dir(q, v, log2_f, k, st_ref, c_ref, reverse):
    n = q.shape[0]
    hb = n // 2
    top = n.bit_length() - 2
    lvl = _pair_level(hb, reverse)
    ones = jnp.ones((LANES, LANES), BF16)
    c_all = _scan_rows(log2_f, reverse)
    yield
    outs = []
    for h in range(A_HEADS):
        sl = slice(h * LANES, (h + 1) * LANES)
        qh, vh, kh = q[:, sl], v[:, sl], k[:, sl]
        c = c_all[:, sl]
        c_ref[h] = c
        read_row = lambda r, h=h: c_ref[h, pl.ds(r, 1), :]
        s_lo = s_hi = None
        for l in range(top + 1):
            m = 1 << l
            w = jnp.exp2(_neg_abs(c - _level_ref(c, read_row, m, m if reverse else m - 1))).astype(BF16)
            qt = qh * w
            kt = kh * w
            if l < top:
                a = _mm_nt(qt[:hb], kt[:hb])
                b = _mm_nt(qt[hb:], kt[hb:])
                s_lo = jnp.where(lvl == l, a, 0.0 if s_lo is None else s_lo)
                s_hi = jnp.where(lvl == l, b, 0.0 if s_hi is None else s_hi)
            elif reverse:
                s_off = _mm_nt(qt[:hb], kt[hb:])
            else:
                s_off = _mm_nt(qt[hb:], kt[:hb])
            yield
        if reverse:
            o = jnp.concatenate([_mm(jnp.concatenate([s_lo, s_off], axis=1).astype(BF16), vh),
                                 _mm(s_hi.astype(BF16), vh[hb:])], axis=0)
        else:
            o = jnp.concatenate([_mm(s_lo.astype(BF16), vh[:hb]),
                                 _mm(jnp.concatenate([s_off, s_hi], axis=1).astype(BF16), vh)], axis=0)
        c_end = read_row(0 if reverse else n - 1)
        st = st_ref[h]
        o = o + _mm(qh * kh, ones) * vh.astype(F32)
        o = o + _mm_nt(qh * jnp.exp2(c).astype(BF16), st.astype(BF16))
        st_ref[h] = st * jnp.exp2(c_end) + _mm_tn(vh, kh * jnp.exp2(c_end - c).astype(BF16))
        outs.append(o)
        yield
    return jnp.concatenate(outs, axis=1)


def _expand_heads(a):
    r = a.shape[0]
    lane = _iota((r, LANES), 1)
    slabs = []
    for j in range(B_HEADS // 2):
        lo = jnp.broadcast_to(a[:, 2 * j:2 * j + 1], (r, LANES))
        hi = jnp.broadcast_to(a[:, 2 * j + 1:2 * j + 2], (r, LANES))
        slabs.append(jnp.where(lane < B_HEADDIM, lo, hi))
    return jnp.concatenate(slabs, axis=1)


def _expand_heads_mxu(a, e2):
    a = jnp.where(_iota(a.shape, 1) < B_HEADS, a, 0.0)
    return _mm(jnp.concatenate(_split_bf16(a, 2), axis=1), e2)


def _ssd_dir(xs, bm, cm, dt_raw, dt_bias, a_log, e2, st_ref, reverse):
    n = xs.shape[0]
    dt = _softplus(dt_raw + dt_bias)
    ac = _scan_rows((-LOG2E) * jnp.exp(a_log) * dt, reverse)
    ac_t = ac.T
    dt_t = dt.T
    hb = n // 2
    row_s = _iota((hb, hb), 0)
    col_s = _iota((hb, hb), 1)
    tri = (col_s >= row_s) if reverse else (col_s <= row_s)
    lane_gn = _iota((n, B_GN), 1)
    lane = _iota((n, LANES), 1)
    lo, hi = slice(0, hb), slice(hb, n)
    yield
    slabs = []
    for g in range(B_GROUPS):
        cb = _mm_nt(jnp.where((lane_gn >> 6) == g, cm, jnp.zeros_like(cm)), bm)
        yield
        for jp in range(2):
            slab = xs[:, (2 * g + jp) * LANES:(2 * g + jp + 1) * LANES]
            zero = jnp.zeros_like(slab)
            slab_e = (jnp.where(lane < B_HEADDIM, slab, zero), jnp.where(lane < B_HEADDIM, zero, slab))
            w_lo, w_hi = [], []
            for e in range(2):
                h = 4 * g + 2 * jp + e
                col = ac[:, h:h + 1]
                row = ac_t[h:h + 1, :]
                dt_row = dt_t[h:h + 1, :]

                def quad(rs, cs, masked):
                    seg = col[rs] - row[:, cs]
                    if masked:
                        seg = jnp.where(tri, seg, -1e30)
                    return (jnp.exp2(seg) * dt_row[:, cs] * cb[rs, cs]).astype(BF16)

                if reverse:
                    w_lo += [quad(lo, lo, True), quad(lo, hi, False)]
                    w_hi += [quad(hi, hi, True)]
                else:
                    w_lo += [quad(lo, lo, True)]
                    w_hi += [quad(hi, lo, False), quad(hi, hi, True)]
                yield
            if reverse:
                x_lo = jnp.concatenate([slab_e[0], slab_e[1]], axis=0)
                x_hi = jnp.concatenate([slab_e[0][hi], slab_e[1][hi]], axis=0)
            else:
                x_lo = jnp.concatenate([slab_e[0][lo], slab_e[1][lo]], axis=0)
                x_hi = jnp.concatenate([slab_e[0], slab_e[1]], axis=0)
            slabs.append(jnp.concatenate([_mm(jnp.concatenate(w_lo, axis=1), x_lo),
                                          _mm(jnp.concatenate(w_hi, axis=1), x_hi)], axis=0))
    y = jnp.concatenate(slabs, axis=1)
    st = st_ref[...]
    y = y + _mm(cm, st.astype(BF16)) * _expand_heads_mxu(jnp.exp2(ac), e2)
    yield
    a_end = ac[0:1, :] if reverse else ac[n - 1:n, :]
    wgt = _expand_heads_mxu(jnp.exp2(a_end - ac) * dt, e2)
    upd = _mm_tn(bm, xs * wgt.astype(BF16))
    block = (_iota(st.shape, 0) >> 6) == (_iota(st.shape, 1) >> 8)
    st_ref[...] = jnp.where(block, st * _expand_heads(jnp.exp2(a_end)) + upd, 0.0)
    return y


def _load_states(sa_in, sb_in, sa_ref, sb_ref):
    if sa_in is None:
        sa_ref[...] = jnp.zeros_like(sa_ref)
        sb_ref[...] = jnp.zeros_like(sb_ref)
    else:
        sa_ref[...] = sa_in[0]
        sb_ref[...] = jnp.zeros_like(sb_ref)
        for g in range(B_GROUPS):
            sb_ref[g * B_STATE:(g + 1) * B_STATE, g * 256:(g + 1) * 256] = sb_in[0, g]


def _store_states(sa_out, sb_out, sa_ref, sb_ref):
    for h in range(A_HEADS):
        sa_out[0, 0, 0, h] = sa_ref[h].T
    per = B_HEADS // B_GROUPS
    for pair in range(B_GROUPS // 2):
        t = sb_ref[pair * 2 * B_STATE:(pair + 1) * 2 * B_STATE, pair * 2 * 256:(pair + 1) * 2 * 256].T
        for gl in range(2):
            for j in range(per):
                r0 = gl * 256 + j * B_HEADDIM
                sb_out[0, 0, 0, (2 * pair + gl) * per + j] = t[r0:r0 + B_HEADDIM, gl * B_STATE:(gl + 1) * B_STATE]


def _mod_kernel(c_ref, w_ref, b_ref, o_ref):
    o_ref[0] = jnp.dot(_silu(c_ref[...]), w_ref[0], preferred_element_type=F32,
                       precision=lax.Precision.HIGHEST) + b_ref[0]


def _modulation(cvec, w_ada, b_ada):
    depth = w_ada.shape[0]
    return pl.pallas_call(
        _mod_kernel,
        grid=(depth, MOD_COLS // MOD_BLOCK),
        in_specs=[
            pl.BlockSpec((MOD_ROWS, D_MODEL), lambda l, j: (0, 0)),
            pl.BlockSpec((1, D_MODEL, MOD_BLOCK), lambda l, j: (l, 0, j)),
            pl.BlockSpec((1, 1, MOD_BLOCK), lambda l, j: (l, 0, j)),
        ],
        out_specs=pl.BlockSpec((1, MOD_ROWS, MOD_BLOCK), lambda l, j: (l, 0, j)),
        out_shape=jax.ShapeDtypeStruct((depth, MOD_ROWS, MOD_COLS), F32),
        compiler_params=pltpu.CompilerParams(dimension_semantics=("arbitrary", "arbitrary"),
                                             vmem_limit_bytes=VMEM_LIMIT),
        name="adaln_mod",
    )(cvec, w_ada, b_ada.reshape(depth, 1, MOD_COLS))


def _const_spec(shape):
    nd = len(shape)
    return pl.BlockSpec(shape, lambda *_: (0,) * nd, pipeline_mode=pl.Buffered(1))


def _state_scratch():
    return [pltpu.VMEM((A_HEADS, A_DV, A_DK), F32), pltpu.VMEM((B_GN, B_WIDTH), F32),
            pltpu.VMEM((A_HEADS, TILE, LANES), F32)]


def _state_out_shapes(bsz):
    return [jax.ShapeDtypeStruct((bsz, DEPTH, 2, A_HEADS, A_DK, A_DV), F32),
            jax.ShapeDtypeStruct((bsz, DEPTH, 2, B_HEADS, B_HEADDIM, B_STATE), F32)]


def _fwd_kernel(*refs, layer, row_block, has_init, emit_state, n_alias):
    (x_ref, mod_ref, ln_ref, w_ref, lbl_ref, cw_ref, cb_ref, dtb_ref, alog_ref, dskip_ref,
     e2_ref), refs = refs[:11], refs[11:]
    if has_init:
        (sa_in, sb_in), refs = refs[:2], refs[2:]
    else:
        sa_in = sb_in = None
    refs = refs[n_alias:]
    (q_out, v_out, fbw_out, xbc_out, dtbw_out, ga_out, z_out, gates_out, oa_out, yb_out), refs = refs[:10], refs[10:]
    if emit_state:
        (sa_out, sb_out), refs = refs[:2], refs[2:]
    sa_ref, sb_ref, c_ref = refs

    i = pl.program_id(1)

    @pl.when(i == 0)
    def _():
        _load_states(sa_in, sb_in, sa_ref, sb_ref)

    x = x_ref[0]
    mod = mod_ref[0]
    shift, scale = mod[:, 0:D_MODEL], mod[:, D_MODEL:2 * D_MODEL]
    u = ((_rms(x) * ln_ref[...]) * (1.0 + scale) + shift).astype(BF16)

    def proj(lo, hi):
        return _mm(u, w_ref[:, lo:hi])

    pending = [(ga_out, 0, C_GA, A_WIDTH), (fbw_out, 0, C_FBW, A_KW)]
    pending += [(z_out, o, C_Z + o, FILL_COLS) for o in range(0, B_WIDTH, FILL_COLS)]
    pending += [(gates_out, o, C_GATES + o, FILL_COLS) for o in range(0, 2 * D_MODEL, FILL_COLS)]
    pending.append((dtbw_out, 0, C_DTBW, LANES))

    def deferred():
        for ref, dst, src, width in pending:
            ref[0, :, dst:dst + width] = proj(src, src + width).astype(ref.dtype)
            yield
            yield
            yield

    fills = deferred()
    q = proj(C_Q, C_I)
    v = proj(C_I, C_FFW)
    fx = proj(C_FFW, C_FBW)
    xbc = proj(C_XBC, C_GATES)
    dt_raw = proj(C_DTFW, C_DTBW)
    next(fills)
    q = _silu(q).astype(BF16)
    v = v.astype(BF16)
    q_out[0] = q
    v_out[0] = v
    log_f, k = _hgrn_gates(fx, _lower_bound(lbl_ref[...], layer))
    next(fills), next(fills), next(fills)

    prev, nxt = _row_neighbours(xbc, row_block)
    cw = cw_ref[...]
    xbc = _silu(cb_ref[...] + prev * cw[0:1] + xbc * cw[1:2] + nxt * cw[2:3])
    xbc16 = xbc.astype(BF16)
    xbc_out[0] = xbc16

    o_a, y, _ = _interleave([
        _hgrn_dir(q, v, log_f * LOG2E, k.astype(BF16), sa_ref, c_ref, False),
        _ssd_dir(xbc16[:, 0:B_WIDTH], xbc16[:, B_WIDTH:B_WIDTH + B_GN], xbc16[:, B_WIDTH + B_GN:B_CONV_CH],
                 dt_raw, dtb_ref[...], alog_ref[...], e2_ref[...], sb_ref, False),
        fills])
    oa_out[0] = o_a
    yb_out[0] = y + dskip_ref[...] * xbc[:, 0:B_WIDTH]

    if emit_state:
        @pl.when(i == pl.num_programs(1) - 1)
        def _():
            _store_states(sa_out, sb_out, sa_ref, sb_ref)


def _fwd_call(x, mod, lp, layer, row_block, mod_row0, mod_stride, init, emit):
    bsz, length, _ = x.shape
    nt = length // TILE

    def tok(width):
        return pl.BlockSpec((1, TILE, width), lambda b, i: (b, i, 0))

    def per_seq(shape):
        return pl.BlockSpec((1,) + shape, lambda b, i: (b,) + (0,) * len(shape))

    in_specs = [tok(D_MODEL), pl.BlockSpec((1, 1, MOD_COLS), lambda b, i: (mod_row0 + mod_stride * b, 0, 0)),
                _const_spec((1, D_MODEL)), _const_spec((D_MODEL, IN_PACKED)), _const_spec((DEPTH, A_KW)),
                _const_spec((3, B_CONV_CH)), _const_spec((1, B_CONV_CH)), _const_spec((1, LANES)),
                _const_spec((1, LANES)), _const_spec((1, B_WIDTH)), _const_spec((2 * LANES, B_WIDTH))]
    args = [x, mod, lp['ln1'], lp['w_in'], lp['lbl'][0], lp['conv_w'], lp['conv_b'], lp['dt_bias'][0],
            lp['a_log'][0], lp['d_skip'], lp['expand']]
    if init is not None:
        in_specs += [per_seq((A_HEADS, A_DV, A_DK)), per_seq((B_GROUPS, B_STATE, 256))]
        args += list(init)
    out_specs = [tok(w) for w, _ in FWD_OUT]
    out_shape = [jax.ShapeDtypeStruct((bsz, length, w), dt) for w, dt in FWD_OUT]
    aliases, n_alias = {}, 0
    if emit is not None:
        out_specs += [pl.BlockSpec((1, 1, 1, A_HEADS, A_DK, A_DV), lambda b, i: (b, layer, 0, 0, 0, 0)),
                      pl.BlockSpec((1, 1, 1, B_HEADS, B_HEADDIM, B_STATE), lambda b, i: (b, layer, 0, 0, 0, 0))]
        out_shape += _state_out_shapes(bsz)
        if emit['acc'] is not None:
            aliases = {len(args): len(FWD_OUT), len(args) + 1: len(FWD_OUT) + 1}
            in_specs += [pl.BlockSpec(memory_space=pl.ANY), pl.BlockSpec(memory_space=pl.ANY)]
            args += list(emit['acc'])
            n_alias = 2
    return pl.pallas_call(
        functools.partial(_fwd_kernel, layer=layer, row_block=row_block, has_init=init is not None,
                          emit_state=emit is not None, n_alias=n_alias),
        grid=(bsz, nt), in_specs=in_specs, out_specs=out_specs, out_shape=out_shape,
        input_output_aliases=aliases, scratch_shapes=_state_scratch(),
        compiler_params=pltpu.CompilerParams(dimension_semantics=("arbitrary", "arbitrary"),
                                             vmem_limit_bytes=VMEM_LIMIT),
        name="mixer_fwd",
    )(*args)


def _bwd_kernel(*refs, layer, has_init, emit_state, n_alias):
    (x_ref, mod_ref, q_ref, v_ref, fbw_ref, xbc_ref, dtbw_ref, ga_ref, z_ref, gates_ref, oa_ref, yb_ref,
     lbl_ref, dtb_ref, alog_ref, an_ref, bn_ref, wa_ref, wb_ref, wo_ref, e2_ref), refs = refs[:21], refs[21:]
    if has_init:
        (sa_in, sb_in), refs = refs[:2], refs[2:]
    else:
        sa_in = sb_in = None
    refs = refs[n_alias:]
    x_out, refs = refs[0], refs[1:]
    if emit_state:
        (sa_out, sb_out), refs = refs[:2], refs[2:]
    sa_ref, sb_ref, c_ref = refs

    i = pl.program_id(1)

    @pl.when(i == 0)
    def _():
        _load_states(sa_in, sb_in, sa_ref, sb_ref)

    q = q_ref[0]
    v = v_ref[0]
    log_f, k = _hgrn_gates(fbw_ref[0], _lower_bound(lbl_ref[...], layer))

    def branch_a():
        o_a = yield from _hgrn_dir(q, v, log_f * LOG2E, k.astype(BF16), sa_ref, c_ref, True)
        o_a = oa_ref[0] + o_a
        an = an_ref[...]
        o_a = jnp.concatenate(
            [_rms(o_a[:, h * A_DV:(h + 1) * A_DV]) * an[:, h * A_DV:(h + 1) * A_DV] for h in range(A_HEADS)],
            axis=1)
        o_a = o_a * _silu(ga_ref[0].astype(F32))
        yield
        return _sigmoid(gates_ref[0, :, 0:D_MODEL].astype(F32)) * _mm(o_a.astype(BF16), wa_ref[...])

    xbc = xbc_ref[0]
    merged_a, y_b = _interleave([
        branch_a(),
        _ssd_dir(xbc[:, 0:B_WIDTH], xbc[:, B_WIDTH:B_WIDTH + B_GN], xbc[:, B_WIDTH + B_GN:B_CONV_CH],
                 dtbw_ref[0], dtb_ref[...], alog_ref[...], e2_ref[...], sb_ref, True)])
    y_b = (yb_ref[0] + y_b) * _silu(z_ref[0].astype(F32))
    bn = bn_ref[...]
    gw = B_WIDTH // B_GROUPS
    y_b = jnp.concatenate(
        [_rms(y_b[:, g * gw:(g + 1) * gw]) * bn[:, g * gw:(g + 1) * gw] for g in range(B_GROUPS)], axis=1)

    merged = merged_a + (_sigmoid(gates_ref[0, :, D_MODEL:2 * D_MODEL].astype(F32))
                         * _mm(y_b.astype(BF16), wb_ref[...]))
    mix = _mm(merged.astype(BF16), wo_ref[...])
    gate1 = mod_ref[0][:, 2 * D_MODEL:3 * D_MODEL]
    x_out[0] = x_ref[0] + gate1 * mix

    if emit_state:
        @pl.when(i == pl.num_programs(1) - 1)
        def _():
            _store_states(sa_out, sb_out, sa_ref, sb_ref)


def _bwd_call(x, mod, fwd_outs, lp, layer, mod_row0, mod_stride, init, emit):
    bsz, length, _ = x.shape
    nt = length // TILE

    def tok(width):
        return pl.BlockSpec((1, TILE, width), lambda b, i: (b, nt - 1 - i, 0))

    def per_seq(shape):
        return pl.BlockSpec((1,) + shape, lambda b, i: (b,) + (0,) * len(shape))

    in_specs = ([tok(D_MODEL), pl.BlockSpec((1, 1, MOD_COLS), lambda b, i: (mod_row0 + mod_stride * b, 0, 0))]
                + [tok(w) for w, _ in FWD_OUT]
                + [_const_spec((DEPTH, A_KW)), _const_spec((1, LANES)), _const_spec((1, LANES)),
                   _const_spec((1, A_WIDTH)), _const_spec((1, B_WIDTH)), _const_spec((A_WIDTH, D_MODEL)),
                   _const_spec((B_WIDTH, D_MODEL)), _const_spec((D_MODEL, D_MODEL)),
                   _const_spec((2 * LANES, B_WIDTH))])
    args = ([x, mod] + list(fwd_outs)
            + [lp['lbl'][1], lp['dt_bias'][1], lp['a_log'][1], lp['a_norm'], lp['b_norm'], lp['w_br_a'],
               lp['w_br_b'], lp['w_out'], lp['expand']])
    if init is not None:
        in_specs += [per_seq((A_HEADS, A_DV, A_DK)), per_seq((B_GROUPS, B_STATE, 256))]
        args += list(init)
    out_specs = [tok(D_MODEL)]
    out_shape = [jax.ShapeDtypeStruct((bsz, length, D_MODEL), F32)]
    aliases, n_alias = {}, 0
    if emit is not None:
        out_specs += [pl.BlockSpec((1, 1, 1, A_HEADS, A_DK, A_DV), lambda b, i: (b, layer, 1, 0, 0, 0)),
                      pl.BlockSpec((1, 1, 1, B_HEADS, B_HEADDIM, B_STATE), lambda b, i: (b, layer, 1, 0, 0, 0))]
        out_shape += _state_out_shapes(bsz)
        aliases = {len(args): 1, len(args) + 1: 2}
        in_specs += [pl.BlockSpec(memory_space=pl.ANY), pl.BlockSpec(memory_space=pl.ANY)]
        args += list(emit['acc'])
        n_alias = 2
    return pl.pallas_call(
        functools.partial(_bwd_kernel, layer=layer, has_init=init is not None, emit_state=emit is not None,
                          n_alias=n_alias),
        grid=(bsz, nt), in_specs=in_specs, out_specs=out_specs, out_shape=out_shape,
        input_output_aliases=aliases, scratch_shapes=_state_scratch(),
        compiler_params=pltpu.CompilerParams(dimension_semantics=("arbitrary", "arbitrary"),
                                             vmem_limit_bytes=VMEM_LIMIT),
        name="mixer_bwd",
    )(*args)


def _ffn_kernel(x_ref, mod_ref, ln_ref, wu_ref, cw_ref, cb_ref, wd_ref, lnf_ref, o_ref, *, row_block, final_norm):
    x = x_ref[0]
    mod = mod_ref[0]
    shift, scale, gate = (mod[:, 3 * D_MODEL:4 * D_MODEL], mod[:, 4 * D_MODEL:5 * D_MODEL],
                          mod[:, 5 * D_MODEL:6 * D_MODEL])
    u = ((_rms(x) * ln_ref[...]) * (1.0 + scale) + shift).astype(BF16)

    def up(j):
        return (_mm(u, wu_ref[:, j * FF_BLOCK:(j + 1) * FF_BLOCK]),
                _mm(u, wu_ref[:, D_FF + j * FF_BLOCK:D_FF + (j + 1) * FF_BLOCK]))

    def conv(h, lo):
        cw = cw_ref[:, lo:lo + FF_BLOCK]
        prev, nxt = _row_neighbours(h, row_block)
        return cb_ref[:, lo:lo + FF_BLOCK] + prev * cw[0:1] + h * cw[1:2] + nxt * cw[2:3]

    steps = D_FF // FF_BLOCK
    acc = None
    acts = []
    h = up(0)
    for j in range(steps):
        h_next = up(j + 1) if j + 1 < steps else None
        acts.append((_silu(conv(h[0], j * FF_BLOCK)) * conv(h[1], D_FF + j * FF_BLOCK)).astype(BF16))
        if len(acts) == FF_DOWN_BLOCKS or j + 1 == steps:
            lo = (j + 1 - len(acts)) * FF_BLOCK
            part = _mm(acts[0] if len(acts) == 1 else jnp.concatenate(acts, axis=1),
                       wd_ref[lo:(j + 1) * FF_BLOCK, :])
            acc = part if acc is None else acc + part
            acts = []
        h = h_next
    y = x + gate * acc
    if final_norm:
        y = _rms(y) * lnf_ref[...]
    o_ref[0] = y


def _ffn_call(x, mod, lp, ln_f, row_block, mod_row0, mod_stride, final_norm):
    bsz, length, _ = x.shape
    nt = length // TILE
    tok = pl.BlockSpec((1, TILE, D_MODEL), lambda b, i: (b, i, 0))
    return pl.pallas_call(
        functools.partial(_ffn_kernel, row_block=row_block, final_norm=final_norm),
        grid=(bsz, nt),
        in_specs=[tok, pl.BlockSpec((1, 1, MOD_COLS), lambda b, i: (mod_row0 + mod_stride * b, 0, 0)),
                  _const_spec((1, D_MODEL)), _const_spec((D_MODEL, 2 * D_FF)), _const_spec((3, 2 * D_FF)),
                  _const_spec((1, 2 * D_FF)), _const_spec((D_FF, D_MODEL)), _const_spec((1, D_MODEL))],
        out_specs=tok,
        out_shape=jax.ShapeDtypeStruct((bsz, length, D_MODEL), F32),
        compiler_params=pltpu.CompilerParams(dimension_semantics=("arbitrary", "arbitrary"),
                                             vmem_limit_bytes=VMEM_LIMIT),
        name="conv_ffn",
    )(x, mod, lp['ln2'], lp['w_ff_up'], lp['ff_conv_w'], lp['ff_conv_b'], lp['w_ff_down'], ln_f)


def _pad_heads(a):
    return jnp.pad(a.astype(F32), [(0, 0)] * (a.ndim - 1) + [(0, LANES - a.shape[-1])])


def _pack_w_in(w):
    src_gates = C_XBC + B_CONV_CH + 2 * B_HEADS
    pad = jnp.zeros((w.shape[0], LANES - B_HEADS), w.dtype)
    return jnp.concatenate([w[:, :C_GATES], w[:, src_gates:], w[:, C_GATES:C_GATES + B_HEADS], pad,
                            w[:, C_GATES + B_HEADS:src_gates], pad], axis=1).astype(BF16)


def _hgrn_state_in(s):
    return jnp.swapaxes(s.astype(F32), -1, -2)


def _ssd_state_in(s):
    b = s.shape[0]
    s = s.astype(F32).reshape(b, B_GROUPS, B_HEADS // B_GROUPS, B_HEADDIM, B_STATE)
    return s.transpose(0, 1, 4, 2, 3).reshape(b, B_GROUPS, B_STATE, 256)


def kernel(x_prompt, x_sample, c, state_hgrn, state_ssd, c_ctx, w_ada, b_ada, ln1, ln2, ln_f, w_in, lb_logits,
           a_norm, conv_w, conv_b, dt_bias, a_log, d_skip, b_norm, w_br_a, w_br_b, w_out, w_ff_up, ff_conv_w,
           ff_conv_b, w_ff_down):
    depth = w_in.shape[0]
    dec_b = x_sample.shape[0]
    assert depth == DEPTH and dec_b + 1 <= MOD_ROWS
    assert x_prompt.shape[1] == TILE and x_sample.shape[1] % TILE == 0 and TILE % GRID_W == 0

    cvec = jnp.concatenate([c_ctx[None].astype(F32), c.astype(F32),
                            jnp.zeros((MOD_ROWS - 1 - dec_b, D_MODEL), F32)], axis=0)
    mod_all = _modulation(cvec, w_ada.astype(F32), b_ada.astype(F32))

    expand = (jnp.arange(2 * LANES)[:, None] % LANES == jnp.arange(B_WIDTH)[None, :] // B_HEADDIM).astype(BF16)
    layers = []
    for l in range(depth):
        layers.append(dict(
            ln1=ln1[l][None].astype(F32), ln2=ln2[l][None].astype(F32),
            w_in=_pack_w_in(w_in[l]),
            lbl=(lb_logits[:, 0, :].astype(F32), lb_logits[:, 1, :].astype(F32)),
            a_norm=a_norm[l][None].astype(F32), b_norm=b_norm[l][None].astype(F32),
            conv_w=conv_w[l].astype(F32), conv_b=conv_b[l][None].astype(F32),
            dt_bias=(_pad_heads(dt_bias[l, 0])[None], _pad_heads(dt_bias[l, 1])[None]),
            a_log=(_pad_heads(a_log[l, 0])[None], _pad_heads(a_log[l, 1])[None]),
            d_skip=jnp.repeat(d_skip[l].astype(F32), B_HEADDIM)[None], expand=expand,
            w_br_a=w_br_a[l].astype(BF16), w_br_b=w_br_b[l].astype(BF16), w_out=w_out[l].astype(BF16),
            w_ff_up=w_ff_up[l].astype(BF16), ff_conv_w=ff_conv_w[l].astype(F32),
            ff_conv_b=ff_conv_b[l][None].astype(F32), w_ff_down=w_ff_down[l].astype(BF16)))
    ln_f2 = ln_f[None].astype(F32)

    def run(x, row_block, mod_row0, mod_stride, states, emit_state):
        acc = None
        for l in range(depth):
            lp = layers[l]
            mod = mod_all[l].reshape(MOD_ROWS, 1, MOD_COLS)
            init_f = init_b = None
            if states is not None:
                init_f = (_hgrn_state_in(states[0][:, l, 0]), _ssd_state_in(states[1][:, l, 0]))
                init_b = (_hgrn_state_in(states[0][:, l, 1]), _ssd_state_in(states[1][:, l, 1]))
            fo = _fwd_call(x, mod, lp, l, row_block, mod_row0, mod_stride, init_f,
                           dict(acc=acc) if emit_state else None)
            if emit_state:
                acc = (fo[10], fo[11])
            bo = _bwd_call(x, mod, fo[:10], lp, l, mod_row0, mod_stride, init_b,
                           dict(acc=acc) if emit_state else None)
            if emit_state:
                acc = (bo[1], bo[2])
            x = _ffn_call(bo[0], mod, lp, ln_f2, row_block, mod_row0, mod_stride, l == depth - 1)
        return x, acc

    y_prompt, (new_hgrn, new_ssd) = run(x_prompt.astype(F32), x_prompt.shape[1], 0, 0, None, True)
    y_sample, _ = run(x_sample.astype(F32), GRID_W, 1, 1, (state_hgrn, state_ssd), False)
    return (y_prompt.astype(x_prompt.dtype), y_sample.astype(x_sample.dtype),
            new_hgrn.astype(x_prompt.dtype), new_ssd.astype(x_prompt.dtype))
```

```python
import functools

import jax
import jax.numpy as jnp
from jax import lax
from jax.experimental import pallas as pl
from jax.experimental.pallas import tpu as pltpu

F32 = jnp.float32
BF16 = jnp.bfloat16

D_MODEL = 1024
DEPTH = 4
GRID_W = 64
A_HEADS = 4
A_DK = 128
A_DV = 128
A_KW = A_HEADS * A_DK
A_WIDTH = A_HEADS * A_DV
B_HEADS = 16
B_HEADDIM = 64
B_WIDTH = B_HEADS * B_HEADDIM
B_GROUPS = 4
B_STATE = 64
B_GN = B_GROUPS * B_STATE
B_CONV_CH = B_WIDTH + 2 * B_GN
D_FF = 2816
N_MOD = 6
EPS = 1e-6
LOG2E = 1.4426950408889634

LANES = 128
SUBLANES = 8
TILE = 256
MOD_ROWS = 8
MOD_COLS = N_MOD * D_MODEL
MOD_BLOCK = 1536
FF_TILE = 512
FF_BLOCK = 256
FF_DOWN_BLOCKS = 2
FILL_COLS = 512
VMEM_LIMIT = 60 * 1024 * 1024

C_Q = 0
C_I = C_Q + A_KW
C_FFW = C_I + A_WIDTH
C_FBW = C_FFW + A_KW
C_GA = C_FBW + A_KW
C_Z = C_GA + A_WIDTH
C_XBC = C_Z + B_WIDTH
C_GATES = C_XBC + B_CONV_CH
C_DTFW = C_GATES + 2 * D_MODEL
C_DTBW = C_DTFW + LANES
IN_PACKED = C_DTBW + LANES

FWD_OUT = ((A_KW, BF16), (A_WIDTH, BF16), (A_KW, F32), (B_CONV_CH, BF16), (LANES, F32), (A_WIDTH, BF16),
           (B_WIDTH, BF16), (2 * D_MODEL, BF16), (A_WIDTH, F32), (B_WIDTH, F32), (A_KW, BF16))

NT_DIMS = (((1,), (1,)), ((), ()))
TN_DIMS = (((0,), (0,)), ((), ()))


def _mm(a, b):
    return jnp.dot(a, b, preferred_element_type=F32)


def _mm_nt(a, b):
    return lax.dot_general(a, b, NT_DIMS, preferred_element_type=F32)


def _mm_tn(a, b):
    return lax.dot_general(a, b, TN_DIMS, preferred_element_type=F32)


def _sigmoid(x):
    return 0.5 * jnp.tanh(0.5 * x) + 0.5


def _silu(x):
    return x * _sigmoid(x)


def _softplus(x):
    return jnp.maximum(x, 0.0) + jnp.log1p(jnp.exp(-jnp.abs(x)))


def _rms(x):
    return x * lax.rsqrt(jnp.mean(x * x, axis=-1, keepdims=True) + EPS)


def _iota(shape, dim):
    return lax.broadcasted_iota(jnp.int32, shape, dim)


def _zero_rows(a, period, offset):
    sub = _iota((SUBLANES, a.shape[1]), 0)
    groups = []
    for g in range(a.shape[0] // SUBLANES):
        blk = a[g * SUBLANES:(g + 1) * SUBLANES]
        if (g * SUBLANES) % period == offset - offset % SUBLANES:
            blk = jnp.where(sub == offset % SUBLANES, 0.0, blk)
        groups.append(blk)
    return jnp.concatenate(groups, axis=0)


def _row_neighbours(h, row_block):
    n = h.shape[0]
    return (_zero_rows(pltpu.roll(h, 1, axis=0), row_block, 0),
            _zero_rows(pltpu.roll(h, n - 1, axis=0), row_block, row_block - 1))


def _split_bf16(x, parts):
    out = []
    for i in range(parts):
        if i + 1 < parts:
            p = lax.bitcast_convert_type(lax.bitcast_convert_type(x, jnp.uint32) & jnp.uint32(0xFFFF0000), F32)
            out.append(p.astype(BF16))
            x = x - p
        else:
            out.append(x.astype(BF16))
    return out


def _scan_rows(g, reverse):
    n = g.shape[0]
    row = _iota((n, n), 0)
    col = _iota((n, n), 1)
    tri = jnp.where((col >= row) if reverse else (col <= row), 1.0, 0.0).astype(BF16)
    return _mm(jnp.concatenate([tri, tri, tri], axis=1), jnp.concatenate(_split_bf16(g, 3), axis=0))


def _level_ref(c, read_row, m, off):
    n, w = c.shape
    if 2 * m < 8:
        rk = _iota(c.shape, 0) & (2 * m - 1)
        out = c
        for k in range(2 * m):
            if k != off:
                out = jnp.where(rk == k, pltpu.roll(c, (k - off) % n, axis=0), out)
        return out
    pieces = [jnp.broadcast_to(read_row(j * 2 * m + off), (2 * m, w)) for j in range(n // (2 * m))]
    return pieces[0] if len(pieces) == 1 else jnp.concatenate(pieces, axis=0)


def _neg_abs(x):
    bits = lax.bitcast_convert_type(x, jnp.uint32) | jnp.uint32(0x80000000)
    return lax.bitcast_convert_type(bits, F32)


def _pair_level(n, reverse):
    t = _iota((n, n), 0)
    s = _iota((n, n), 1)
    x = t ^ s
    lvl = jnp.full((n, n), -1, jnp.int32)
    for l in range(n.bit_length() - 1):
        lvl = lvl + (x >= (1 << l)).astype(jnp.int32)
    return jnp.where((s > t) if reverse else (s < t), lvl, -1)


def _hgrn_gates(fx, lb):
    log_sig = jnp.minimum(fx, 0.0) - jnp.log(1.0 + jnp.exp(-jnp.abs(fx)))
    a = jnp.log(lb)
    b = jnp.log1p(-lb) + log_sig
    log_f = jnp.maximum(a, b) + jnp.log(1.0 + jnp.exp(-jnp.abs(a - b)))
    return log_f, (1.0 - lb) * _sigmoid(-fx)


def _lower_bound(logits, layer):
    e = jnp.exp(logits - jnp.max(logits, axis=0, keepdims=True))
    sm = e / jnp.sum(e, axis=0, keepdims=True)
    acc = jnp.zeros_like(sm[0:1])
    for j in range(1, layer + 1):
        acc = acc + sm[j:j + 1]
    return jnp.maximum(acc, 0.0)


def _interleave(gens):
    results = [None] * len(gens)
    live = list(range(len(gens)))
    while live:
        for idx in list(live):
            try:
                next(gens[idx])
            except StopIteration as stop:
                results[idx] = stop.value
                live.remove(idx)
    return results


def _hgrn_dir(q, v, log2_f, k, st_ref, c_ref, reverse):
    n = q.shape[0]
    hb = n // 2
    top = n.bit_length() - 2
    lvl = _pair_level(hb, reverse)
    ones = jnp.ones((LANES, LANES), BF16)
    c_all = _scan_rows(log2_f, reverse)
    zeros = jnp.zeros((hb, LANES), BF16)
    yield
    outs = []
    for pair in range(A_HEADS // 2):
        sl2 = slice(2 * pair * LANES, (2 * pair + 2) * LANES)
        c2 = c_all[:, sl2]
        c_ref[pair] = c2
        read_row = lambda r, pair=pair: c_ref[pair, pl.ds(r, 1), :]

        def scores(qt, kt):
            rhs = jnp.concatenate([jnp.concatenate([kt[:, :LANES], zeros], axis=1),
                                   jnp.concatenate([zeros, kt[:, LANES:]], axis=1)], axis=0)
            return _mm_nt(qt, rhs)

        s_lo = s_hi = None
        for l in range(top + 1):
            m = 1 << l
            w = jnp.exp2(_neg_abs(c2 - _level_ref(c2, read_row, m, m if reverse else m - 1))).astype(BF16)
            qt = q[:, sl2] * w
            kt = k[:, sl2] * w
            if l < top:
                a = scores(qt[:hb], kt[:hb])
                b = scores(qt[hb:], kt[hb:])
                keep = jnp.concatenate([lvl, lvl], axis=1) == l
                s_lo = jnp.where(keep, a, 0.0 if s_lo is None else s_lo)
                s_hi = jnp.where(keep, b, 0.0 if s_hi is None else s_hi)
            elif reverse:
                s_off = scores(qt[:hb], kt[hb:])
            else:
                s_off = scores(qt[hb:], kt[:hb])
            yield
        c_end2 = read_row(0 if reverse else n - 1)
        for e in range(2):
            h = 2 * pair + e
            sl = slice(h * LANES, (h + 1) * LANES)
            se = slice(e * LANES, (e + 1) * LANES)
            qh, vh, kh = q[:, sl], v[:, sl], k[:, sl]
            c, c_end = c2[:, se], c_end2[:, se]
            if reverse:
                o = jnp.concatenate([_mm(jnp.concatenate([s_lo[:, se], s_off[:, se]], axis=1).astype(BF16), vh),
                                     _mm(s_hi[:, se].astype(BF16), vh[hb:])], axis=0)
            else:
                o = jnp.concatenate([_mm(s_lo[:, se].astype(BF16), vh[:hb]),
                                     _mm(jnp.concatenate([s_off[:, se], s_hi[:, se]], axis=1).astype(BF16), vh)],
                                    axis=0)
            st = st_ref[h]
            o = o + _mm(qh * kh, ones) * vh.astype(F32)
            o = o + _mm_nt(qh * jnp.exp2(c).astype(BF16), st.astype(BF16))
            st_ref[h] = st * jnp.exp2(c_end) + _mm_tn(vh, kh * jnp.exp2(c_end - c).astype(BF16))
            outs.append(o)
            yield
    return jnp.concatenate(outs, axis=1)


def _expand_heads(a):
    r = a.shape[0]
    lane = _iota((r, LANES), 1)
    slabs = []
    for j in range(B_HEADS // 2):
        lo = jnp.broadcast_to(a[:, 2 * j:2 * j + 1], (r, LANES))
        hi = jnp.broadcast_to(a[:, 2 * j + 1:2 * j + 2], (r, LANES))
        slabs.append(jnp.where(lane < B_HEADDIM, lo, hi))
    return jnp.concatenate(slabs, axis=1)


def _expand_heads_mxu(a, e2):
    a = jnp.where(_iota(a.shape, 1) < B_HEADS, a, 0.0)
    return _mm(jnp.concatenate(_split_bf16(a, 2), axis=1), e2)


def _ssd_dir(xs, bm, cm, dt_raw, dt_bias, a_log, e2, st_ref, reverse):
    n = xs.shape[0]
    dt = _softplus(dt_raw + dt_bias)
    ac = _scan_rows((-LOG2E) * jnp.exp(a_log) * dt, reverse)
    ac_t = ac.T
    dt_t = dt.T
    hb = n // 2
    row_s = _iota((hb, hb), 0)
    col_s = _iota((hb, hb), 1)
    tri = (col_s >= row_s) if reverse else (col_s <= row_s)
    lane_gn = _iota((n, B_GN), 1)
    lane = _iota((n, LANES), 1)
    lo, hi = slice(0, hb), slice(hb, n)
    yield
    slabs = []
    for g in range(B_GROUPS):
        cb = _mm_nt(jnp.where((lane_gn >> 6) == g, cm, jnp.zeros_like(cm)), bm)
        yield
        for jp in range(2):
            slab = xs[:, (2 * g + jp) * LANES:(2 * g + jp + 1) * LANES]
            zero = jnp.zeros_like(slab)
            slab_e = (jnp.where(lane < B_HEADDIM, slab, zero), jnp.where(lane < B_HEADDIM, zero, slab))
            w_lo, w_hi = [], []
            for e in range(2):
                h = 4 * g + 2 * jp + e
                col = ac[:, h:h + 1]
                row = ac_t[h:h + 1, :]
                dt_row = dt_t[h:h + 1, :]

                def quad(rs, cs, masked):
                    seg = col[rs] - row[:, cs]
                    if masked:
                        seg = jnp.where(tri, seg, -1e30)
                    return (jnp.exp2(seg) * dt_row[:, cs] * cb[rs, cs]).astype(BF16)

                if reverse:
                    w_lo += [quad(lo, lo, True), quad(lo, hi, False)]
                    w_hi += [quad(hi, hi, True)]
                else:
                    w_lo += [quad(lo, lo, True)]
                    w_hi += [quad(hi, lo, False), quad(hi, hi, True)]
                yield
            if reverse:
                x_lo = jnp.concatenate([slab_e[0], slab_e[1]], axis=0)
                x_hi = jnp.concatenate([slab_e[0][hi], slab_e[1][hi]], axis=0)
            else:
                x_lo = jnp.concatenate([slab_e[0][lo], slab_e[1][lo]], axis=0)
                x_hi = jnp.concatenate([slab_e[0], slab_e[1]], axis=0)
            slabs.append(jnp.concatenate([_mm(jnp.concatenate(w_lo, axis=1), x_lo),
                                          _mm(jnp.concatenate(w_hi, axis=1), x_hi)], axis=0))
    y = jnp.concatenate(slabs, axis=1)
    st = st_ref[...]
    y = y + _mm(cm, st.astype(BF16)) * _expand_heads_mxu(jnp.exp2(ac), e2)
    yield
    a_end = ac[0:1, :] if reverse else ac[n - 1:n, :]
    wgt = _expand_heads_mxu(jnp.exp2(a_end - ac) * dt, e2)
    upd = _mm_tn(bm, xs * wgt.astype(BF16))
    block = (_iota(st.shape, 0) >> 6) == (_iota(st.shape, 1) >> 8)
    st_ref[...] = jnp.where(block, st * _expand_heads(jnp.exp2(a_end)) + upd, 0.0)
    return y


def _load_states(sa_in, sb_in, sa_ref, sb_ref):
    if sa_in is None:
        sa_ref[...] = jnp.zeros_like(sa_ref)
        sb_ref[...] = jnp.zeros_like(sb_ref)
    else:
        sa_ref[...] = sa_in[0]
        sb_ref[...] = jnp.zeros_like(sb_ref)
        for g in range(B_GROUPS):
            sb_ref[g * B_STATE:(g + 1) * B_STATE, g * 256:(g + 1) * 256] = sb_in[0, g]


def _store_states(sa_out, sb_out, sa_ref, sb_ref):
    for h in range(A_HEADS):
        sa_out[0, 0, 0, h] = sa_ref[h].T
    per = B_HEADS // B_GROUPS
    for pair in range(B_GROUPS // 2):
        t = sb_ref[pair * 2 * B_STATE:(pair + 1) * 2 * B_STATE, pair * 2 * 256:(pair + 1) * 2 * 256].T
        for gl in range(2):
            for j in range(per):
                r0 = gl * 256 + j * B_HEADDIM
                sb_out[0, 0, 0, (2 * pair + gl) * per + j] = t[r0:r0 + B_HEADDIM, gl * B_STATE:(gl + 1) * B_STATE]


def _mod_kernel(c_ref, w_ref, b_ref, o_ref):
    o_ref[0] = jnp.dot(_silu(c_ref[...]), w_ref[0], preferred_element_type=F32,
                       precision=lax.Precision.HIGHEST) + b_ref[0]


def _modulation(cvec, w_ada, b_ada):
    depth = w_ada.shape[0]
    return pl.pallas_call(
        _mod_kernel,
        grid=(depth, MOD_COLS // MOD_BLOCK),
        in_specs=[
            pl.BlockSpec((MOD_ROWS, D_MODEL), lambda l, j: (0, 0)),
            pl.BlockSpec((1, D_MODEL, MOD_BLOCK), lambda l, j: (l, 0, j)),
            pl.BlockSpec((1, 1, MOD_BLOCK), lambda l, j: (l, 0, j)),
        ],
        out_specs=pl.BlockSpec((1, MOD_ROWS, MOD_BLOCK), lambda l, j: (l, 0, j)),
        out_shape=jax.ShapeDtypeStruct((depth, MOD_ROWS, MOD_COLS), F32),
        compiler_params=pltpu.CompilerParams(dimension_semantics=("arbitrary", "arbitrary"),
                                             vmem_limit_bytes=VMEM_LIMIT),
        name="adaln_mod",
    )(cvec, w_ada, b_ada.reshape(depth, 1, MOD_COLS))


def _const_spec(shape):
    nd = len(shape)
    return pl.BlockSpec(shape, lambda *_: (0,) * nd, pipeline_mode=pl.Buffered(1))


def _state_scratch():
    return [pltpu.VMEM((A_HEADS, A_DV, A_DK), F32), pltpu.VMEM((B_GN, B_WIDTH), F32),
            pltpu.VMEM((A_HEADS // 2, TILE, 2 * LANES), F32)]


def _state_out_shapes(bsz):
    return [jax.ShapeDtypeStruct((bsz, DEPTH, 2, A_HEADS, A_DK, A_DV), F32),
            jax.ShapeDtypeStruct((bsz, DEPTH, 2, B_HEADS, B_HEADDIM, B_STATE), F32)]


def _fwd_kernel(*refs, layer, row_block, has_init, emit_state, n_alias):
    (x_ref, mod_ref, ln_ref, w_ref, lbl_ref, lblb_ref, cw_ref, cb_ref, dtb_ref, alog_ref, dskip_ref,
     e2_ref), refs = refs[:12], refs[12:]
    if has_init:
        (sa_in, sb_in), refs = refs[:2], refs[2:]
    else:
        sa_in = sb_in = None
    refs = refs[n_alias:]
    (q_out, v_out, l2fbw_out, xbc_out, dtbw_out, ga_out, z_out, gates_out, oa_out, yb_out,
     kbw_out), refs = refs[:11], refs[11:]
    if emit_state:
        (sa_out, sb_out), refs = refs[:2], refs[2:]
    sa_ref, sb_ref, c_ref = refs

    i = pl.program_id(1)

    @pl.when(i == 0)
    def _():
        _load_states(sa_in, sb_in, sa_ref, sb_ref)

    x = x_ref[0]
    mod = mod_ref[0]
    shift, scale = mod[:, 0:D_MODEL], mod[:, D_MODEL:2 * D_MODEL]
    u = ((_rms(x) * ln_ref[...]) * (1.0 + scale) + shift).astype(BF16)

    def proj(lo, hi):
        return _mm(u, w_ref[:, lo:hi])

    def backward_gates(fx):
        log_f, k = _hgrn_gates(fx, _lower_bound(lblb_ref[...], layer))
        kbw_out[0] = k.astype(BF16)
        return log_f * LOG2E

    pending = [(ga_out, 0, C_GA, A_WIDTH, _silu), (l2fbw_out, 0, C_FBW, A_KW, backward_gates)]
    pending += [(z_out, o, C_Z + o, FILL_COLS, _silu) for o in range(0, B_WIDTH, FILL_COLS)]
    pending += [(gates_out, o, C_GATES + o, FILL_COLS, _sigmoid) for o in range(0, 2 * D_MODEL, FILL_COLS)]
    pending.append((dtbw_out, 0, C_DTBW, LANES, lambda raw: raw))

    def deferred():
        for ref, dst, src, width, fn in pending:
            raw = proj(src, src + width)
            yield
            ref[0, :, dst:dst + width] = fn(raw).astype(ref.dtype)
            yield
            yield

    fills = deferred()
    q = proj(C_Q, C_I)
    v = proj(C_I, C_FFW)
    fx = proj(C_FFW, C_FBW)
    xbc = proj(C_XBC, C_GATES)
    dt_raw = proj(C_DTFW, C_DTBW)
    next(fills)
    q = _silu(q).astype(BF16)
    v = v.astype(BF16)
    q_out[0] = q
    v_out[0] = v
    log_f, k = _hgrn_gates(fx, _lower_bound(lbl_ref[...], layer))
    next(fills), next(fills), next(fills)

    prev, nxt = _row_neighbours(xbc, row_block)
    cw = cw_ref[...]
    xbc = _silu(cb_ref[...] + prev * cw[0:1] + xbc * cw[1:2] + nxt * cw[2:3])
    xbc16 = xbc.astype(BF16)
    xbc_out[0] = xbc16

    o_a, y, _ = _interleave([
        _hgrn_dir(q, v, log_f * LOG2E, k.astype(BF16), sa_ref, c_ref, False),
        _ssd_dir(xbc16[:, 0:B_WIDTH], xbc16[:, B_WIDTH:B_WIDTH + B_GN], xbc16[:, B_WIDTH + B_GN:B_CONV_CH],
                 dt_raw, dtb_ref[...], alog_ref[...], e2_ref[...], sb_ref, False),
        fills])
    oa_out[0] = o_a
    yb_out[0] = y + dskip_ref[...] * xbc[:, 0:B_WIDTH]

    if emit_state:
        @pl.when(i == pl.num_programs(1) - 1)
        def _():
            _store_states(sa_out, sb_out, sa_ref, sb_ref)


def _fwd_call(x, mod, lp, layer, row_block, mod_row0, mod_stride, init, emit):
    bsz, length, _ = x.shape
    nt = length // TILE

    def tok(width):
        return pl.BlockSpec((1, TILE, width), lambda b, i: (b, i, 0))

    def per_seq(shape):
        return pl.BlockSpec((1,) + shape, lambda b, i: (b,) + (0,) * len(shape))

    in_specs = [tok(D_MODEL), pl.BlockSpec((1, 1, MOD_COLS), lambda b, i: (mod_row0 + mod_stride * b, 0, 0)),
                _const_spec((1, D_MODEL)), _const_spec((D_MODEL, IN_PACKED)), _const_spec((DEPTH, A_KW)),
                _const_spec((DEPTH, A_KW)), _const_spec((3, B_CONV_CH)), _const_spec((1, B_CONV_CH)), _const_spec((1, LANES)),
                _const_spec((1, LANES)), _const_spec((1, B_WIDTH)), _const_spec((2 * LANES, B_WIDTH))]
    args = [x, mod, lp['ln1'], lp['w_in'], lp['lbl'][0], lp['lbl'][1], lp['conv_w'], lp['conv_b'], lp['dt_bias'][0],
            lp['a_log'][0], lp['d_skip'], lp['expand']]
    if init is not None:
        in_specs += [per_seq((A_HEADS, A_DV, A_DK)), per_seq((B_GROUPS, B_STATE, 256))]
        args += list(init)
    out_specs = [tok(w) for w, _ in FWD_OUT]
    out_shape = [jax.ShapeDtypeStruct((bsz, length, w), dt) for w, dt in FWD_OUT]
    aliases, n_alias = {}, 0
    if emit is not None:
        out_specs += [pl.BlockSpec((1, 1, 1, A_HEADS, A_DK, A_DV), lambda b, i: (b, layer, 0, 0, 0, 0)),
                      pl.BlockSpec((1, 1, 1, B_HEADS, B_HEADDIM, B_STATE), lambda b, i: (b, layer, 0, 0, 0, 0))]
        out_shape += _state_out_shapes(bsz)
        if emit['acc'] is not None:
            aliases = {len(args): len(FWD_OUT), len(args) + 1: len(FWD_OUT) + 1}
            in_specs += [pl.BlockSpec(memory_space=pl.ANY), pl.BlockSpec(memory_space=pl.ANY)]
            args += list(emit['acc'])
            n_alias = 2
    return pl.pallas_call(
        functools.partial(_fwd_kernel, layer=layer, row_block=row_block, has_init=init is not None,
                          emit_state=emit is not None, n_alias=n_alias),
        grid=(bsz, nt), in_specs=in_specs, out_specs=out_specs, out_shape=out_shape,
        input_output_aliases=aliases, scratch_shapes=_state_scratch(),
        compiler_params=pltpu.CompilerParams(dimension_semantics=("arbitrary", "arbitrary"),
                                             vmem_limit_bytes=VMEM_LIMIT),
        name="mixer_fwd",
    )(*args)


def _bwd_kernel(*refs, has_init, emit_state, n_alias):
    (x_ref, mod_ref, q_ref, v_ref, l2fbw_ref, xbc_ref, dtbw_ref, ga_ref, z_ref, gates_ref, oa_ref, yb_ref, kbw_ref,
     dtb_ref, alog_ref, an_ref, bn_ref, wa_ref, wb_ref, wo_ref, e2_ref), refs = refs[:21], refs[21:]
    if has_init:
        (sa_in, sb_in), refs = refs[:2], refs[2:]
    else:
        sa_in = sb_in = None
    refs = refs[n_alias:]
    x_out, refs = refs[0], refs[1:]
    if emit_state:
        (sa_out, sb_out), refs = refs[:2], refs[2:]
    sa_ref, sb_ref, c_ref = refs

    i = pl.program_id(1)

    @pl.when(i == 0)
    def _():
        _load_states(sa_in, sb_in, sa_ref, sb_ref)

    def branch_a():
        o_a = yield from _hgrn_dir(q_ref[0], v_ref[0], l2fbw_ref[0], kbw_ref[0], sa_ref, c_ref, True)
        o_a = oa_ref[0] + o_a
        an = an_ref[...]
        o_a = jnp.concatenate(
            [_rms(o_a[:, h * A_DV:(h + 1) * A_DV]) * an[:, h * A_DV:(h + 1) * A_DV] for h in range(A_HEADS)],
            axis=1)
        o_a = o_a * ga_ref[0].astype(F32)
        yield
        return gates_ref[0, :, 0:D_MODEL].astype(F32) * _mm(o_a.astype(BF16), wa_ref[...])

    xbc = xbc_ref[0]
    merged_a, y_b = _interleave([
        branch_a(),
        _ssd_dir(xbc[:, 0:B_WIDTH], xbc[:, B_WIDTH:B_WIDTH + B_GN], xbc[:, B_WIDTH + B_GN:B_CONV_CH],
                 dtbw_ref[0], dtb_ref[...], alog_ref[...], e2_ref[...], sb_ref, True)])
    y_b = (yb_ref[0] + y_b) * z_ref[0].astype(F32)
    bn = bn_ref[...]
    gw = B_WIDTH // B_GROUPS
    y_b = jnp.concatenate(
        [_rms(y_b[:, g * gw:(g + 1) * gw]) * bn[:, g * gw:(g + 1) * gw] for g in range(B_GROUPS)], axis=1)

    merged = merged_a + gates_ref[0, :, D_MODEL:2 * D_MODEL].astype(F32) * _mm(y_b.astype(BF16), wb_ref[...])
    mix = _mm(merged.astype(BF16), wo_ref[...])
    gate1 = mod_ref[0][:, 2 * D_MODEL:3 * D_MODEL]
    x_out[0] = x_ref[0] + gate1 * mix

    if emit_state:
        @pl.when(i == pl.num_programs(1) - 1)
        def _():
            _store_states(sa_out, sb_out, sa_ref, sb_ref)


def _bwd_call(x, mod, fwd_outs, lp, layer, mod_row0, mod_stride, init, emit):
    bsz, length, _ = x.shape
    nt = length // TILE

    def tok(width):
        return pl.BlockSpec((1, TILE, width), lambda b, i: (b, nt - 1 - i, 0))

    def per_seq(shape):
        return pl.BlockSpec((1,) + shape, lambda b, i: (b,) + (0,) * len(shape))

    in_specs = ([tok(D_MODEL), pl.BlockSpec((1, 1, MOD_COLS), lambda b, i: (mod_row0 + mod_stride * b, 0, 0))]
                + [tok(w) for w, _ in FWD_OUT]
                + [_const_spec((1, LANES)), _const_spec((1, LANES)),
                   _const_spec((1, A_WIDTH)), _const_spec((1, B_WIDTH)), _const_spec((A_WIDTH, D_MODEL)),
                   _const_spec((B_WIDTH, D_MODEL)), _const_spec((D_MODEL, D_MODEL)),
                   _const_spec((2 * LANES, B_WIDTH))])
    args = ([x, mod] + list(fwd_outs)
            + [lp['dt_bias'][1], lp['a_log'][1], lp['a_norm'], lp['b_norm'], lp['w_br_a'],
               lp['w_br_b'], lp['w_out'], lp['expand']])
    if init is not None:
        in_specs += [per_seq((A_HEADS, A_DV, A_DK)), per_seq((B_GROUPS, B_STATE, 256))]
        args += list(init)
    out_specs = [tok(D_MODEL)]
    out_shape = [jax.ShapeDtypeStruct((bsz, length, D_MODEL), F32)]
    aliases, n_alias = {}, 0
    if emit is not None:
        out_specs += [pl.BlockSpec((1, 1, 1, A_HEADS, A_DK, A_DV), lambda b, i: (b, layer, 1, 0, 0, 0)),
                      pl.BlockSpec((1, 1, 1, B_HEADS, B_HEADDIM, B_STATE), lambda b, i: (b, layer, 1, 0, 0, 0))]
        out_shape += _state_out_shapes(bsz)
        aliases = {len(args): 1, len(args) + 1: 2}
        in_specs += [pl.BlockSpec(memory_space=pl.ANY), pl.BlockSpec(memory_space=pl.ANY)]
        args += list(emit['acc'])
        n_alias = 2
    return pl.pallas_call(
        functools.partial(_bwd_kernel, has_init=init is not None, emit_state=emit is not None, n_alias=n_alias),
        grid=(bsz, nt), in_specs=in_specs, out_specs=out_specs, out_shape=out_shape,
        input_output_aliases=aliases, scratch_shapes=_state_scratch(),
        compiler_params=pltpu.CompilerParams(dimension_semantics=("arbitrary", "arbitrary"),
                                             vmem_limit_bytes=VMEM_LIMIT),
        name="mixer_bwd",
    )(*args)


def _ffn_kernel(x_ref, mod_ref, ln_ref, wu_ref, cw_ref, cb_ref, wd_ref, lnf_ref, o_ref, *, row_block, final_norm):
    x = x_ref[0]
    mod = mod_ref[0]
    shift, scale, gate = (mod[:, 3 * D_MODEL:4 * D_MODEL], mod[:, 4 * D_MODEL:5 * D_MODEL],
                          mod[:, 5 * D_MODEL:6 * D_MODEL])
    u = ((_rms(x) * ln_ref[...]) * (1.0 + scale) + shift).astype(BF16)

    def up(j):
        return (_mm(u, wu_ref[:, j * FF_BLOCK:(j + 1) * FF_BLOCK]),
                _mm(u, wu_ref[:, D_FF + j * FF_BLOCK:D_FF + (j + 1) * FF_BLOCK]))

    def conv(h, lo):
        cw = cw_ref[:, lo:lo + FF_BLOCK]
        prev, nxt = _row_neighbours(h, row_block)
        return cb_ref[:, lo:lo + FF_BLOCK] + prev * cw[0:1] + h * cw[1:2] + nxt * cw[2:3]

    steps = D_FF // FF_BLOCK
    acc = None
    acts = []
    h = up(0)
    for j in range(steps):
        h_next = up(j + 1) if j + 1 < steps else None
        acts.append((_silu(conv(h[0], j * FF_BLOCK)) * conv(h[1], D_FF + j * FF_BLOCK)).astype(BF16))
        if len(acts) == FF_DOWN_BLOCKS or j + 1 == steps:
            lo = (j + 1 - len(acts)) * FF_BLOCK
            part = _mm(acts[0] if len(acts) == 1 else jnp.concatenate(acts, axis=1),
                       wd_ref[lo:(j + 1) * FF_BLOCK, :])
            acc = part if acc is None else acc + part
            acts = []
        h = h_next
    y = x + gate * acc
    if final_norm:
        y = _rms(y) * lnf_ref[...]
    o_ref[0] = y


def _ffn_call(x, mod, lp, ln_f, row_block, mod_row0, mod_stride, final_norm):
    shape = x.shape
    if shape[1] % FF_TILE:
        assert mod_stride == 0 and FF_TILE % shape[1] == 0 and shape[1] % row_block == 0
        x = x.reshape(shape[0] * shape[1] // FF_TILE, FF_TILE, D_MODEL)
    bsz, length, _ = x.shape
    nt = length // FF_TILE
    tok = pl.BlockSpec((1, FF_TILE, D_MODEL), lambda b, i: (b, i, 0))
    return pl.pallas_call(
        functools.partial(_ffn_kernel, row_block=row_block, final_norm=final_norm),
        grid=(bsz, nt),
        in_specs=[tok, pl.BlockSpec((1, 1, MOD_COLS), lambda b, i: (mod_row0 + mod_stride * b, 0, 0)),
                  _const_spec((1, D_MODEL)), _const_spec((D_MODEL, 2 * D_FF)), _const_spec((3, 2 * D_FF)),
                  _const_spec((1, 2 * D_FF)), _const_spec((D_FF, D_MODEL)), _const_spec((1, D_MODEL))],
        out_specs=tok,
        out_shape=jax.ShapeDtypeStruct((bsz, length, D_MODEL), F32),
        compiler_params=pltpu.CompilerParams(dimension_semantics=("arbitrary", "arbitrary"),
                                             vmem_limit_bytes=VMEM_LIMIT),
        name="conv_ffn",
    )(x, mod, lp['ln2'], lp['w_ff_up'], lp['ff_conv_w'], lp['ff_conv_b'], lp['w_ff_down'], ln_f).reshape(shape)


def _pad_heads(a):
    return jnp.pad(a.astype(F32), [(0, 0)] * (a.ndim - 1) + [(0, LANES - a.shape[-1])])


def _pack_w_in(w):
    src_gates = C_XBC + B_CONV_CH + 2 * B_HEADS
    pad = jnp.zeros((w.shape[0], LANES - B_HEADS), w.dtype)
    return jnp.concatenate([w[:, :C_GATES], w[:, src_gates:], w[:, C_GATES:C_GATES + B_HEADS], pad,
                            w[:, C_GATES + B_HEADS:src_gates], pad], axis=1).astype(BF16)


def _hgrn_state_in(s):
    return jnp.swapaxes(s.astype(F32), -1, -2)


def _ssd_state_in(s):
    b = s.shape[0]
    s = s.astype(F32).reshape(b, B_GROUPS, B_HEADS // B_GROUPS, B_HEADDIM, B_STATE)
    return s.transpose(0, 1, 4, 2, 3).reshape(b, B_GROUPS, B_STATE, 256)


def kernel(x_prompt, x_sample, c, state_hgrn, state_ssd, c_ctx, w_ada, b_ada, ln1, ln2, ln_f, w_in, lb_logits,
           a_norm, conv_w, conv_b, dt_bias, a_log, d_skip, b_norm, w_br_a, w_br_b, w_out, w_ff_up, ff_conv_w,
           ff_conv_b, w_ff_down):
    depth = w_in.shape[0]
    dec_b = x_sample.shape[0]
    assert depth == DEPTH and dec_b + 1 <= MOD_ROWS
    assert x_prompt.shape[1] == TILE and x_sample.shape[1] % TILE == 0 and TILE % GRID_W == 0

    cvec = jnp.concatenate([c_ctx[None].astype(F32), c.astype(F32),
                            jnp.zeros((MOD_ROWS - 1 - dec_b, D_MODEL), F32)], axis=0)
    mod_all = _modulation(cvec, w_ada.astype(F32), b_ada.astype(F32))

    expand = (jnp.arange(2 * LANES)[:, None] % LANES == jnp.arange(B_WIDTH)[None, :] // B_HEADDIM).astype(BF16)
    layers = []
    for l in range(depth):
        layers.append(dict(
            ln1=ln1[l][None].astype(F32), ln2=ln2[l][None].astype(F32),
            w_in=_pack_w_in(w_in[l]),
            lbl=(lb_logits[:, 0, :].astype(F32), lb_logits[:, 1, :].astype(F32)),
            a_norm=a_norm[l][None].astype(F32), b_norm=b_norm[l][None].astype(F32),
            conv_w=conv_w[l].astype(F32), conv_b=conv_b[l][None].astype(F32),
            dt_bias=(_pad_heads(dt_bias[l, 0])[None], _pad_heads(dt_bias[l, 1])[None]),
            a_log=(_pad_heads(a_log[l, 0])[None], _pad_heads(a_log[l, 1])[None]),
            d_skip=jnp.repeat(d_skip[l].astype(F32), B_HEADDIM)[None], expand=expand,
            w_br_a=w_br_a[l].astype(BF16), w_br_b=w_br_b[l].astype(BF16), w_out=w_out[l].astype(BF16),
            w_ff_up=w_ff_up[l].astype(BF16), ff_conv_w=ff_conv_w[l].astype(F32),
            ff_conv_b=ff_conv_b[l][None].astype(F32), w_ff_down=w_ff_down[l].astype(BF16)))
    ln_f2 = ln_f[None].astype(F32)

    def run(x, row_block, mod_row0, mod_stride, states, emit_state):
        acc = None
        for l in range(depth):
            lp = layers[l]
            mod = mod_all[l].reshape(MOD_ROWS, 1, MOD_COLS)
            init_f = init_b = None
            if states is not None:
                init_f = (_hgrn_state_in(states[0][:, l, 0]), _ssd_state_in(states[1][:, l, 0]))
                init_b = (_hgrn_state_in(states[0][:, l, 1]), _ssd_state_in(states[1][:, l, 1]))
            fo = _fwd_call(x, mod, lp, l, row_block, mod_row0, mod_stride, init_f,
                           dict(acc=acc) if emit_state else None)
            if emit_state:
                acc = tuple(fo[len(FWD_OUT):])
            bo = _bwd_call(x, mod, fo[:len(FWD_OUT)], lp, l, mod_row0, mod_stride, init_b,
                           dict(acc=acc) if emit_state else None)
            if emit_state:
                acc = (bo[1], bo[2])
            x = _ffn_call(bo[0], mod, lp, ln_f2, row_block, mod_row0, mod_stride, l == depth - 1)
        return x, acc

    y_prompt, (new_hgrn, new_ssd) = run(x_prompt.astype(F32), x_prompt.shape[1], 0, 0, None, True)
    y_sample, _ = run(x_sample.astype(F32), GRID_W, 1, 1, (state_hgrn, state_ssd), False)
    return (y_prompt.astype(x_prompt.dtype), y_sample.astype(x_sample.dtype),
            new_hgrn.astype(x_prompt.dtype), new_ssd.astype(x_prompt.dtype))
```

```python
import functools

import jax
import jax.numpy as jnp
from jax import lax
from jax.experimental import pallas as pl
from jax.experimental.pallas import tpu as pltpu

F32 = jnp.float32
BF16 = jnp.bfloat16

D_MODEL = 1024
DEPTH = 4
GRID_W = 64
A_HEADS = 4
A_DK = 128
A_DV = 128
A_KW = A_HEADS * A_DK
A_WIDTH = A_HEADS * A_DV
B_HEADS = 16
B_HEADDIM = 64
B_WIDTH = B_HEADS * B_HEADDIM
B_GROUPS = 4
B_STATE = 64
B_GN = B_GROUPS * B_STATE
B_CONV_CH = B_WIDTH + 2 * B_GN
D_FF = 2816
N_MOD = 6
EPS = 1e-6
LOG2E = 1.4426950408889634

LANES = 128
SUBLANES = 8
TILE = 256
MOD_ROWS = 8
MOD_COLS = N_MOD * D_MODEL
MOD_BLOCK = 1536
FF_TILE = 512
FF_BLOCK = 256
FF_DOWN_BLOCKS = 2
FILL_COLS = 512
VMEM_LIMIT = 60 * 1024 * 1024

C_Q = 0
C_I = C_Q + A_KW
C_FFW = C_I + A_WIDTH
C_FBW = C_FFW + A_KW
C_GA = C_FBW + A_KW
C_Z = C_GA + A_WIDTH
C_XBC = C_Z + B_WIDTH
C_GATES = C_XBC + B_CONV_CH
C_DTFW = C_GATES + 2 * D_MODEL
C_DTBW = C_DTFW + LANES
IN_PACKED = C_DTBW + LANES

FWD_OUT = ((A_KW, BF16), (A_WIDTH, BF16), (A_KW, F32), (B_CONV_CH, BF16), (LANES, F32), (A_WIDTH, BF16),
           (B_WIDTH, BF16), (2 * D_MODEL, BF16), (A_WIDTH, F32), (B_WIDTH, F32), (A_KW, BF16))

NT_DIMS = (((1,), (1,)), ((), ()))
TN_DIMS = (((0,), (0,)), ((), ()))


def _mm(a, b):
    return jnp.dot(a, b, preferred_element_type=F32)


def _mm_nt(a, b):
    return lax.dot_general(a, b, NT_DIMS, preferred_element_type=F32)


def _mm_tn(a, b):
    return lax.dot_general(a, b, TN_DIMS, preferred_element_type=F32)


def _sigmoid(x):
    return 0.5 * jnp.tanh(0.5 * x) + 0.5


def _silu(x):
    return x * _sigmoid(x)


def _softplus(x):
    return jnp.maximum(x, 0.0) + jnp.log1p(jnp.exp(-jnp.abs(x)))


def _rms(x):
    return x * lax.rsqrt(jnp.mean(x * x, axis=-1, keepdims=True) + EPS)


def _iota(shape, dim):
    return lax.broadcasted_iota(jnp.int32, shape, dim)


def _zero_rows(a, period, offset):
    sub = _iota((SUBLANES, a.shape[1]), 0)
    groups = []
    for g in range(a.shape[0] // SUBLANES):
        blk = a[g * SUBLANES:(g + 1) * SUBLANES]
        if (g * SUBLANES) % period == offset - offset % SUBLANES:
            blk = jnp.where(sub == offset % SUBLANES, 0.0, blk)
        groups.append(blk)
    return jnp.concatenate(groups, axis=0)


def _row_neighbours(h, row_block):
    n = h.shape[0]
    return (_zero_rows(pltpu.roll(h, 1, axis=0), row_block, 0),
            _zero_rows(pltpu.roll(h, n - 1, axis=0), row_block, row_block - 1))


def _split_bf16(x, parts):
    out = []
    for i in range(parts):
        if i + 1 < parts:
            p = lax.bitcast_convert_type(lax.bitcast_convert_type(x, jnp.uint32) & jnp.uint32(0xFFFF0000), F32)
            out.append(p.astype(BF16))
            x = x - p
        else:
            out.append(x.astype(BF16))
    return out


def _scan_rows(g, reverse):
    n = g.shape[0]
    row = _iota((n, n), 0)
    col = _iota((n, n), 1)
    tri = jnp.where((col >= row) if reverse else (col <= row), 1.0, 0.0).astype(BF16)
    return _mm(jnp.concatenate([tri, tri, tri], axis=1), jnp.concatenate(_split_bf16(g, 3), axis=0))


def _level_ref(c, read_row, m, off):
    n, w = c.shape
    if 2 * m < 8:
        rk = _iota(c.shape, 0) & (2 * m - 1)
        out = c
        for k in range(2 * m):
            if k != off:
                out = jnp.where(rk == k, pltpu.roll(c, (k - off) % n, axis=0), out)
        return out
    pieces = [jnp.broadcast_to(read_row(j * 2 * m + off), (2 * m, w)) for j in range(n // (2 * m))]
    return pieces[0] if len(pieces) == 1 else jnp.concatenate(pieces, axis=0)


def _neg_abs(x):
    bits = lax.bitcast_convert_type(x, jnp.uint32) | jnp.uint32(0x80000000)
    return lax.bitcast_convert_type(bits, F32)


def _pair_level(n, reverse):
    t = _iota((n, n), 0)
    s = _iota((n, n), 1)
    x = t ^ s
    lvl = jnp.full((n, n), -1, jnp.int32)
    for l in range(n.bit_length() - 1):
        lvl = lvl + (x >= (1 << l)).astype(jnp.int32)
    return jnp.where((s > t) if reverse else (s < t), lvl, -1)


def _hgrn_gates(fx, lb):
    log_sig = jnp.minimum(fx, 0.0) - jnp.log(1.0 + jnp.exp(-jnp.abs(fx)))
    a = jnp.log(lb)
    b = jnp.log1p(-lb) + log_sig
    log_f = jnp.maximum(a, b) + jnp.log(1.0 + jnp.exp(-jnp.abs(a - b)))
    return log_f, (1.0 - lb) * _sigmoid(-fx)


def _lower_bound(logits, layer):
    e = jnp.exp(logits - jnp.max(logits, axis=0, keepdims=True))
    sm = e / jnp.sum(e, axis=0, keepdims=True)
    acc = jnp.zeros_like(sm[0:1])
    for j in range(1, layer + 1):
        acc = acc + sm[j:j + 1]
    return jnp.maximum(acc, 0.0)


def _interleave(gens):
    results = [None] * len(gens)
    live = list(range(len(gens)))
    while live:
        for idx in list(live):
            try:
                next(gens[idx])
            except StopIteration as stop:
                results[idx] = stop.value
                live.remove(idx)
    return results


def _hgrn_dir(q, v, log2_f, k, st_ref, c_ref, reverse):
    n = q.shape[0]
    hb = n // 2
    top = n.bit_length() - 2
    lvl = _pair_level(hb, reverse)
    ones = jnp.ones((LANES, LANES), BF16)
    c_all = _scan_rows(log2_f, reverse)
    zeros = jnp.zeros((hb, LANES), BF16)
    yield
    outs = []
    for pair in range(A_HEADS // 2):
        sl2 = slice(2 * pair * LANES, (2 * pair + 2) * LANES)
        c2 = c_all[:, sl2]
        c_ref[pair] = c2
        read_row = lambda r, pair=pair: c_ref[pair, pl.ds(r, 1), :]

        def scores(qt, kt):
            rhs = jnp.concatenate([jnp.concatenate([kt[:, :LANES], zeros], axis=1),
                                   jnp.concatenate([zeros, kt[:, LANES:]], axis=1)], axis=0)
            return _mm_nt(qt, rhs)

        s_lo = s_hi = None
        for l in range(top + 1):
            m = 1 << l
            w = jnp.exp2(_neg_abs(c2 - _level_ref(c2, read_row, m, m if reverse else m - 1))).astype(BF16)
            qt = q[:, sl2] * w
            kt = k[:, sl2] * w
            if l < top:
                a = scores(qt[:hb], kt[:hb])
                b = scores(qt[hb:], kt[hb:])
                keep = jnp.concatenate([lvl, lvl], axis=1) == l
                s_lo = jnp.where(keep, a, 0.0 if s_lo is None else s_lo)
                s_hi = jnp.where(keep, b, 0.0 if s_hi is None else s_hi)
            elif reverse:
                s_off = scores(qt[:hb], kt[hb:])
            else:
                s_off = scores(qt[hb:], kt[:hb])
            yield
        c_end2 = read_row(0 if reverse else n - 1)
        for e in range(2):
            h = 2 * pair + e
            sl = slice(h * LANES, (h + 1) * LANES)
            se = slice(e * LANES, (e + 1) * LANES)
            qh, vh, kh = q[:, sl], v[:, sl], k[:, sl]
            c, c_end = c2[:, se], c_end2[:, se]
            if reverse:
                o = jnp.concatenate([_mm(jnp.concatenate([s_lo[:, se], s_off[:, se]], axis=1).astype(BF16), vh),
                                     _mm(s_hi[:, se].astype(BF16), vh[hb:])], axis=0)
            else:
                o = jnp.concatenate([_mm(s_lo[:, se].astype(BF16), vh[:hb]),
                                     _mm(jnp.concatenate([s_off[:, se], s_hi[:, se]], axis=1).astype(BF16), vh)],
                                    axis=0)
            st = st_ref[h]
            o = o + _mm(qh * kh, ones) * vh.astype(F32)
            o = o + _mm_nt(qh * jnp.exp2(c).astype(BF16), st.astype(BF16))
            st_ref[h] = st * jnp.exp2(c_end) + _mm_tn(vh, kh * jnp.exp2(c_end - c).astype(BF16))
            outs.append(o)
            yield
    return jnp.concatenate(outs, axis=1)


def _expand_heads(a):
    r = a.shape[0]
    lane = _iota((r, LANES), 1)
    slabs = []
    for j in range(B_HEADS // 2):
        lo = jnp.broadcast_to(a[:, 2 * j:2 * j + 1], (r, LANES))
        hi = jnp.broadcast_to(a[:, 2 * j + 1:2 * j + 2], (r, LANES))
        slabs.append(jnp.where(lane < B_HEADDIM, lo, hi))
    return jnp.concatenate(slabs, axis=1)


def _expand_heads_mxu(a, e2):
    a = jnp.where(_iota(a.shape, 1) < B_HEADS, a, 0.0)
    return _mm(jnp.concatenate(_split_bf16(a, 2), axis=1), e2)


def _ssd_dir(xs, bm, cm, dt_raw, dt_bias, a_log, e2, st_ref, reverse):
    n = xs.shape[0]
    dt = _softplus(dt_raw + dt_bias)
    ac = _scan_rows((-LOG2E) * jnp.exp(a_log) * dt, reverse)
    ac_t = ac.T
    dt_t = dt.T
    hb = n // 2
    row_s = _iota((hb, hb), 0)
    col_s = _iota((hb, hb), 1)
    tri = (col_s >= row_s) if reverse else (col_s <= row_s)
    lane_gn = _iota((n, B_GN), 1)
    lane = _iota((n, LANES), 1)
    lo, hi = slice(0, hb), slice(hb, n)
    yield
    slabs = []
    for g in range(B_GROUPS):
        cb = _mm_nt(jnp.where((lane_gn >> 6) == g, cm, jnp.zeros_like(cm)), bm)
        yield
        for jp in range(2):
            slab = xs[:, (2 * g + jp) * LANES:(2 * g + jp + 1) * LANES]
            zero = jnp.zeros_like(slab)
            slab_e = (jnp.where(lane < B_HEADDIM, slab, zero), jnp.where(lane < B_HEADDIM, zero, slab))
            w_lo, w_hi = [], []
            for e in range(2):
                h = 4 * g + 2 * jp + e
                col = ac[:, h:h + 1]
                row = ac_t[h:h + 1, :]
                dt_row = dt_t[h:h + 1, :]

                def quad(rs, cs, masked):
                    seg = col[rs] - row[:, cs]
                    if masked:
                        seg = jnp.where(tri, seg, -1e30)
                    return (jnp.exp2(seg) * dt_row[:, cs] * cb[rs, cs]).astype(BF16)

                if reverse:
                    w_lo += [quad(lo, lo, True), quad(lo, hi, False)]
                    w_hi += [quad(hi, hi, True)]
                else:
                    w_lo += [quad(lo, lo, True)]
                    w_hi += [quad(hi, lo, False), quad(hi, hi, True)]
                yield
            if reverse:
                x_lo = jnp.concatenate([slab_e[0], slab_e[1]], axis=0)
                x_hi = jnp.concatenate([slab_e[0][hi], slab_e[1][hi]], axis=0)
            else:
                x_lo = jnp.concatenate([slab_e[0][lo], slab_e[1][lo]], axis=0)
                x_hi = jnp.concatenate([slab_e[0], slab_e[1]], axis=0)
            slabs.append(jnp.concatenate([_mm(jnp.concatenate(w_lo, axis=1), x_lo),
                                          _mm(jnp.concatenate(w_hi, axis=1), x_hi)], axis=0))
    y = jnp.concatenate(slabs, axis=1)
    st = st_ref[...]
    y = y + _mm(cm, st.astype(BF16)) * _expand_heads_mxu(jnp.exp2(ac), e2)
    yield
    a_end = ac[0:1, :] if reverse else ac[n - 1:n, :]
    wgt = _expand_heads_mxu(jnp.exp2(a_end - ac) * dt, e2)
    upd = _mm_tn(bm, xs * wgt.astype(BF16))
    block = (_iota(st.shape, 0) >> 6) == (_iota(st.shape, 1) >> 8)
    st_ref[...] = jnp.where(block, st * _expand_heads(jnp.exp2(a_end)) + upd, 0.0)
    return y


def _load_states(sa_in, sb_in, sa_ref, sb_ref):
    if sa_in is None:
        sa_ref[...] = jnp.zeros_like(sa_ref)
        sb_ref[...] = jnp.zeros_like(sb_ref)
    else:
        sa_ref[...] = sa_in[0]
        sb_ref[...] = jnp.zeros_like(sb_ref)
        for g in range(B_GROUPS):
            sb_ref[g * B_STATE:(g + 1) * B_STATE, g * 256:(g + 1) * 256] = sb_in[0, g]


def _store_states(sa_out, sb_out, sa_ref, sb_ref):
    for h in range(A_HEADS):
        sa_out[0, 0, 0, h] = sa_ref[h].T
    per = B_HEADS // B_GROUPS
    for pair in range(B_GROUPS // 2):
        t = sb_ref[pair * 2 * B_STATE:(pair + 1) * 2 * B_STATE, pair * 2 * 256:(pair + 1) * 2 * 256].T
        for gl in range(2):
            for j in range(per):
                r0 = gl * 256 + j * B_HEADDIM
                sb_out[0, 0, 0, (2 * pair + gl) * per + j] = t[r0:r0 + B_HEADDIM, gl * B_STATE:(gl + 1) * B_STATE]


def _mod_kernel(c_ref, w_ref, b_ref, o_ref):
    o_ref[0] = jnp.dot(_silu(c_ref[...]), w_ref[0], preferred_element_type=F32,
                       precision=lax.Precision.HIGHEST) + b_ref[0]


def _modulation(cvec, w_ada, b_ada):
    depth = w_ada.shape[0]
    return pl.pallas_call(
        _mod_kernel,
        grid=(depth, MOD_COLS // MOD_BLOCK),
        in_specs=[
            pl.BlockSpec((MOD_ROWS, D_MODEL), lambda l, j: (0, 0)),
            pl.BlockSpec((1, D_MODEL, MOD_BLOCK), lambda l, j: (l, 0, j)),
            pl.BlockSpec((1, 1, MOD_BLOCK), lambda l, j: (l, 0, j)),
        ],
        out_specs=pl.BlockSpec((1, MOD_ROWS, MOD_BLOCK), lambda l, j: (l, 0, j)),
        out_shape=jax.ShapeDtypeStruct((depth, MOD_ROWS, MOD_COLS), F32),
        compiler_params=pltpu.CompilerParams(dimension_semantics=("arbitrary", "arbitrary"),
                                             vmem_limit_bytes=VMEM_LIMIT),
        name="adaln_mod",
    )(cvec, w_ada, b_ada.reshape(depth, 1, MOD_COLS))


def _const_spec(shape):
    nd = len(shape)
    return pl.BlockSpec(shape, lambda *_: (0,) * nd, pipeline_mode=pl.Buffered(1))


def _layer_spec(shape, layer):
    nd = len(shape)
    return pl.BlockSpec((1,) + shape, lambda *_: (layer,) + (0,) * nd, pipeline_mode=pl.Buffered(1))


def _tile_kernel_params():
    return pltpu.CompilerParams(dimension_semantics=("arbitrary", "arbitrary"), vmem_limit_bytes=VMEM_LIMIT)


def _state_scratch():
    return [pltpu.VMEM((A_HEADS, A_DV, A_DK), F32), pltpu.VMEM((B_GN, B_WIDTH), F32),
            pltpu.VMEM((A_HEADS // 2, TILE, 2 * LANES), F32)]


def _state_out_shapes(bsz):
    return [jax.ShapeDtypeStruct((bsz, DEPTH, 2, A_HEADS, A_DK, A_DV), F32),
            jax.ShapeDtypeStruct((bsz, DEPTH, 2, B_HEADS, B_HEADDIM, B_STATE), F32)]


def _fwd_kernel(*refs, layer, row_block, has_init, emit_state, n_alias):
    (x_ref, mod_ref, ln_ref, w_ref, lbl_ref, lblb_ref, cw_ref, cb_ref, dtb_ref, alog_ref, dskip_ref,
     e2_ref), refs = refs[:12], refs[12:]
    if has_init:
        (sa_in, sb_in), refs = refs[:2], refs[2:]
    else:
        sa_in = sb_in = None
    refs = refs[n_alias:]
    (q_out, v_out, l2fbw_out, xbc_out, dtbw_out, ga_out, z_out, gates_out, oa_out, yb_out,
     kbw_out), refs = refs[:11], refs[11:]
    if emit_state:
        (sa_out, sb_out), refs = refs[:2], refs[2:]
    sa_ref, sb_ref, c_ref = refs

    i = pl.program_id(1)

    @pl.when(i == 0)
    def _():
        _load_states(sa_in, sb_in, sa_ref, sb_ref)

    x = x_ref[0]
    mod = mod_ref[0]
    shift, scale = mod[:, 0:D_MODEL], mod[:, D_MODEL:2 * D_MODEL]
    u = ((_rms(x) * ln_ref[...]) * (1.0 + scale) + shift).astype(BF16)

    def proj(lo, hi):
        return _mm(u, w_ref[0, :, lo:hi])

    def backward_gates(fx):
        log_f, k = _hgrn_gates(fx, _lower_bound(lblb_ref[...], layer))
        kbw_out[0] = k.astype(BF16)
        return log_f * LOG2E

    pending = [(ga_out, 0, C_GA, A_WIDTH, _silu), (l2fbw_out, 0, C_FBW, A_KW, backward_gates)]
    pending += [(z_out, o, C_Z + o, FILL_COLS, _silu) for o in range(0, B_WIDTH, FILL_COLS)]
    pending += [(gates_out, o, C_GATES + o, FILL_COLS, _sigmoid) for o in range(0, 2 * D_MODEL, FILL_COLS)]
    pending.append((dtbw_out, 0, C_DTBW, LANES, lambda raw: raw))

    def deferred():
        for ref, dst, src, width, fn in pending:
            raw = proj(src, src + width)
            yield
            ref[0, :, dst:dst + width] = fn(raw).astype(ref.dtype)
            yield

    fills = deferred()
    q = proj(C_Q, C_I)
    v = proj(C_I, C_FFW)
    fx = proj(C_FFW, C_FBW)
    xbc = proj(C_XBC, C_GATES)
    dt_raw = proj(C_DTFW, C_DTBW)
    next(fills)
    q = _silu(q).astype(BF16)
    v = v.astype(BF16)
    q_out[0] = q
    v_out[0] = v
    log_f, k = _hgrn_gates(fx, _lower_bound(lbl_ref[...], layer))
    next(fills), next(fills), next(fills)

    prev, nxt = _row_neighbours(xbc, row_block)
    cw = cw_ref[...]
    xbc = _silu(cb_ref[...] + prev * cw[0:1] + xbc * cw[1:2] + nxt * cw[2:3])
    xbc16 = xbc.astype(BF16)
    xbc_out[0] = xbc16

    o_a, y, _ = _interleave([
        _hgrn_dir(q, v, log_f * LOG2E, k.astype(BF16), sa_ref, c_ref, False),
        _ssd_dir(xbc16[:, 0:B_WIDTH], xbc16[:, B_WIDTH:B_WIDTH + B_GN], xbc16[:, B_WIDTH + B_GN:B_CONV_CH],
                 dt_raw, dtb_ref[...], alog_ref[...], e2_ref[...], sb_ref, False),
        fills])
    oa_out[0] = o_a
    yb_out[0] = y + dskip_ref[...] * xbc[:, 0:B_WIDTH]

    if emit_state:
        @pl.when(i == pl.num_programs(1) - 1)
        def _():
            _store_states(sa_out, sb_out, sa_ref, sb_ref)


def _fwd_call(x, mod, lp, layer, row_block, mod_row0, mod_stride, init, emit):
    bsz, length, _ = x.shape
    nt = length // TILE

    def tok(width):
        return pl.BlockSpec((1, TILE, width), lambda b, i: (b, i, 0))

    def per_seq(shape):
        return pl.BlockSpec((1,) + shape, lambda b, i: (b,) + (0,) * len(shape))

    in_specs = [tok(D_MODEL), pl.BlockSpec((1, 1, MOD_COLS), lambda b, i: (mod_row0 + mod_stride * b, 0, 0)),
                _const_spec((1, D_MODEL)), _layer_spec((D_MODEL, IN_PACKED), layer), _const_spec((DEPTH, A_KW)),
                _const_spec((DEPTH, A_KW)), _const_spec((3, B_CONV_CH)), _const_spec((1, B_CONV_CH)), _const_spec((1, LANES)),
                _const_spec((1, LANES)), _const_spec((1, B_WIDTH)), _const_spec((2 * LANES, B_WIDTH))]
    args = [x, mod, lp['ln1'], lp['w_in'], lp['lbl'][0], lp['lbl'][1], lp['conv_w'], lp['conv_b'], lp['dt_bias'][0],
            lp['a_log'][0], lp['d_skip'], lp['expand']]
    if init is not None:
        in_specs += [per_seq((A_HEADS, A_DV, A_DK)), per_seq((B_GROUPS, B_STATE, 256))]
        args += list(init)
    out_specs = [tok(w) for w, _ in FWD_OUT]
    out_shape = [jax.ShapeDtypeStruct((bsz, length, w), dt) for w, dt in FWD_OUT]
    aliases, n_alias = {}, 0
    if emit is not None:
        out_specs += [pl.BlockSpec((1, 1, 1, A_HEADS, A_DK, A_DV), lambda b, i: (b, layer, 0, 0, 0, 0)),
                      pl.BlockSpec((1, 1, 1, B_HEADS, B_HEADDIM, B_STATE), lambda b, i: (b, layer, 0, 0, 0, 0))]
        out_shape += _state_out_shapes(bsz)
        if emit['acc'] is not None:
            aliases = {len(args): len(FWD_OUT), len(args) + 1: len(FWD_OUT) + 1}
            in_specs += [pl.BlockSpec(memory_space=pl.ANY), pl.BlockSpec(memory_space=pl.ANY)]
            args += list(emit['acc'])
            n_alias = 2
    return pl.pallas_call(
        functools.partial(_fwd_kernel, layer=layer, row_block=row_block, has_init=init is not None,
                          emit_state=emit is not None, n_alias=n_alias),
        grid=(bsz, nt), in_specs=in_specs, out_specs=out_specs, out_shape=out_shape,
        input_output_aliases=aliases, scratch_shapes=_state_scratch(),
        compiler_params=_tile_kernel_params(),
        name="mixer_fwd",
    )(*args)


def _bwd_kernel(*refs, has_init, emit_state, n_alias):
    (x_ref, mod_ref, q_ref, v_ref, l2fbw_ref, xbc_ref, dtbw_ref, ga_ref, z_ref, gates_ref, oa_ref, yb_ref, kbw_ref,
     dtb_ref, alog_ref, an_ref, bn_ref, wa_ref, wb_ref, wo_ref, e2_ref), refs = refs[:21], refs[21:]
    if has_init:
        (sa_in, sb_in), refs = refs[:2], refs[2:]
    else:
        sa_in = sb_in = None
    refs = refs[n_alias:]
    x_out, refs = refs[0], refs[1:]
    if emit_state:
        (sa_out, sb_out), refs = refs[:2], refs[2:]
    sa_ref, sb_ref, c_ref = refs

    i = pl.program_id(1)

    @pl.when(i == 0)
    def _():
        _load_states(sa_in, sb_in, sa_ref, sb_ref)

    def branch_a():
        o_a = yield from _hgrn_dir(q_ref[0], v_ref[0], l2fbw_ref[0], kbw_ref[0], sa_ref, c_ref, True)
        o_a = oa_ref[0] + o_a
        an = an_ref[...]
        o_a = jnp.concatenate(
            [_rms(o_a[:, h * A_DV:(h + 1) * A_DV]) * an[:, h * A_DV:(h + 1) * A_DV] for h in range(A_HEADS)],
            axis=1)
        o_a = o_a * ga_ref[0].astype(F32)
        yield
        return gates_ref[0, :, 0:D_MODEL].astype(F32) * _mm(o_a.astype(BF16), wa_ref[0])

    xbc = xbc_ref[0]
    merged_a, y_b = _interleave([
        branch_a(),
        _ssd_dir(xbc[:, 0:B_WIDTH], xbc[:, B_WIDTH:B_WIDTH + B_GN], xbc[:, B_WIDTH + B_GN:B_CONV_CH],
                 dtbw_ref[0], dtb_ref[...], alog_ref[...], e2_ref[...], sb_ref, True)])
    y_b = (yb_ref[0] + y_b) * z_ref[0].astype(F32)
    bn = bn_ref[...]
    gw = B_WIDTH // B_GROUPS
    y_b = jnp.concatenate(
        [_rms(y_b[:, g * gw:(g + 1) * gw]) * bn[:, g * gw:(g + 1) * gw] for g in range(B_GROUPS)], axis=1)

    merged = merged_a + gates_ref[0, :, D_MODEL:2 * D_MODEL].astype(F32) * _mm(y_b.astype(BF16), wb_ref[0])
    mix = _mm(merged.astype(BF16), wo_ref[0])
    gate1 = mod_ref[0][:, 2 * D_MODEL:3 * D_MODEL]
    x_out[0] = x_ref[0] + gate1 * mix

    if emit_state:
        @pl.when(i == pl.num_programs(1) - 1)
        def _():
            _store_states(sa_out, sb_out, sa_ref, sb_ref)


def _bwd_call(x, mod, fwd_outs, lp, layer, mod_row0, mod_stride, init, emit):
    bsz, length, _ = x.shape
    nt = length // TILE

    def tok(width):
        return pl.BlockSpec((1, TILE, width), lambda b, i: (b, nt - 1 - i, 0))

    def per_seq(shape):
        return pl.BlockSpec((1,) + shape, lambda b, i: (b,) + (0,) * len(shape))

    in_specs = ([tok(D_MODEL), pl.BlockSpec((1, 1, MOD_COLS), lambda b, i: (mod_row0 + mod_stride * b, 0, 0))]
                + [tok(w) for w, _ in FWD_OUT]
                + [_const_spec((1, LANES)), _const_spec((1, LANES)),
                   _const_spec((1, A_WIDTH)), _const_spec((1, B_WIDTH)), _layer_spec((A_WIDTH, D_MODEL), layer),
                   _layer_spec((B_WIDTH, D_MODEL), layer), _layer_spec((D_MODEL, D_MODEL), layer),
                   _const_spec((2 * LANES, B_WIDTH))])
    args = ([x, mod] + list(fwd_outs)
            + [lp['dt_bias'][1], lp['a_log'][1], lp['a_norm'], lp['b_norm'], lp['w_br_a'],
               lp['w_br_b'], lp['w_out'], lp['expand']])
    if init is not None:
        in_specs += [per_seq((A_HEADS, A_DV, A_DK)), per_seq((B_GROUPS, B_STATE, 256))]
        args += list(init)
    out_specs = [tok(D_MODEL)]
    out_shape = [jax.ShapeDtypeStruct((bsz, length, D_MODEL), F32)]
    aliases, n_alias = {}, 0
    if emit is not None:
        out_specs += [pl.BlockSpec((1, 1, 1, A_HEADS, A_DK, A_DV), lambda b, i: (b, layer, 1, 0, 0, 0)),
                      pl.BlockSpec((1, 1, 1, B_HEADS, B_HEADDIM, B_STATE), lambda b, i: (b, layer, 1, 0, 0, 0))]
        out_shape += _state_out_shapes(bsz)
        aliases = {len(args): 1, len(args) + 1: 2}
        in_specs += [pl.BlockSpec(memory_space=pl.ANY), pl.BlockSpec(memory_space=pl.ANY)]
        args += list(emit['acc'])
        n_alias = 2
    return pl.pallas_call(
        functools.partial(_bwd_kernel, has_init=init is not None, emit_state=emit is not None, n_alias=n_alias),
        grid=(bsz, nt), in_specs=in_specs, out_specs=out_specs, out_shape=out_shape,
        input_output_aliases=aliases, scratch_shapes=_state_scratch(),
        compiler_params=_tile_kernel_params(),
        name="mixer_bwd",
    )(*args)


def _ffn_kernel(x_ref, mod_ref, ln_ref, wu_ref, cw_ref, cb_ref, wd_ref, lnf_ref, o_ref, *, row_block, final_norm):
    x = x_ref[0]
    mod = mod_ref[0]
    shift, scale, gate = (mod[:, 3 * D_MODEL:4 * D_MODEL], mod[:, 4 * D_MODEL:5 * D_MODEL],
                          mod[:, 5 * D_MODEL:6 * D_MODEL])
    u = ((_rms(x) * ln_ref[...]) * (1.0 + scale) + shift).astype(BF16)

    def up(j):
        return (_mm(u, wu_ref[0, :, j * FF_BLOCK:(j + 1) * FF_BLOCK]),
                _mm(u, wu_ref[0, :, D_FF + j * FF_BLOCK:D_FF + (j + 1) * FF_BLOCK]))

    def conv(h, lo):
        cw = cw_ref[:, lo:lo + FF_BLOCK]
        prev, nxt = _row_neighbours(h, row_block)
        return cb_ref[:, lo:lo + FF_BLOCK] + prev * cw[0:1] + h * cw[1:2] + nxt * cw[2:3]

    steps = D_FF // FF_BLOCK
    acc = None
    acts = []
    h = up(0)
    for j in range(steps):
        h_next = up(j + 1) if j + 1 < steps else None
        acts.append((_silu(conv(h[0], j * FF_BLOCK)) * conv(h[1], D_FF + j * FF_BLOCK)).astype(BF16))
        if len(acts) == FF_DOWN_BLOCKS or j + 1 == steps:
            lo = (j + 1 - len(acts)) * FF_BLOCK
            part = _mm(acts[0] if len(acts) == 1 else jnp.concatenate(acts, axis=1),
                       wd_ref[0, lo:(j + 1) * FF_BLOCK, :])
            acc = part if acc is None else acc + part
            acts = []
        h = h_next
    y = x + gate * acc
    if final_norm:
        y = _rms(y) * lnf_ref[...]
    o_ref[0] = y


def _ffn_call(x, mod, lp, ln_f, layer, row_block, mod_row0, mod_stride, final_norm):
    shape = x.shape
    if shape[1] % FF_TILE:
        assert mod_stride == 0 and FF_TILE % shape[1] == 0 and shape[1] % row_block == 0
        x = x.reshape(shape[0] * shape[1] // FF_TILE, FF_TILE, D_MODEL)
    bsz, length, _ = x.shape
    nt = length // FF_TILE
    tok = pl.BlockSpec((1, FF_TILE, D_MODEL), lambda b, i: (b, i, 0))
    return pl.pallas_call(
        functools.partial(_ffn_kernel, row_block=row_block, final_norm=final_norm),
        grid=(bsz, nt),
        in_specs=[tok, pl.BlockSpec((1, 1, MOD_COLS), lambda b, i: (mod_row0 + mod_stride * b, 0, 0)),
                  _const_spec((1, D_MODEL)), _layer_spec((D_MODEL, 2 * D_FF), layer), _const_spec((3, 2 * D_FF)),
                  _const_spec((1, 2 * D_FF)), _layer_spec((D_FF, D_MODEL), layer), _const_spec((1, D_MODEL))],
        out_specs=tok,
        out_shape=jax.ShapeDtypeStruct((bsz, length, D_MODEL), F32),
        compiler_params=_tile_kernel_params(),
        name="conv_ffn",
    )(x, mod, lp['ln2'], lp['w_ff_up'], lp['ff_conv_w'], lp['ff_conv_b'], lp['w_ff_down'], ln_f).reshape(shape)


def _pad_heads(a):
    return jnp.pad(a.astype(F32), [(0, 0)] * (a.ndim - 1) + [(0, LANES - a.shape[-1])])


def _pack_w_in(w):
    src_gates = C_XBC + B_CONV_CH + 2 * B_HEADS
    pad = jnp.zeros(w.shape[:-1] + (LANES - B_HEADS,), w.dtype)
    return jnp.concatenate([w[..., :C_GATES], w[..., src_gates:], w[..., C_GATES:C_GATES + B_HEADS], pad,
                            w[..., C_GATES + B_HEADS:src_gates], pad], axis=-1).astype(BF16)


def _hgrn_state_in(s):
    return jnp.swapaxes(s.astype(F32), -1, -2)


def _ssd_state_in(s):
    b = s.shape[0]
    s = s.astype(F32).reshape(b, B_GROUPS, B_HEADS // B_GROUPS, B_HEADDIM, B_STATE)
    return s.transpose(0, 1, 4, 2, 3).reshape(b, B_GROUPS, B_STATE, 256)


def kernel(x_prompt, x_sample, c, state_hgrn, state_ssd, c_ctx, w_ada, b_ada, ln1, ln2, ln_f, w_in, lb_logits,
           a_norm, conv_w, conv_b, dt_bias, a_log, d_skip, b_norm, w_br_a, w_br_b, w_out, w_ff_up, ff_conv_w,
           ff_conv_b, w_ff_down):
    depth = w_in.shape[0]
    dec_b = x_sample.shape[0]
    assert depth == DEPTH and dec_b + 1 <= MOD_ROWS
    assert x_prompt.shape[1] == TILE and x_sample.shape[1] % TILE == 0 and TILE % GRID_W == 0

    cvec = jnp.concatenate([c_ctx[None].astype(F32), c.astype(F32),
                            jnp.zeros((MOD_ROWS - 1 - dec_b, D_MODEL), F32)], axis=0)
    mod_all = _modulation(cvec, w_ada.astype(F32), b_ada.astype(F32))

    expand = (jnp.arange(2 * LANES)[:, None] % LANES == jnp.arange(B_WIDTH)[None, :] // B_HEADDIM).astype(BF16)
    stacked = dict(w_in=_pack_w_in(w_in), w_br_a=w_br_a.astype(BF16), w_br_b=w_br_b.astype(BF16),
                   w_out=w_out.astype(BF16), w_ff_up=w_ff_up.astype(BF16), w_ff_down=w_ff_down.astype(BF16))
    layers = []
    for l in range(depth):
        layers.append(dict(
            stacked, ln1=ln1[l][None].astype(F32), ln2=ln2[l][None].astype(F32),
            lbl=(lb_logits[:, 0, :].astype(F32), lb_logits[:, 1, :].astype(F32)),
            a_norm=a_norm[l][None].astype(F32), b_norm=b_norm[l][None].astype(F32),
            conv_w=conv_w[l].astype(F32), conv_b=conv_b[l][None].astype(F32),
            dt_bias=(_pad_heads(dt_bias[l, 0])[None], _pad_heads(dt_bias[l, 1])[None]),
            a_log=(_pad_heads(a_log[l, 0])[None], _pad_heads(a_log[l, 1])[None]),
            d_skip=jnp.repeat(d_skip[l].astype(F32), B_HEADDIM)[None], expand=expand,
            ff_conv_w=ff_conv_w[l].astype(F32), ff_conv_b=ff_conv_b[l][None].astype(F32)))
    ln_f2 = ln_f[None].astype(F32)

    def run(x, row_block, mod_row0, mod_stride, states, emit_state):
        acc = None
        for l in range(depth):
            lp = layers[l]
            mod = mod_all[l].reshape(MOD_ROWS, 1, MOD_COLS)
            init_f = init_b = None
            if states is not None:
                init_f = (_hgrn_state_in(states[0][:, l, 0]), _ssd_state_in(states[1][:, l, 0]))
                init_b = (_hgrn_state_in(states[0][:, l, 1]), _ssd_state_in(states[1][:, l, 1]))
            fo = _fwd_call(x, mod, lp, l, row_block, mod_row0, mod_stride, init_f,
                           dict(acc=acc) if emit_state else None)
            if emit_state:
                acc = tuple(fo[len(FWD_OUT):])
            bo = _bwd_call(x, mod, fo[:len(FWD_OUT)], lp, l, mod_row0, mod_stride, init_b,
                           dict(acc=acc) if emit_state else None)
            if emit_state:
                acc = (bo[1], bo[2])
            x = _ffn_call(bo[0], mod, lp, ln_f2, l, row_block, mod_row0, mod_stride, l == depth - 1)
        return x, acc

    y_prompt, (new_hgrn, new_ssd) = run(x_prompt.astype(F32), x_prompt.shape[1], 0, 0, None, True)
    y_sample, _ = run(x_sample.astype(F32), GRID_W, 1, 1, (state_hgrn, state_ssd), False)
    return (y_prompt.astype(x_prompt.dtype), y_sample.astype(x_sample.dtype),
            new_hgrn.astype(x_prompt.dtype), new_ssd.astype(x_prompt.dtype))
```

```python
import functools

import jax
import jax.numpy as jnp
from jax import lax
from jax.experimental import pallas as pl
from jax.experimental.pallas import tpu as pltpu

F32 = jnp.float32
BF16 = jnp.bfloat16

D_MODEL = 1024
DEPTH = 4
GRID_W = 64
A_HEADS = 4
A_DK = 128
A_DV = 128
A_KW = A_HEADS * A_DK
A_WIDTH = A_HEADS * A_DV
B_HEADS = 16
B_HEADDIM = 64
B_WIDTH = B_HEADS * B_HEADDIM
B_GROUPS = 4
B_STATE = 64
B_GN = B_GROUPS * B_STATE
B_CONV_CH = B_WIDTH + 2 * B_GN
B_GW = B_WIDTH // B_GROUPS
D_FF = 2816
N_MOD = 6
EPS = 1e-6
LOG2E = 1.4426950408889634

LANES = 128
SUBLANES = 8
TILE = 256
MOD_ROWS = 8
MOD_COLS = N_MOD * D_MODEL
MOD_BLOCK = 1536
FF_TILE = 512
FF_BLOCK = 256
FF_DOWN_BLOCKS = 2
FILL_COLS = 512
VMEM_LIMIT = 60 * 1024 * 1024

C_Q = 0
C_I = C_Q + A_KW
C_FFW = C_I + A_WIDTH
C_FBW = C_FFW + A_KW
C_GA = C_FBW + A_KW
C_Z = C_GA + A_WIDTH
C_XBC = C_Z + B_WIDTH
C_GATES = C_XBC + B_CONV_CH
C_DTFW = C_GATES + 2 * D_MODEL
C_DTBW = C_DTFW + LANES
IN_PACKED = C_DTBW + LANES

FWD_OUT = ((A_KW, BF16), (A_WIDTH, BF16), (A_KW, F32), (B_CONV_CH, BF16), (LANES, F32), (A_WIDTH, BF16),
           (B_WIDTH, BF16), (2 * D_MODEL, BF16), (A_WIDTH, F32), (B_WIDTH, F32), (A_KW, BF16))

NT_DIMS = (((1,), (1,)), ((), ()))
TN_DIMS = (((0,), (0,)), ((), ()))


def _mm(a, b):
    return jnp.dot(a, b, preferred_element_type=F32)


def _mm_nt(a, b):
    return lax.dot_general(a, b, NT_DIMS, preferred_element_type=F32)


def _mm_tn(a, b):
    return lax.dot_general(a, b, TN_DIMS, preferred_element_type=F32)


def _sigmoid(x):
    return 0.5 * jnp.tanh(0.5 * x) + 0.5


def _silu(x):
    return x * _sigmoid(x)


def _softplus(x):
    return jnp.maximum(x, 0.0) + jnp.log1p(jnp.exp(-jnp.abs(x)))


def _rms(x):
    return x * lax.rsqrt(jnp.mean(x * x, axis=-1, keepdims=True) + EPS)


def _iota(shape, dim):
    return lax.broadcasted_iota(jnp.int32, shape, dim)


def _zero_rows(a, period, offset):
    sub = _iota((SUBLANES, a.shape[1]), 0)
    groups = []
    for g in range(a.shape[0] // SUBLANES):
        blk = a[g * SUBLANES:(g + 1) * SUBLANES]
        if (g * SUBLANES) % period == offset - offset % SUBLANES:
            blk = jnp.where(sub == offset % SUBLANES, 0.0, blk)
        groups.append(blk)
    return jnp.concatenate(groups, axis=0)


def _row_neighbours(h, row_block):
    n = h.shape[0]
    return (_zero_rows(pltpu.roll(h, 1, axis=0), row_block, 0),
            _zero_rows(pltpu.roll(h, n - 1, axis=0), row_block, row_block - 1))


def _split_bf16(x, parts):
    out = []
    for i in range(parts):
        if i + 1 < parts:
            p = lax.bitcast_convert_type(lax.bitcast_convert_type(x, jnp.uint32) & jnp.uint32(0xFFFF0000), F32)
            out.append(p.astype(BF16))
            x = x - p
        else:
            out.append(x.astype(BF16))
    return out


def _scan_rows(g, reverse):
    n = g.shape[0]
    row = _iota((n, n), 0)
    col = _iota((n, n), 1)
    tri = jnp.where((col >= row) if reverse else (col <= row), 1.0, 0.0).astype(BF16)
    return _mm(jnp.concatenate([tri, tri, tri], axis=1), jnp.concatenate(_split_bf16(g, 3), axis=0))


def _level_ref(c, read_row, m, off):
    n, w = c.shape
    if 2 * m < 8:
        rk = _iota(c.shape, 0) & (2 * m - 1)
        out = c
        for k in range(2 * m):
            if k != off:
                out = jnp.where(rk == k, pltpu.roll(c, (k - off) % n, axis=0), out)
        return out
    pieces = [jnp.broadcast_to(read_row(j * 2 * m + off), (2 * m, w)) for j in range(n // (2 * m))]
    return pieces[0] if len(pieces) == 1 else jnp.concatenate(pieces, axis=0)


def _neg_abs(x):
    bits = lax.bitcast_convert_type(x, jnp.uint32) | jnp.uint32(0x80000000)
    return lax.bitcast_convert_type(bits, F32)


def _level_exponent(c, g, read_row, m, reverse):
    n = c.shape[0]
    if m > 2:
        return _neg_abs(c - _level_ref(c, read_row, m, m if reverse else m - 1))
    rk = _iota(c.shape, 0) & (2 * m - 1)
    if m == 1:
        return jnp.where(rk == (0 if reverse else 1), g, 0.0)
    up = pltpu.roll(g, n - 1, axis=0)
    down = pltpu.roll(g, 1, axis=0)
    if reverse:
        return jnp.where(rk == 0, g + up, jnp.where(rk == 1, g, jnp.where(rk == 2, 0.0, down)))
    return jnp.where(rk == 0, up, jnp.where(rk == 1, 0.0, jnp.where(rk == 2, g, g + down)))


def _pair_level(n, reverse):
    t = _iota((n, n), 0)
    s = _iota((n, n), 1)
    x = t ^ s
    lvl = jnp.full((n, n), -1, jnp.int32)
    for l in range(n.bit_length() - 1):
        lvl = lvl + (x >= (1 << l)).astype(jnp.int32)
    return jnp.where((s > t) if reverse else (s < t), lvl, -1)


def _hgrn_gates(fx, lb):
    log_sig = jnp.minimum(fx, 0.0) - jnp.log(1.0 + jnp.exp(-jnp.abs(fx)))
    a = jnp.log(lb)
    b = jnp.log1p(-lb) + log_sig
    log_f = jnp.maximum(a, b) + jnp.log(1.0 + jnp.exp(-jnp.abs(a - b)))
    return log_f, (1.0 - lb) * _sigmoid(-fx)


def _lower_bound(logits, layer):
    e = jnp.exp(logits - jnp.max(logits, axis=0, keepdims=True))
    sm = e / jnp.sum(e, axis=0, keepdims=True)
    acc = jnp.zeros_like(sm[0:1])
    for j in range(1, layer + 1):
        acc = acc + sm[j:j + 1]
    return jnp.maximum(acc, 0.0)


def _interleave(gens):
    results = [None] * len(gens)
    live = list(range(len(gens)))
    while live:
        for idx in list(live):
            try:
                next(gens[idx])
            except StopIteration as stop:
                results[idx] = stop.value
                live.remove(idx)
    return results


def _hgrn_dir(q, v, log2_f, k, st_ref, c_ref, reverse):
    n = q.shape[0]
    hb = n // 2
    top = n.bit_length() - 2
    lvl = _pair_level(hb, reverse)
    c_all = _scan_rows(log2_f, reverse)
    zeros = jnp.zeros((hb, LANES), BF16)
    yield
    outs = []
    for pair in range(A_HEADS // 2):
        sl2 = slice(2 * pair * LANES, (2 * pair + 2) * LANES)
        c2 = c_all[:, sl2]
        c_ref[pair] = c2
        read_row = lambda r, pair=pair: c_ref[pair, pl.ds(r, 1), :]

        def scores(qt, kt):
            rhs = jnp.concatenate([jnp.concatenate([kt[:, :LANES], zeros], axis=1),
                                   jnp.concatenate([zeros, kt[:, LANES:]], axis=1)], axis=0)
            return _mm_nt(qt, rhs)

        s_lo = s_hi = None
        for l in range(top + 1):
            m = 1 << l
            w = jnp.exp2(_level_exponent(c2, log2_f[:, sl2], read_row, m, reverse)).astype(BF16)
            qt = q[:, sl2] * w
            kt = k[:, sl2] * w
            if l < top:
                a = scores(qt[:hb], kt[:hb])
                b = scores(qt[hb:], kt[hb:])
                keep = jnp.concatenate([lvl, lvl], axis=1) == l
                s_lo = jnp.where(keep, a, 0.0 if s_lo is None else s_lo)
                s_hi = jnp.where(keep, b, 0.0 if s_hi is None else s_hi)
            elif reverse:
                s_off = scores(qt[:hb], kt[hb:])
            else:
                s_off = scores(qt[hb:], kt[:hb])
            yield
        c_end2 = read_row(0 if reverse else n - 1)
        for e in range(2):
            h = 2 * pair + e
            sl = slice(h * LANES, (h + 1) * LANES)
            se = slice(e * LANES, (e + 1) * LANES)
            qh, vh, kh = q[:, sl], v[:, sl], k[:, sl]
            c, c_end = c2[:, se], c_end2[:, se]
            if reverse:
                o = jnp.concatenate([_mm(jnp.concatenate([s_lo[:, se], s_off[:, se]], axis=1).astype(BF16), vh),
                                     _mm(s_hi[:, se].astype(BF16), vh[hb:])], axis=0)
            else:
                o = jnp.concatenate([_mm(s_lo[:, se].astype(BF16), vh[:hb]),
                                     _mm(jnp.concatenate([s_off[:, se], s_hi[:, se]], axis=1).astype(BF16), vh)],
                                    axis=0)
            st = st_ref[h]
            o = o + jnp.sum((qh * kh).astype(F32), axis=-1, keepdims=True) * vh.astype(F32)
            o = o + _mm_nt(qh * jnp.exp2(c).astype(BF16), st.astype(BF16))
            st_ref[h] = st * jnp.exp2(c_end) + _mm_tn(vh, kh * jnp.exp2(c_end - c).astype(BF16))
            outs.append(o)
            yield
    return jnp.concatenate(outs, axis=1)


def _expand_heads(a):
    r = a.shape[0]
    lane = _iota((r, LANES), 1)
    slabs = []
    for j in range(B_HEADS // 2):
        lo = jnp.broadcast_to(a[:, 2 * j:2 * j + 1], (r, LANES))
        hi = jnp.broadcast_to(a[:, 2 * j + 1:2 * j + 2], (r, LANES))
        slabs.append(jnp.where(lane < B_HEADDIM, lo, hi))
    return jnp.concatenate(slabs, axis=1)


def _expand_heads_mxu(a, e2):
    a = jnp.where(_iota(a.shape, 1) < B_HEADS, a, 0.0)
    return _mm(jnp.concatenate(_split_bf16(a, 2), axis=1), e2)


def _ssd_dir(xs, bm, cm, dt_raw, dt_bias, a_log, e2, st_ref, reverse):
    n = xs.shape[0]
    dt = _softplus(dt_raw + dt_bias)
    ac = _scan_rows((-LOG2E) * jnp.exp(a_log) * dt, reverse)
    ac_t = (ac - jnp.log2(dt)).T
    hb = n // 2
    row_s = _iota((hb, hb), 0)
    col_s = _iota((hb, hb), 1)
    tri = (col_s >= row_s) if reverse else (col_s <= row_s)
    lane_gn = _iota((n, B_GN), 1)
    lane = _iota((n, LANES), 1)
    lo, hi = slice(0, hb), slice(hb, n)
    yield
    slabs = []
    for g in range(B_GROUPS):
        cb = _mm_nt(jnp.where((lane_gn >> 6) == g, cm, jnp.zeros_like(cm)), bm)
        yield
        for jp in range(2):
            slab = xs[:, (2 * g + jp) * LANES:(2 * g + jp + 1) * LANES]
            zero = jnp.zeros_like(slab)
            slab_e = (jnp.where(lane < B_HEADDIM, slab, zero), jnp.where(lane < B_HEADDIM, zero, slab))
            w_lo, w_hi = [], []
            for e in range(2):
                h = 4 * g + 2 * jp + e
                col = ac[:, h:h + 1]
                row = ac_t[h:h + 1, :]

                def quad(rs, cs, masked):
                    seg = col[rs] - row[:, cs]
                    if masked:
                        seg = jnp.where(tri, seg, -1e30)
                    return (jnp.exp2(seg) * cb[rs, cs]).astype(BF16)

                if reverse:
                    w_lo += [quad(lo, lo, True), quad(lo, hi, False)]
                    w_hi += [quad(hi, hi, True)]
                else:
                    w_lo += [quad(lo, lo, True)]
                    w_hi += [quad(hi, lo, False), quad(hi, hi, True)]
                yield
            if reverse:
                x_lo = jnp.concatenate([slab_e[0], slab_e[1]], axis=0)
                x_hi = jnp.concatenate([slab_e[0][hi], slab_e[1][hi]], axis=0)
            else:
                x_lo = jnp.concatenate([slab_e[0][lo], slab_e[1][lo]], axis=0)
                x_hi = jnp.concatenate([slab_e[0], slab_e[1]], axis=0)
            slabs.append(jnp.concatenate([_mm(jnp.concatenate(w_lo, axis=1), x_lo),
                                          _mm(jnp.concatenate(w_hi, axis=1), x_hi)], axis=0))
    y = jnp.concatenate(slabs, axis=1)
    st = st_ref[...]
    y = y + _mm(cm, st.astype(BF16)) * _expand_heads_mxu(jnp.exp2(ac), e2)
    yield
    a_end = ac[0:1, :] if reverse else ac[n - 1:n, :]
    wgt = _expand_heads_mxu(jnp.exp2(a_end - ac) * dt, e2)
    upd = _mm_tn(bm, xs * wgt.astype(BF16))
    block = (_iota(st.shape, 0) >> 6) == (_iota(st.shape, 1) >> 8)
    st_ref[...] = jnp.where(block, st * _expand_heads(jnp.exp2(a_end)) + upd, 0.0)
    return y


def _load_states(sa_in, sb_in, sa_ref, sb_ref):
    if sa_in is None:
        sa_ref[...] = jnp.zeros_like(sa_ref)
        sb_ref[...] = jnp.zeros_like(sb_ref)
    else:
        sa_ref[...] = sa_in[0]
        sb_ref[...] = jnp.zeros_like(sb_ref)
        for g in range(B_GROUPS):
            sb_ref[g * B_STATE:(g + 1) * B_STATE, g * B_GW:(g + 1) * B_GW] = sb_in[0, g]


def _store_states(sa_out, sb_out, sa_ref, sb_ref):
    for h in range(A_HEADS):
        sa_out[0, 0, 0, h] = sa_ref[h].T
    per = B_HEADS // B_GROUPS
    for pair in range(B_GROUPS // 2):
        t = sb_ref[pair * 2 * B_STATE:(pair + 1) * 2 * B_STATE, pair * 2 * B_GW:(pair + 1) * 2 * B_GW].T
        for gl in range(2):
            for j in range(per):
                r0 = gl * B_GW + j * B_HEADDIM
                sb_out[0, 0, 0, (2 * pair + gl) * per + j] = t[r0:r0 + B_HEADDIM, gl * B_STATE:(gl + 1) * B_STATE]


def _mod_kernel(c_ref, w_ref, b_ref, o_ref):
    o_ref[0] = jnp.dot(_silu(c_ref[...]), w_ref[0], preferred_element_type=F32,
                       precision=lax.Precision.HIGHEST) + b_ref[0]


def _modulation(cvec, w_ada, b_ada):
    depth = w_ada.shape[0]
    return pl.pallas_call(
        _mod_kernel,
        grid=(depth, MOD_COLS // MOD_BLOCK),
        in_specs=[
            pl.BlockSpec((MOD_ROWS, D_MODEL), lambda l, j: (0, 0)),
            pl.BlockSpec((1, D_MODEL, MOD_BLOCK), lambda l, j: (l, 0, j)),
            pl.BlockSpec((1, 1, MOD_BLOCK), lambda l, j: (l, 0, j)),
        ],
        out_specs=pl.BlockSpec((1, MOD_ROWS, MOD_BLOCK), lambda l, j: (l, 0, j)),
        out_shape=jax.ShapeDtypeStruct((depth, MOD_ROWS, MOD_COLS), F32),
        compiler_params=pltpu.CompilerParams(dimension_semantics=("arbitrary", "arbitrary"),
                                             vmem_limit_bytes=VMEM_LIMIT),
        name="adaln_mod",
    )(cvec, w_ada, b_ada.reshape(depth, 1, MOD_COLS))


def _const_spec(shape):
    nd = len(shape)
    return pl.BlockSpec(shape, lambda *_: (0,) * nd, pipeline_mode=pl.Buffered(1))


def _layer_spec(shape, layer):
    nd = len(shape)
    return pl.BlockSpec((1,) + shape, lambda *_: (layer,) + (0,) * nd, pipeline_mode=pl.Buffered(1))


def _tile_kernel_params():
    return pltpu.CompilerParams(dimension_semantics=("arbitrary", "arbitrary"), vmem_limit_bytes=VMEM_LIMIT)


def _state_scratch():
    return [pltpu.VMEM((A_HEADS, A_DV, A_DK), F32), pltpu.VMEM((B_GN, B_WIDTH), F32),
            pltpu.VMEM((A_HEADS // 2, TILE, 2 * LANES), F32)]


def _state_out_shapes(bsz):
    return [jax.ShapeDtypeStruct((bsz, DEPTH, 2, A_HEADS, A_DK, A_DV), F32),
            jax.ShapeDtypeStruct((bsz, DEPTH, 2, B_HEADS, B_HEADDIM, B_STATE), F32)]


def _fwd_kernel(*refs, layer, row_block, has_init, emit_state, n_alias):
    (x_ref, mod_ref, ln_ref, w_ref, lbl_ref, lblb_ref, cw_ref, cb_ref, dtb_ref, alog_ref, dskip_ref,
     e2_ref), refs = refs[:12], refs[12:]
    if has_init:
        (sa_in, sb_in), refs = refs[:2], refs[2:]
    else:
        sa_in = sb_in = None
    refs = refs[n_alias:]
    (q_out, v_out, l2fbw_out, xbc_out, dtbw_out, ga_out, z_out, gates_out, oa_out, yb_out,
     kbw_out), refs = refs[:11], refs[11:]
    if emit_state:
        (sa_out, sb_out), refs = refs[:2], refs[2:]
    sa_ref, sb_ref, c_ref = refs

    i = pl.program_id(1)

    @pl.when(i == 0)
    def _():
        _load_states(sa_in, sb_in, sa_ref, sb_ref)

    x = x_ref[0]
    mod = mod_ref[0]
    shift, scale = mod[:, 0:D_MODEL], mod[:, D_MODEL:2 * D_MODEL]
    u = ((_rms(x) * ln_ref[...]) * (1.0 + scale) + shift).astype(BF16)

    def proj(lo, hi):
        return _mm(u, w_ref[0, :, lo:hi])

    def backward_gates(fx):
        log_f, k = _hgrn_gates(fx, _lower_bound(lblb_ref[...], layer))
        kbw_out[0] = k.astype(BF16)
        return log_f * LOG2E

    pending = [(ga_out, 0, C_GA, A_WIDTH, _silu), (l2fbw_out, 0, C_FBW, A_KW, backward_gates)]
    pending += [(z_out, o, C_Z + o, FILL_COLS, _silu) for o in range(0, B_WIDTH, FILL_COLS)]
    pending += [(gates_out, o, C_GATES + o, FILL_COLS, _sigmoid) for o in range(0, 2 * D_MODEL, FILL_COLS)]
    pending.append((dtbw_out, 0, C_DTBW, LANES, lambda raw: raw))

    def deferred():
        for ref, dst, src, width, fn in pending:
            raw = proj(src, src + width)
            yield
            ref[0, :, dst:dst + width] = fn(raw).astype(ref.dtype)
            yield

    fills = deferred()
    q = proj(C_Q, C_I)
    v = proj(C_I, C_FFW)
    fx = proj(C_FFW, C_FBW)
    xbc = proj(C_XBC, C_GATES)
    dt_raw = proj(C_DTFW, C_DTBW)
    next(fills)
    q = _silu(q).astype(BF16)
    v = v.astype(BF16)
    q_out[0] = q
    v_out[0] = v
    log_f, k = _hgrn_gates(fx, _lower_bound(lbl_ref[...], layer))
    next(fills), next(fills), next(fills)

    prev, nxt = _row_neighbours(xbc, row_block)
    cw = cw_ref[...]
    xbc = _silu(cb_ref[...] + prev * cw[0:1] + xbc * cw[1:2] + nxt * cw[2:3])
    xbc16 = xbc.astype(BF16)
    xbc_out[0] = xbc16

    o_a, y, _ = _interleave([
        _hgrn_dir(q, v, log_f * LOG2E, k.astype(BF16), sa_ref, c_ref, False),
        _ssd_dir(xbc16[:, 0:B_WIDTH], xbc16[:, B_WIDTH:B_WIDTH + B_GN], xbc16[:, B_WIDTH + B_GN:B_CONV_CH],
                 dt_raw, dtb_ref[...], alog_ref[...], e2_ref[...], sb_ref, False),
        fills])
    oa_out[0] = o_a
    yb_out[0] = y + dskip_ref[...] * xbc[:, 0:B_WIDTH]

    if emit_state:
        @pl.when(i == pl.num_programs(1) - 1)
        def _():
            _store_states(sa_out, sb_out, sa_ref, sb_ref)


def _fwd_call(x, mod, lp, layer, row_block, mod_row0, mod_stride, init, emit):
    bsz, length, _ = x.shape
    nt = length // TILE

    def tok(width):
        return pl.BlockSpec((1, TILE, width), lambda b, i: (b, i, 0))

    def per_seq(shape):
        return pl.BlockSpec((1,) + shape, lambda b, i: (b,) + (0,) * len(shape))

    in_specs = [tok(D_MODEL), pl.BlockSpec((1, 1, MOD_COLS), lambda b, i: (mod_row0 + mod_stride * b, 0, 0)),
                _const_spec((1, D_MODEL)), _layer_spec((D_MODEL, IN_PACKED), layer), _const_spec((DEPTH, A_KW)),
                _const_spec((DEPTH, A_KW)), _const_spec((3, B_CONV_CH)), _const_spec((1, B_CONV_CH)), _const_spec((1, LANES)),
                _const_spec((1, LANES)), _const_spec((1, B_WIDTH)), _const_spec((2 * LANES, B_WIDTH))]
    args = [x, mod, lp['ln1'], lp['w_in'], lp['lbl'][0], lp['lbl'][1], lp['conv_w'], lp['conv_b'], lp['dt_bias'][0],
            lp['a_log'][0], lp['d_skip'], lp['expand']]
    if init is not None:
        in_specs += [per_seq((A_HEADS, A_DV, A_DK)), per_seq((B_GROUPS, B_STATE, B_GW))]
        args += list(init)
    out_specs = [tok(w) for w, _ in FWD_OUT]
    out_shape = [jax.ShapeDtypeStruct((bsz, length, w), dt) for w, dt in FWD_OUT]
    aliases, n_alias = {}, 0
    if emit is not None:
        out_specs += [pl.BlockSpec((1, 1, 1, A_HEADS, A_DK, A_DV), lambda b, i: (b, layer, 0, 0, 0, 0)),
                      pl.BlockSpec((1, 1, 1, B_HEADS, B_HEADDIM, B_STATE), lambda b, i: (b, layer, 0, 0, 0, 0))]
        out_shape += _state_out_shapes(bsz)
        if emit['acc'] is not None:
            aliases = {len(args): len(FWD_OUT), len(args) + 1: len(FWD_OUT) + 1}
            in_specs += [pl.BlockSpec(memory_space=pl.ANY), pl.BlockSpec(memory_space=pl.ANY)]
            args += list(emit['acc'])
            n_alias = 2
    return pl.pallas_call(
        functools.partial(_fwd_kernel, layer=layer, row_block=row_block, has_init=init is not None,
                          emit_state=emit is not None, n_alias=n_alias),
        grid=(bsz, nt), in_specs=in_specs, out_specs=out_specs, out_shape=out_shape,
        input_output_aliases=aliases, scratch_shapes=_state_scratch(),
        compiler_params=_tile_kernel_params(),
        name="mixer_fwd",
    )(*args)


def _bwd_kernel(*refs, has_init, emit_state, n_alias):
    (x_ref, mod_ref, q_ref, v_ref, l2fbw_ref, xbc_ref, dtbw_ref, ga_ref, z_ref, gates_ref, oa_ref, yb_ref, kbw_ref,
     dtb_ref, alog_ref, an_ref, bn_ref, wa_ref, wb_ref, wo_ref, e2_ref), refs = refs[:21], refs[21:]
    if has_init:
        (sa_in, sb_in), refs = refs[:2], refs[2:]
    else:
        sa_in = sb_in = None
    refs = refs[n_alias:]
    x_out, refs = refs[0], refs[1:]
    if emit_state:
        (sa_out, sb_out), refs = refs[:2], refs[2:]
    sa_ref, sb_ref, c_ref = refs

    i = pl.program_id(1)

    @pl.when(i == 0)
    def _():
        _load_states(sa_in, sb_in, sa_ref, sb_ref)

    def branch_a():
        o_a = yield from _hgrn_dir(q_ref[0], v_ref[0], l2fbw_ref[0], kbw_ref[0], sa_ref, c_ref, True)
        o_a = oa_ref[0] + o_a
        an = an_ref[...]
        o_a = jnp.concatenate(
            [_rms(o_a[:, h * A_DV:(h + 1) * A_DV]) * an[:, h * A_DV:(h + 1) * A_DV] for h in range(A_HEADS)],
            axis=1)
        o_a = o_a * ga_ref[0].astype(F32)
        yield
        return gates_ref[0, :, 0:D_MODEL].astype(F32) * _mm(o_a.astype(BF16), wa_ref[0])

    xbc = xbc_ref[0]
    merged_a, y_b = _interleave([
        branch_a(),
        _ssd_dir(xbc[:, 0:B_WIDTH], xbc[:, B_WIDTH:B_WIDTH + B_GN], xbc[:, B_WIDTH + B_GN:B_CONV_CH],
                 dtbw_ref[0], dtb_ref[...], alog_ref[...], e2_ref[...], sb_ref, True)])
    y_b = (yb_ref[0] + y_b) * z_ref[0].astype(F32)
    bn = bn_ref[...]
    gw = B_WIDTH // B_GROUPS
    y_b = jnp.concatenate(
        [_rms(y_b[:, g * gw:(g + 1) * gw]) * bn[:, g * gw:(g + 1) * gw] for g in range(B_GROUPS)], axis=1)

    merged = merged_a + gates_ref[0, :, D_MODEL:2 * D_MODEL].astype(F32) * _mm(y_b.astype(BF16), wb_ref[0])
    mix = _mm(merged.astype(BF16), wo_ref[0])
    gate1 = mod_ref[0][:, 2 * D_MODEL:3 * D_MODEL]
    x_out[0] = x_ref[0] + gate1 * mix

    if emit_state:
        @pl.when(i == pl.num_programs(1) - 1)
        def _():
            _store_states(sa_out, sb_out, sa_ref, sb_ref)


def _bwd_call(x, mod, fwd_outs, lp, layer, mod_row0, mod_stride, init, emit):
    bsz, length, _ = x.shape
    nt = length // TILE

    def tok(width):
        return pl.BlockSpec((1, TILE, width), lambda b, i: (b, nt - 1 - i, 0))

    def per_seq(shape):
        return pl.BlockSpec((1,) + shape, lambda b, i: (b,) + (0,) * len(shape))

    in_specs = ([tok(D_MODEL), pl.BlockSpec((1, 1, MOD_COLS), lambda b, i: (mod_row0 + mod_stride * b, 0, 0))]
                + [tok(w) for w, _ in FWD_OUT]
                + [_const_spec((1, LANES)), _const_spec((1, LANES)),
                   _const_spec((1, A_WIDTH)), _const_spec((1, B_WIDTH)), _layer_spec((A_WIDTH, D_MODEL), layer),
                   _layer_spec((B_WIDTH, D_MODEL), layer), _layer_spec((D_MODEL, D_MODEL), layer),
                   _const_spec((2 * LANES, B_WIDTH))])
    args = ([x, mod] + list(fwd_outs)
            + [lp['dt_bias'][1], lp['a_log'][1], lp['a_norm'], lp['b_norm'], lp['w_br_a'],
               lp['w_br_b'], lp['w_out'], lp['expand']])
    if init is not None:
        in_specs += [per_seq((A_HEADS, A_DV, A_DK)), per_seq((B_GROUPS, B_STATE, B_GW))]
        args += list(init)
    out_specs = [tok(D_MODEL)]
    out_shape = [jax.ShapeDtypeStruct((bsz, length, D_MODEL), F32)]
    aliases, n_alias = {}, 0
    if emit is not None:
        out_specs += [pl.BlockSpec((1, 1, 1, A_HEADS, A_DK, A_DV), lambda b, i: (b, layer, 1, 0, 0, 0)),
                      pl.BlockSpec((1, 1, 1, B_HEADS, B_HEADDIM, B_STATE), lambda b, i: (b, layer, 1, 0, 0, 0))]
        out_shape += _state_out_shapes(bsz)
        aliases = {len(args): 1, len(args) + 1: 2}
        in_specs += [pl.BlockSpec(memory_space=pl.ANY), pl.BlockSpec(memory_space=pl.ANY)]
        args += list(emit['acc'])
        n_alias = 2
    return pl.pallas_call(
        functools.partial(_bwd_kernel, has_init=init is not None, emit_state=emit is not None, n_alias=n_alias),
        grid=(bsz, nt), in_specs=in_specs, out_specs=out_specs, out_shape=out_shape,
        input_output_aliases=aliases, scratch_shapes=_state_scratch(),
        compiler_params=_tile_kernel_params(),
        name="mixer_bwd",
    )(*args)


def _ffn_kernel(x_ref, mod_ref, ln_ref, wu_ref, cw_ref, cb_ref, wd_ref, lnf_ref, o_ref, *, row_block, final_norm):
    x = x_ref[0]
    mod = mod_ref[0]
    shift, scale, gate = (mod[:, 3 * D_MODEL:4 * D_MODEL], mod[:, 4 * D_MODEL:5 * D_MODEL],
                          mod[:, 5 * D_MODEL:6 * D_MODEL])
    u = ((_rms(x) * ln_ref[...]) * (1.0 + scale) + shift).astype(BF16)

    def up(j):
        return (_mm(u, wu_ref[0, :, j * FF_BLOCK:(j + 1) * FF_BLOCK]),
                _mm(u, wu_ref[0, :, D_FF + j * FF_BLOCK:D_FF + (j + 1) * FF_BLOCK]))

    def conv(h, lo):
        cw = cw_ref[:, lo:lo + FF_BLOCK]
        prev, nxt = _row_neighbours(h, row_block)
        return cb_ref[:, lo:lo + FF_BLOCK] + prev * cw[0:1] + h * cw[1:2] + nxt * cw[2:3]

    steps = D_FF // FF_BLOCK
    acc = None
    acts = []
    h = up(0)
    for j in range(steps):
        h_next = up(j + 1) if j + 1 < steps else None
        acts.append((_silu(conv(h[0], j * FF_BLOCK)) * conv(h[1], D_FF + j * FF_BLOCK)).astype(BF16))
        if len(acts) == FF_DOWN_BLOCKS or j + 1 == steps:
            lo = (j + 1 - len(acts)) * FF_BLOCK
            part = _mm(acts[0] if len(acts) == 1 else jnp.concatenate(acts, axis=1),
                       wd_ref[0, lo:(j + 1) * FF_BLOCK, :])
            acc = part if acc is None else acc + part
            acts = []
        h = h_next
    y = x + gate * acc
    if final_norm:
        y = _rms(y) * lnf_ref[...]
    o_ref[0] = y


def _ffn_call(x, mod, lp, ln_f, layer, row_block, mod_row0, mod_stride, final_norm):
    shape = x.shape
    if shape[1] % FF_TILE:
        assert mod_stride == 0 and FF_TILE % shape[1] == 0 and shape[1] % row_block == 0
        x = x.reshape(shape[0] * shape[1] // FF_TILE, FF_TILE, D_MODEL)
    bsz, length, _ = x.shape
    nt = length // FF_TILE
    tok = pl.BlockSpec((1, FF_TILE, D_MODEL), lambda b, i: (b, i, 0))
    return pl.pallas_call(
        functools.partial(_ffn_kernel, row_block=row_block, final_norm=final_norm),
        grid=(bsz, nt),
        in_specs=[tok, pl.BlockSpec((1, 1, MOD_COLS), lambda b, i: (mod_row0 + mod_stride * b, 0, 0)),
                  _const_spec((1, D_MODEL)), _layer_spec((D_MODEL, 2 * D_FF), layer), _const_spec((3, 2 * D_FF)),
                  _const_spec((1, 2 * D_FF)), _layer_spec((D_FF, D_MODEL), layer), _const_spec((1, D_MODEL))],
        out_specs=tok,
        out_shape=jax.ShapeDtypeStruct((bsz, length, D_MODEL), F32),
        compiler_params=_tile_kernel_params(),
        name="conv_ffn",
    )(x, mod, lp['ln2'], lp['w_ff_up'], lp['ff_conv_w'], lp['ff_conv_b'], lp['w_ff_down'], ln_f).reshape(shape)


def _pad_heads(a):
    return jnp.pad(a.astype(F32), [(0, 0)] * (a.ndim - 1) + [(0, LANES - a.shape[-1])])


def _pack_w_in(w):
    src_gates = C_XBC + B_CONV_CH + 2 * B_HEADS
    pad = jnp.zeros(w.shape[:-1] + (LANES - B_HEADS,), w.dtype)
    return jnp.concatenate([w[..., :C_GATES], w[..., src_gates:], w[..., C_GATES:C_GATES + B_HEADS], pad,
                            w[..., C_GATES + B_HEADS:src_gates], pad], axis=-1).astype(BF16)


def _hgrn_state_in(s):
    return jnp.swapaxes(s.astype(F32), -1, -2)


def _ssd_state_in(s):
    b = s.shape[0]
    s = s.astype(F32).reshape(b, B_GROUPS, B_HEADS // B_GROUPS, B_HEADDIM, B_STATE)
    return s.transpose(0, 1, 4, 2, 3).reshape(b, B_GROUPS, B_STATE, B_GW)


def kernel(x_prompt, x_sample, c, state_hgrn, state_ssd, c_ctx, w_ada, b_ada, ln1, ln2, ln_f, w_in, lb_logits,
           a_norm, conv_w, conv_b, dt_bias, a_log, d_skip, b_norm, w_br_a, w_br_b, w_out, w_ff_up, ff_conv_w,
           ff_conv_b, w_ff_down):
    depth = w_in.shape[0]
    dec_b = x_sample.shape[0]
    assert depth == DEPTH and dec_b + 1 <= MOD_ROWS
    assert x_prompt.shape[1] == TILE and x_sample.shape[1] % TILE == 0 and TILE % GRID_W == 0

    cvec = jnp.concatenate([c_ctx[None].astype(F32), c.astype(F32),
                            jnp.zeros((MOD_ROWS - 1 - dec_b, D_MODEL), F32)], axis=0)
    mod_all = _modulation(cvec, w_ada.astype(F32), b_ada.astype(F32))

    expand = (jnp.arange(2 * LANES)[:, None] % LANES == jnp.arange(B_WIDTH)[None, :] // B_HEADDIM).astype(BF16)
    stacked = dict(w_in=_pack_w_in(w_in), w_br_a=w_br_a.astype(BF16), w_br_b=w_br_b.astype(BF16),
                   w_out=w_out.astype(BF16), w_ff_up=w_ff_up.astype(BF16), w_ff_down=w_ff_down.astype(BF16))
    layers = []
    for l in range(depth):
        layers.append(dict(
            stacked, ln1=ln1[l][None].astype(F32), ln2=ln2[l][None].astype(F32),
            lbl=(lb_logits[:, 0, :].astype(F32), lb_logits[:, 1, :].astype(F32)),
            a_norm=a_norm[l][None].astype(F32), b_norm=b_norm[l][None].astype(F32),
            conv_w=conv_w[l].astype(F32), conv_b=conv_b[l][None].astype(F32),
            dt_bias=(_pad_heads(dt_bias[l, 0])[None], _pad_heads(dt_bias[l, 1])[None]),
            a_log=(_pad_heads(a_log[l, 0])[None], _pad_heads(a_log[l, 1])[None]),
            d_skip=jnp.repeat(d_skip[l].astype(F32), B_HEADDIM)[None], expand=expand,
            ff_conv_w=ff_conv_w[l].astype(F32), ff_conv_b=ff_conv_b[l][None].astype(F32)))
    ln_f2 = ln_f[None].astype(F32)

    def run(x, row_block, mod_row0, mod_stride, states, emit_state):
        acc = None
        for l in range(depth):
            lp = layers[l]
            mod = mod_all[l].reshape(MOD_ROWS, 1, MOD_COLS)
            init_f = init_b = None
            if states is not None:
                init_f = (_hgrn_state_in(states[0][:, l, 0]), _ssd_state_in(states[1][:, l, 0]))
                init_b = (_hgrn_state_in(states[0][:, l, 1]), _ssd_state_in(states[1][:, l, 1]))
            fo = _fwd_call(x, mod, lp, l, row_block, mod_row0, mod_stride, init_f,
                           dict(acc=acc) if emit_state else None)
            if emit_state:
                acc = tuple(fo[len(FWD_OUT):])
            bo = _bwd_call(x, mod, fo[:len(FWD_OUT)], lp, l, mod_row0, mod_stride, init_b,
                           dict(acc=acc) if emit_state else None)
            if emit_state:
                acc = (bo[1], bo[2])
            x = _ffn_call(bo[0], mod, lp, ln_f2, l, row_block, mod_row0, mod_stride, l == depth - 1)
        return x, acc

    y_prompt, (new_hgrn, new_ssd) = run(x_prompt.astype(F32), x_prompt.shape[1], 0, 0, None, True)
    y_sample, _ = run(x_sample.astype(F32), GRID_W, 1, 1, (state_hgrn, state_ssd), False)
    return (y_prompt.astype(x_prompt.dtype), y_sample.astype(x_sample.dtype),
            new_hgrn.astype(x_prompt.dtype), new_ssd.astype(x_prompt.dtype))
```

```python
import functools

import jax
import jax.numpy as jnp
from jax import lax
from jax.experimental import pallas as pl
from jax.experimental.pallas import tpu as pltpu

F32 = jnp.float32
BF16 = jnp.bfloat16

D_MODEL = 1024
DEPTH = 4
GRID_W = 64
A_HEADS = 4
A_DK = 128
A_DV = 128
A_KW = A_HEADS * A_DK
A_WIDTH = A_HEADS * A_DV
B_HEADS = 16
B_HEADDIM = 64
B_WIDTH = B_HEADS * B_HEADDIM
B_GROUPS = 4
B_STATE = 64
B_GN = B_GROUPS * B_STATE
B_CONV_CH = B_WIDTH + 2 * B_GN
B_GW = B_WIDTH // B_GROUPS
D_FF = 2816
N_MOD = 6
EPS = 1e-6
LOG2E = 1.4426950408889634

LANES = 128
SUBLANES = 8
TILE = 256
MOD_ROWS = 8
MOD_COLS = N_MOD * D_MODEL
MOD_BLOCK = 1536
FF_TILE = 512
FF_BLOCK = 256
FF_DOWN_BLOCKS = D_FF // FF_BLOCK
FILL_COLS = 512
VMEM_LIMIT = 60 * 1024 * 1024

C_Q = 0
C_I = C_Q + A_KW
C_FFW = C_I + A_WIDTH
C_FBW = C_FFW + A_KW
C_GA = C_FBW + A_KW
C_Z = C_GA + A_WIDTH
C_XBC = C_Z + B_WIDTH
C_GATES = C_XBC + B_CONV_CH
C_DTFW = C_GATES + 2 * D_MODEL
C_DTBW = C_DTFW + LANES
IN_PACKED = C_DTBW + LANES

FWD_OUT = ((A_KW, BF16), (A_WIDTH, BF16), (A_KW, F32), (B_CONV_CH, BF16), (LANES, F32), (A_WIDTH, BF16),
           (B_WIDTH, BF16), (2 * D_MODEL, BF16), (A_WIDTH, F32), (B_WIDTH, F32), (A_KW, BF16))

NT_DIMS = (((1,), (1,)), ((), ()))
TN_DIMS = (((0,), (0,)), ((), ()))


def _mm(a, b):
    return jnp.dot(a, b, preferred_element_type=F32)


def _mm_nt(a, b):
    return lax.dot_general(a, b, NT_DIMS, preferred_element_type=F32)


def _mm_tn(a, b):
    return lax.dot_general(a, b, TN_DIMS, preferred_element_type=F32)


def _sigmoid(x):
    return 0.5 * jnp.tanh(0.5 * x) + 0.5


def _silu(x):
    return x * _sigmoid(x)


def _softplus(x):
    return jnp.maximum(x, 0.0) + jnp.log1p(jnp.exp(-jnp.abs(x)))


def _rms(x):
    return x * lax.rsqrt(jnp.mean(x * x, axis=-1, keepdims=True) + EPS)


def _iota(shape, dim):
    return lax.broadcasted_iota(jnp.int32, shape, dim)


def _zero_rows(a, period, offset):
    sub = _iota((SUBLANES, a.shape[1]), 0)
    groups = []
    for g in range(a.shape[0] // SUBLANES):
        blk = a[g * SUBLANES:(g + 1) * SUBLANES]
        if (g * SUBLANES) % period == offset - offset % SUBLANES:
            blk = jnp.where(sub == offset % SUBLANES, 0.0, blk)
        groups.append(blk)
    return jnp.concatenate(groups, axis=0)


def _row_neighbours(h, row_block):
    n = h.shape[0]
    return (_zero_rows(pltpu.roll(h, 1, axis=0), row_block, 0),
            _zero_rows(pltpu.roll(h, n - 1, axis=0), row_block, row_block - 1))


def _split_bf16(x, parts):
    out = []
    for i in range(parts):
        if i + 1 < parts:
            p = lax.bitcast_convert_type(lax.bitcast_convert_type(x, jnp.uint32) & jnp.uint32(0xFFFF0000), F32)
            out.append(p.astype(BF16))
            x = x - p
        else:
            out.append(x.astype(BF16))
    return out


def _scan_rows(g, reverse):
    n = g.shape[0]
    row = _iota((n, n), 0)
    col = _iota((n, n), 1)
    tri = jnp.where((col >= row) if reverse else (col <= row), 1.0, 0.0).astype(BF16)
    return _mm(jnp.concatenate([tri, tri, tri], axis=1), jnp.concatenate(_split_bf16(g, 3), axis=0))


def _level_ref(c, read_row, m, off):
    n, w = c.shape
    if 2 * m < 8:
        rk = _iota(c.shape, 0) & (2 * m - 1)
        out = c
        for k in range(2 * m):
            if k != off:
                out = jnp.where(rk == k, pltpu.roll(c, (k - off) % n, axis=0), out)
        return out
    pieces = [jnp.broadcast_to(read_row(j * 2 * m + off), (2 * m, w)) for j in range(n // (2 * m))]
    return pieces[0] if len(pieces) == 1 else jnp.concatenate(pieces, axis=0)


def _neg_abs(x):
    bits = lax.bitcast_convert_type(x, jnp.uint32) | jnp.uint32(0x80000000)
    return lax.bitcast_convert_type(bits, F32)


def _level_exponent(c, g, read_row, m, reverse):
    n = c.shape[0]
    if m > 2:
        return _neg_abs(c - _level_ref(c, read_row, m, m if reverse else m - 1))
    rk = _iota(c.shape, 0) & (2 * m - 1)
    if m == 1:
        return jnp.where(rk == (0 if reverse else 1), g, 0.0)
    up = pltpu.roll(g, n - 1, axis=0)
    down = pltpu.roll(g, 1, axis=0)
    if reverse:
        return jnp.where(rk == 0, g + up, jnp.where(rk == 1, g, jnp.where(rk == 2, 0.0, down)))
    return jnp.where(rk == 0, up, jnp.where(rk == 1, 0.0, jnp.where(rk == 2, g, g + down)))


def _pair_level(n, reverse):
    t = _iota((n, n), 0)
    s = _iota((n, n), 1)
    x = t ^ s
    lvl = jnp.full((n, n), -1, jnp.int32)
    for l in range(n.bit_length() - 1):
        lvl = lvl + (x >= (1 << l)).astype(jnp.int32)
    return jnp.where((s > t) if reverse else (s < t), lvl, -1)


def _hgrn_gates(fx, lb):
    log_sig = jnp.minimum(fx, 0.0) - jnp.log(1.0 + jnp.exp(-jnp.abs(fx)))
    a = jnp.log(lb)
    b = jnp.log1p(-lb) + log_sig
    log_f = jnp.maximum(a, b) + jnp.log(1.0 + jnp.exp(-jnp.abs(a - b)))
    return log_f, (1.0 - lb) * _sigmoid(-fx)


def _lower_bound(logits, layer):
    e = jnp.exp(logits - jnp.max(logits, axis=0, keepdims=True))
    sm = e / jnp.sum(e, axis=0, keepdims=True)
    acc = jnp.zeros_like(sm[0:1])
    for j in range(1, layer + 1):
        acc = acc + sm[j:j + 1]
    return jnp.maximum(acc, 0.0)


def _interleave(gens):
    results = [None] * len(gens)
    live = list(range(len(gens)))
    while live:
        for idx in list(live):
            try:
                next(gens[idx])
            except StopIteration as stop:
                results[idx] = stop.value
                live.remove(idx)
    return results


def _hgrn_dir(q, v, log2_f, k, st_ref, c_ref, reverse):
    n = q.shape[0]
    hb = n // 2
    top = n.bit_length() - 2
    lvl = _pair_level(hb, reverse)
    c_all = _scan_rows(log2_f, reverse)
    zeros = jnp.zeros((hb, LANES), BF16)
    yield
    outs = []
    for pair in range(A_HEADS // 2):
        sl2 = slice(2 * pair * LANES, (2 * pair + 2) * LANES)
        c2 = c_all[:, sl2]
        c_ref[pair] = c2
        read_row = lambda r, pair=pair: c_ref[pair, pl.ds(r, 1), :]

        def scores(qt, kt):
            rhs = jnp.concatenate([jnp.concatenate([kt[:, :LANES], zeros], axis=1),
                                   jnp.concatenate([zeros, kt[:, LANES:]], axis=1)], axis=0)
            return _mm_nt(qt, rhs)

        s_lo = s_hi = None
        for l in range(top + 1):
            m = 1 << l
            w = jnp.exp2(_level_exponent(c2, log2_f[:, sl2], read_row, m, reverse)).astype(BF16)
            qt = q[:, sl2] * w
            kt = k[:, sl2] * w
            if l < top:
                a = scores(qt[:hb], kt[:hb])
                b = scores(qt[hb:], kt[hb:])
                keep = jnp.concatenate([lvl, lvl], axis=1) == l
                s_lo = jnp.where(keep, a, 0.0 if s_lo is None else s_lo)
                s_hi = jnp.where(keep, b, 0.0 if s_hi is None else s_hi)
            elif reverse:
                s_off = scores(qt[:hb], kt[hb:])
            else:
                s_off = scores(qt[hb:], kt[:hb])
            yield
        c_end2 = read_row(0 if reverse else n - 1)
        for e in range(2):
            h = 2 * pair + e
            sl = slice(h * LANES, (h + 1) * LANES)
            se = slice(e * LANES, (e + 1) * LANES)
            qh, vh, kh = q[:, sl], v[:, sl], k[:, sl]
            c, c_end = c2[:, se], c_end2[:, se]
            if reverse:
                o = jnp.concatenate([_mm(jnp.concatenate([s_lo[:, se], s_off[:, se]], axis=1).astype(BF16), vh),
                                     _mm(s_hi[:, se].astype(BF16), vh[hb:])], axis=0)
            else:
                o = jnp.concatenate([_mm(s_lo[:, se].astype(BF16), vh[:hb]),
                                     _mm(jnp.concatenate([s_off[:, se], s_hi[:, se]], axis=1).astype(BF16), vh)],
                                    axis=0)
            st = st_ref[h]
            o = o + jnp.sum((qh * kh).astype(F32), axis=-1, keepdims=True) * vh.astype(F32)
            o = o + _mm_nt(qh * jnp.exp2(c).astype(BF16), st.astype(BF16))
            st_ref[h] = st * jnp.exp2(c_end) + _mm_tn(vh, kh * jnp.exp2(c_end - c).astype(BF16))
            outs.append(o)
            yield
    return jnp.concatenate(outs, axis=1)


def _expand_heads(a):
    r = a.shape[0]
    lane = _iota((r, LANES), 1)
    slabs = []
    for j in range(B_HEADS // 2):
        lo = jnp.broadcast_to(a[:, 2 * j:2 * j + 1], (r, LANES))
        hi = jnp.broadcast_to(a[:, 2 * j + 1:2 * j + 2], (r, LANES))
        slabs.append(jnp.where(lane < B_HEADDIM, lo, hi))
    return jnp.concatenate(slabs, axis=1)


def _expand_heads_mxu(a, e2):
    a = jnp.where(_iota(a.shape, 1) < B_HEADS, a, 0.0)
    return _mm(jnp.concatenate(_split_bf16(a, 2), axis=1), e2)


def _ssd_dir(xs, bm, cm, dt_raw, dt_bias, a_log, e2, st_ref, reverse):
    n = xs.shape[0]
    dt = _softplus(dt_raw + dt_bias)
    ac = _scan_rows((-LOG2E) * jnp.exp(a_log) * dt, reverse)
    ac_t = (ac - jnp.log2(dt)).T
    hb = n // 2
    row_s = _iota((hb, hb), 0)
    col_s = _iota((hb, hb), 1)
    tri = (col_s >= row_s) if reverse else (col_s <= row_s)
    lane_gn = _iota((n, B_GN), 1)
    lane = _iota((n, LANES), 1)
    lo, hi = slice(0, hb), slice(hb, n)
    yield
    slabs = []
    for g in range(B_GROUPS):
        cb = _mm_nt(jnp.where((lane_gn >> 6) == g, cm, jnp.zeros_like(cm)), bm)
        yield
        for jp in range(2):
            slab = xs[:, (2 * g + jp) * LANES:(2 * g + jp + 1) * LANES]
            zero = jnp.zeros_like(slab)
            slab_e = (jnp.where(lane < B_HEADDIM, slab, zero), jnp.where(lane < B_HEADDIM, zero, slab))
            w_lo, w_hi = [], []
            for e in range(2):
                h = 4 * g + 2 * jp + e
                col = ac[:, h:h + 1]
                row = ac_t[h:h + 1, :]

                def quad(rs, cs, masked):
                    seg = col[rs] - row[:, cs]
                    if masked:
                        seg = jnp.where(tri, seg, -1e30)
                    return (jnp.exp2(seg) * cb[rs, cs]).astype(BF16)

                if reverse:
                    w_lo += [quad(lo, lo, True), quad(lo, hi, False)]
                    w_hi += [quad(hi, hi, True)]
                else:
                    w_lo += [quad(lo, lo, True)]
                    w_hi += [quad(hi, lo, False), quad(hi, hi, True)]
                yield
            if reverse:
                x_lo = jnp.concatenate([slab_e[0], slab_e[1]], axis=0)
                x_hi = jnp.concatenate([slab_e[0][hi], slab_e[1][hi]], axis=0)
            else:
                x_lo = jnp.concatenate([slab_e[0][lo], slab_e[1][lo]], axis=0)
                x_hi = jnp.concatenate([slab_e[0], slab_e[1]], axis=0)
            slabs.append(jnp.concatenate([_mm(jnp.concatenate(w_lo, axis=1), x_lo),
                                          _mm(jnp.concatenate(w_hi, axis=1), x_hi)], axis=0))
    y = jnp.concatenate(slabs, axis=1)
    st = st_ref[...]
    y = y + _mm(cm, st.astype(BF16)) * _expand_heads_mxu(jnp.exp2(ac), e2)
    yield
    a_end = ac[0:1, :] if reverse else ac[n - 1:n, :]
    wgt = _expand_heads_mxu(jnp.exp2(a_end - ac) * dt, e2)
    upd = _mm_tn(bm, xs * wgt.astype(BF16))
    block = (_iota(st.shape, 0) >> 6) == (_iota(st.shape, 1) >> 8)
    st_ref[...] = jnp.where(block, st * _expand_heads(jnp.exp2(a_end)) + upd, 0.0)
    return y


def _load_states(sa_in, sb_in, sa_ref, sb_ref):
    if sa_in is None:
        sa_ref[...] = jnp.zeros_like(sa_ref)
        sb_ref[...] = jnp.zeros_like(sb_ref)
    else:
        sa_ref[...] = sa_in[0]
        sb_ref[...] = jnp.zeros_like(sb_ref)
        for g in range(B_GROUPS):
            sb_ref[g * B_STATE:(g + 1) * B_STATE, g * B_GW:(g + 1) * B_GW] = sb_in[0, g]


def _store_states(sa_out, sb_out, sa_ref, sb_ref):
    for h in range(A_HEADS):
        sa_out[0, 0, 0, h] = sa_ref[h].T
    per = B_HEADS // B_GROUPS
    for pair in range(B_GROUPS // 2):
        t = sb_ref[pair * 2 * B_STATE:(pair + 1) * 2 * B_STATE, pair * 2 * B_GW:(pair + 1) * 2 * B_GW].T
        for gl in range(2):
            for j in range(per):
                r0 = gl * B_GW + j * B_HEADDIM
                sb_out[0, 0, 0, (2 * pair + gl) * per + j] = t[r0:r0 + B_HEADDIM, gl * B_STATE:(gl + 1) * B_STATE]


def _mod_kernel(c_ref, w_ref, b_ref, o_ref):
    o_ref[0] = jnp.dot(_silu(c_ref[...]), w_ref[0], preferred_element_type=F32,
                       precision=lax.Precision.HIGHEST) + b_ref[0]


def _modulation(cvec, w_ada, b_ada):
    depth = w_ada.shape[0]
    return pl.pallas_call(
        _mod_kernel,
        grid=(depth, MOD_COLS // MOD_BLOCK),
        in_specs=[
            pl.BlockSpec((MOD_ROWS, D_MODEL), lambda l, j: (0, 0)),
            pl.BlockSpec((1, D_MODEL, MOD_BLOCK), lambda l, j: (l, 0, j)),
            pl.BlockSpec((1, 1, MOD_BLOCK), lambda l, j: (l, 0, j)),
        ],
        out_specs=pl.BlockSpec((1, MOD_ROWS, MOD_BLOCK), lambda l, j: (l, 0, j)),
        out_shape=jax.ShapeDtypeStruct((depth, MOD_ROWS, MOD_COLS), F32),
        compiler_params=pltpu.CompilerParams(dimension_semantics=("arbitrary", "arbitrary"),
                                             vmem_limit_bytes=VMEM_LIMIT),
        name="adaln_mod",
    )(cvec, w_ada, b_ada.reshape(depth, 1, MOD_COLS))


def _const_spec(shape):
    nd = len(shape)
    return pl.BlockSpec(shape, lambda *_: (0,) * nd, pipeline_mode=pl.Buffered(1))


def _layer_spec(shape, layer):
    nd = len(shape)
    return pl.BlockSpec((1,) + shape, lambda *_: (layer,) + (0,) * nd, pipeline_mode=pl.Buffered(1))


def _tile_kernel_params():
    return pltpu.CompilerParams(dimension_semantics=("arbitrary", "arbitrary"), vmem_limit_bytes=VMEM_LIMIT)


def _state_scratch():
    return [pltpu.VMEM((A_HEADS, A_DV, A_DK), F32), pltpu.VMEM((B_GN, B_WIDTH), F32),
            pltpu.VMEM((A_HEADS // 2, TILE, 2 * LANES), F32)]


def _state_out_shapes(bsz):
    return [jax.ShapeDtypeStruct((bsz, DEPTH, 2, A_HEADS, A_DK, A_DV), F32),
            jax.ShapeDtypeStruct((bsz, DEPTH, 2, B_HEADS, B_HEADDIM, B_STATE), F32)]


def _fwd_kernel(*refs, layer, row_block, has_init, emit_state, n_alias):
    (x_ref, mod_ref, ln_ref, w_ref, lbl_ref, lblb_ref, cw_ref, cb_ref, dtb_ref, alog_ref, dskip_ref,
     e2_ref), refs = refs[:12], refs[12:]
    if has_init:
        (sa_in, sb_in), refs = refs[:2], refs[2:]
    else:
        sa_in = sb_in = None
    refs = refs[n_alias:]
    (q_out, v_out, l2fbw_out, xbc_out, dtbw_out, ga_out, z_out, gates_out, oa_out, yb_out,
     kbw_out), refs = refs[:11], refs[11:]
    if emit_state:
        (sa_out, sb_out), refs = refs[:2], refs[2:]
    sa_ref, sb_ref, c_ref = refs

    i = pl.program_id(1)

    @pl.when(i == 0)
    def _():
        _load_states(sa_in, sb_in, sa_ref, sb_ref)

    x = x_ref[0]
    mod = mod_ref[0]
    shift, scale = mod[:, 0:D_MODEL], mod[:, D_MODEL:2 * D_MODEL]
    u = ((_rms(x) * ln_ref[...]) * (1.0 + scale) + shift).astype(BF16)

    def proj(lo, hi):
        return _mm(u, w_ref[0, :, lo:hi])

    def backward_gates(fx):
        log_f, k = _hgrn_gates(fx, _lower_bound(lblb_ref[...], layer))
        kbw_out[0] = k.astype(BF16)
        return log_f * LOG2E

    pending = [(ga_out, 0, C_GA, A_WIDTH, _silu), (l2fbw_out, 0, C_FBW, A_KW, backward_gates)]
    pending += [(z_out, o, C_Z + o, FILL_COLS, _silu) for o in range(0, B_WIDTH, FILL_COLS)]
    pending += [(gates_out, o, C_GATES + o, FILL_COLS, _sigmoid) for o in range(0, 2 * D_MODEL, FILL_COLS)]
    pending.append((dtbw_out, 0, C_DTBW, LANES, lambda raw: raw))

    def deferred():
        for ref, dst, src, width, fn in pending:
            raw = proj(src, src + width)
            yield
            ref[0, :, dst:dst + width] = fn(raw).astype(ref.dtype)
            yield

    fills = deferred()
    q = proj(C_Q, C_I)
    v = proj(C_I, C_FFW)
    fx = proj(C_FFW, C_FBW)
    xbc = proj(C_XBC, C_GATES)
    dt_raw = proj(C_DTFW, C_DTBW)
    next(fills)
    q = _silu(q).astype(BF16)
    v = v.astype(BF16)
    q_out[0] = q
    v_out[0] = v
    log_f, k = _hgrn_gates(fx, _lower_bound(lbl_ref[...], layer))
    next(fills), next(fills), next(fills)

    prev, nxt = _row_neighbours(xbc, row_block)
    cw = cw_ref[...]
    xbc = _silu(cb_ref[...] + prev * cw[0:1] + xbc * cw[1:2] + nxt * cw[2:3])
    xbc16 = xbc.astype(BF16)
    xbc_out[0] = xbc16

    o_a, y, _ = _interleave([
        _hgrn_dir(q, v, log_f * LOG2E, k.astype(BF16), sa_ref, c_ref, False),
        _ssd_dir(xbc16[:, 0:B_WIDTH], xbc16[:, B_WIDTH:B_WIDTH + B_GN], xbc16[:, B_WIDTH + B_GN:B_CONV_CH],
                 dt_raw, dtb_ref[...], alog_ref[...], e2_ref[...], sb_ref, False),
        fills])
    oa_out[0] = o_a
    yb_out[0] = y + dskip_ref[...] * xbc[:, 0:B_WIDTH]

    if emit_state:
        @pl.when(i == pl.num_programs(1) - 1)
        def _():
            _store_states(sa_out, sb_out, sa_ref, sb_ref)


def _fwd_call(x, mod, lp, layer, row_block, mod_row0, mod_stride, init, emit):
    bsz, length, _ = x.shape
    nt = length // TILE

    def tok(width):
        return pl.BlockSpec((1, TILE, width), lambda b, i: (b, i, 0))

    def per_seq(shape):
        return pl.BlockSpec((1,) + shape, lambda b, i: (b,) + (0,) * len(shape))

    in_specs = [tok(D_MODEL), pl.BlockSpec((1, 1, MOD_COLS), lambda b, i: (mod_row0 + mod_stride * b, 0, 0)),
                _const_spec((1, D_MODEL)), _layer_spec((D_MODEL, IN_PACKED), layer), _const_spec((DEPTH, A_KW)),
                _const_spec((DEPTH, A_KW)), _const_spec((3, B_CONV_CH)), _const_spec((1, B_CONV_CH)), _const_spec((1, LANES)),
                _const_spec((1, LANES)), _const_spec((1, B_WIDTH)), _const_spec((2 * LANES, B_WIDTH))]
    args = [x, mod, lp['ln1'], lp['w_in'], lp['lbl'][0], lp['lbl'][1], lp['conv_w'], lp['conv_b'], lp['dt_bias'][0],
            lp['a_log'][0], lp['d_skip'], lp['expand']]
    if init is not None:
        in_specs += [per_seq((A_HEADS, A_DV, A_DK)), per_seq((B_GROUPS, B_STATE, B_GW))]
        args += list(init)
    out_specs = [tok(w) for w, _ in FWD_OUT]
    out_shape = [jax.ShapeDtypeStruct((bsz, length, w), dt) for w, dt in FWD_OUT]
    aliases, n_alias = {}, 0
    if emit is not None:
        out_specs += [pl.BlockSpec((1, 1, 1, A_HEADS, A_DK, A_DV), lambda b, i: (b, layer, 0, 0, 0, 0)),
                      pl.BlockSpec((1, 1, 1, B_HEADS, B_HEADDIM, B_STATE), lambda b, i: (b, layer, 0, 0, 0, 0))]
        out_shape += _state_out_shapes(bsz)
        if emit['acc'] is not None:
            aliases = {len(args): len(FWD_OUT), len(args) + 1: len(FWD_OUT) + 1}
            in_specs += [pl.BlockSpec(memory_space=pl.ANY), pl.BlockSpec(memory_space=pl.ANY)]
            args += list(emit['acc'])
            n_alias = 2
    return pl.pallas_call(
        functools.partial(_fwd_kernel, layer=layer, row_block=row_block, has_init=init is not None,
                          emit_state=emit is not None, n_alias=n_alias),
        grid=(bsz, nt), in_specs=in_specs, out_specs=out_specs, out_shape=out_shape,
        input_output_aliases=aliases, scratch_shapes=_state_scratch(),
        compiler_params=_tile_kernel_params(),
        name="mixer_fwd",
    )(*args)


def _bwd_kernel(*refs, has_init, emit_state, n_alias):
    (x_ref, mod_ref, q_ref, v_ref, l2fbw_ref, xbc_ref, dtbw_ref, ga_ref, z_ref, gates_ref, oa_ref, yb_ref, kbw_ref,
     dtb_ref, alog_ref, an_ref, bn_ref, wa_ref, wb_ref, wo_ref, e2_ref), refs = refs[:21], refs[21:]
    if has_init:
        (sa_in, sb_in), refs = refs[:2], refs[2:]
    else:
        sa_in = sb_in = None
    refs = refs[n_alias:]
    x_out, refs = refs[0], refs[1:]
    if emit_state:
        (sa_out, sb_out), refs = refs[:2], refs[2:]
    sa_ref, sb_ref, c_ref = refs

    i = pl.program_id(1)

    @pl.when(i == 0)
    def _():
        _load_states(sa_in, sb_in, sa_ref, sb_ref)

    def branch_a():
        o_a = yield from _hgrn_dir(q_ref[0], v_ref[0], l2fbw_ref[0], kbw_ref[0], sa_ref, c_ref, True)
        o_a = oa_ref[0] + o_a
        an = an_ref[...]
        o_a = jnp.concatenate(
            [_rms(o_a[:, h * A_DV:(h + 1) * A_DV]) * an[:, h * A_DV:(h + 1) * A_DV] for h in range(A_HEADS)],
            axis=1)
        o_a = o_a * ga_ref[0].astype(F32)
        yield
        return gates_ref[0, :, 0:D_MODEL].astype(F32) * _mm(o_a.astype(BF16), wa_ref[0])

    xbc = xbc_ref[0]
    merged_a, y_b = _interleave([
        branch_a(),
        _ssd_dir(xbc[:, 0:B_WIDTH], xbc[:, B_WIDTH:B_WIDTH + B_GN], xbc[:, B_WIDTH + B_GN:B_CONV_CH],
                 dtbw_ref[0], dtb_ref[...], alog_ref[...], e2_ref[...], sb_ref, True)])
    y_b = (yb_ref[0] + y_b) * z_ref[0].astype(F32)
    bn = bn_ref[...]
    gw = B_WIDTH // B_GROUPS
    y_b = jnp.concatenate(
        [_rms(y_b[:, g * gw:(g + 1) * gw]) * bn[:, g * gw:(g + 1) * gw] for g in range(B_GROUPS)], axis=1)

    merged = merged_a + gates_ref[0, :, D_MODEL:2 * D_MODEL].astype(F32) * _mm(y_b.astype(BF16), wb_ref[0])
    mix = _mm(merged.astype(BF16), wo_ref[0])
    gate1 = mod_ref[0][:, 2 * D_MODEL:3 * D_MODEL]
    x_out[0] = x_ref[0] + gate1 * mix

    if emit_state:
        @pl.when(i == pl.num_programs(1) - 1)
        def _():
            _store_states(sa_out, sb_out, sa_ref, sb_ref)


def _bwd_call(x, mod, fwd_outs, lp, layer, mod_row0, mod_stride, init, emit):
    bsz, length, _ = x.shape
    nt = length // TILE

    def tok(width):
        return pl.BlockSpec((1, TILE, width), lambda b, i: (b, nt - 1 - i, 0))

    def per_seq(shape):
        return pl.BlockSpec((1,) + shape, lambda b, i: (b,) + (0,) * len(shape))

    in_specs = ([tok(D_MODEL), pl.BlockSpec((1, 1, MOD_COLS), lambda b, i: (mod_row0 + mod_stride * b, 0, 0))]
                + [tok(w) for w, _ in FWD_OUT]
                + [_const_spec((1, LANES)), _const_spec((1, LANES)),
                   _const_spec((1, A_WIDTH)), _const_spec((1, B_WIDTH)), _layer_spec((A_WIDTH, D_MODEL), layer),
                   _layer_spec((B_WIDTH, D_MODEL), layer), _layer_spec((D_MODEL, D_MODEL), layer),
                   _const_spec((2 * LANES, B_WIDTH))])
    args = ([x, mod] + list(fwd_outs)
            + [lp['dt_bias'][1], lp['a_log'][1], lp['a_norm'], lp['b_norm'], lp['w_br_a'],
               lp['w_br_b'], lp['w_out'], lp['expand']])
    if init is not None:
        in_specs += [per_seq((A_HEADS, A_DV, A_DK)), per_seq((B_GROUPS, B_STATE, B_GW))]
        args += list(init)
    out_specs = [tok(D_MODEL)]
    out_shape = [jax.ShapeDtypeStruct((bsz, length, D_MODEL), F32)]
    aliases, n_alias = {}, 0
    if emit is not None:
        out_specs += [pl.BlockSpec((1, 1, 1, A_HEADS, A_DK, A_DV), lambda b, i: (b, layer, 1, 0, 0, 0)),
                      pl.BlockSpec((1, 1, 1, B_HEADS, B_HEADDIM, B_STATE), lambda b, i: (b, layer, 1, 0, 0, 0))]
        out_shape += _state_out_shapes(bsz)
        aliases = {len(args): 1, len(args) + 1: 2}
        in_specs += [pl.BlockSpec(memory_space=pl.ANY), pl.BlockSpec(memory_space=pl.ANY)]
        args += list(emit['acc'])
        n_alias = 2
    return pl.pallas_call(
        functools.partial(_bwd_kernel, has_init=init is not None, emit_state=emit is not None, n_alias=n_alias),
        grid=(bsz, nt), in_specs=in_specs, out_specs=out_specs, out_shape=out_shape,
        input_output_aliases=aliases, scratch_shapes=_state_scratch(),
        compiler_params=_tile_kernel_params(),
        name="mixer_bwd",
    )(*args)


def _ffn_kernel(x_ref, mod_ref, ln_ref, wu_ref, cw_ref, cb_ref, wd_ref, lnf_ref, o_ref, *, row_block, final_norm):
    x = x_ref[0]
    mod = mod_ref[0]
    shift, scale, gate = (mod[:, 3 * D_MODEL:4 * D_MODEL], mod[:, 4 * D_MODEL:5 * D_MODEL],
                          mod[:, 5 * D_MODEL:6 * D_MODEL])
    u = ((_rms(x) * ln_ref[...]) * (1.0 + scale) + shift).astype(BF16)

    def up(j):
        return (_mm(u, wu_ref[0, :, j * FF_BLOCK:(j + 1) * FF_BLOCK]),
                _mm(u, wu_ref[0, :, D_FF + j * FF_BLOCK:D_FF + (j + 1) * FF_BLOCK]))

    def conv(h, lo):
        cw = cw_ref[:, lo:lo + FF_BLOCK]
        prev, nxt = _row_neighbours(h, row_block)
        return cb_ref[:, lo:lo + FF_BLOCK] + prev * cw[0:1] + h * cw[1:2] + nxt * cw[2:3]

    steps = D_FF // FF_BLOCK
    acc = None
    acts = []
    h = up(0)
    for j in range(steps):
        h_next = up(j + 1) if j + 1 < steps else None
        acts.append((_silu(conv(h[0], j * FF_BLOCK)) * conv(h[1], D_FF + j * FF_BLOCK)).astype(BF16))
        if len(acts) == FF_DOWN_BLOCKS or j + 1 == steps:
            lo = (j + 1 - len(acts)) * FF_BLOCK
            part = _mm(acts[0] if len(acts) == 1 else jnp.concatenate(acts, axis=1),
                       wd_ref[0, lo:(j + 1) * FF_BLOCK, :])
            acc = part if acc is None else acc + part
            acts = []
        h = h_next
    y = x + gate * acc
    if final_norm:
        y = _rms(y) * lnf_ref[...]
    o_ref[0] = y


def _ffn_call(x, mod, lp, ln_f, layer, row_block, mod_row0, mod_stride, final_norm):
    shape = x.shape
    if shape[1] % FF_TILE:
        assert mod_stride == 0 and FF_TILE % shape[1] == 0 and shape[1] % row_block == 0
        x = x.reshape(shape[0] * shape[1] // FF_TILE, FF_TILE, D_MODEL)
    bsz, length, _ = x.shape
    nt = length // FF_TILE
    tok = pl.BlockSpec((1, FF_TILE, D_MODEL), lambda b, i: (b, i, 0))
    return pl.pallas_call(
        functools.partial(_ffn_kernel, row_block=row_block, final_norm=final_norm),
        grid=(bsz, nt),
        in_specs=[tok, pl.BlockSpec((1, 1, MOD_COLS), lambda b, i: (mod_row0 + mod_stride * b, 0, 0)),
                  _const_spec((1, D_MODEL)), _layer_spec((D_MODEL, 2 * D_FF), layer), _const_spec((3, 2 * D_FF)),
                  _const_spec((1, 2 * D_FF)), _layer_spec((D_FF, D_MODEL), layer), _const_spec((1, D_MODEL))],
        out_specs=tok,
        out_shape=jax.ShapeDtypeStruct((bsz, length, D_MODEL), F32),
        compiler_params=_tile_kernel_params(),
        name="conv_ffn",
    )(x, mod, lp['ln2'], lp['w_ff_up'], lp['ff_conv_w'], lp['ff_conv_b'], lp['w_ff_down'], ln_f).reshape(shape)


def _pad_heads(a):
    return jnp.pad(a.astype(F32), [(0, 0)] * (a.ndim - 1) + [(0, LANES - a.shape[-1])])


def _pack_w_in(w):
    src_gates = C_XBC + B_CONV_CH + 2 * B_HEADS
    pad = jnp.zeros(w.shape[:-1] + (LANES - B_HEADS,), w.dtype)
    return jnp.concatenate([w[..., :C_GATES], w[..., src_gates:], w[..., C_GATES:C_GATES + B_HEADS], pad,
                            w[..., C_GATES + B_HEADS:src_gates], pad], axis=-1).astype(BF16)


def _hgrn_state_in(s):
    return jnp.swapaxes(s.astype(F32), -1, -2)


def _ssd_state_in(s):
    b = s.shape[0]
    s = s.astype(F32).reshape(b, B_GROUPS, B_HEADS // B_GROUPS, B_HEADDIM, B_STATE)
    return s.transpose(0, 1, 4, 2, 3).reshape(b, B_GROUPS, B_STATE, B_GW)


def kernel(x_prompt, x_sample, c, state_hgrn, state_ssd, c_ctx, w_ada, b_ada, ln1, ln2, ln_f, w_in, lb_logits,
           a_norm, conv_w, conv_b, dt_bias, a_log, d_skip, b_norm, w_br_a, w_br_b, w_out, w_ff_up, ff_conv_w,
           ff_conv_b, w_ff_down):
    depth = w_in.shape[0]
    dec_b = x_sample.shape[0]
    assert depth == DEPTH and dec_b + 1 <= MOD_ROWS
    assert x_prompt.shape[1] == TILE and x_sample.shape[1] % TILE == 0 and TILE % GRID_W == 0

    cvec = jnp.concatenate([c_ctx[None].astype(F32), c.astype(F32),
                            jnp.zeros((MOD_ROWS - 1 - dec_b, D_MODEL), F32)], axis=0)
    mod_all = _modulation(cvec, w_ada.astype(F32), b_ada.astype(F32))

    expand = (jnp.arange(2 * LANES)[:, None] % LANES == jnp.arange(B_WIDTH)[None, :] // B_HEADDIM).astype(BF16)
    stacked = dict(w_in=_pack_w_in(w_in), w_br_a=w_br_a.astype(BF16), w_br_b=w_br_b.astype(BF16),
                   w_out=w_out.astype(BF16), w_ff_up=w_ff_up.astype(BF16), w_ff_down=w_ff_down.astype(BF16))
    layers = []
    for l in range(depth):
        layers.append(dict(
            stacked, ln1=ln1[l][None].astype(F32), ln2=ln2[l][None].astype(F32),
            lbl=(lb_logits[:, 0, :].astype(F32), lb_logits[:, 1, :].astype(F32)),
            a_norm=a_norm[l][None].astype(F32), b_norm=b_norm[l][None].astype(F32),
            conv_w=conv_w[l].astype(F32), conv_b=conv_b[l][None].astype(F32),
            dt_bias=(_pad_heads(dt_bias[l, 0])[None], _pad_heads(dt_bias[l, 1])[None]),
            a_log=(_pad_heads(a_log[l, 0])[None], _pad_heads(a_log[l, 1])[None]),
            d_skip=jnp.repeat(d_skip[l].astype(F32), B_HEADDIM)[None], expand=expand,
            ff_conv_w=ff_conv_w[l].astype(F32), ff_conv_b=ff_conv_b[l][None].astype(F32)))
    ln_f2 = ln_f[None].astype(F32)

    def run(x, row_block, mod_row0, mod_stride, states, emit_state):
        acc = None
        for l in range(depth):
            lp = layers[l]
            mod = mod_all[l].reshape(MOD_ROWS, 1, MOD_COLS)
            init_f = init_b = None
            if states is not None:
                init_f = (_hgrn_state_in(states[0][:, l, 0]), _ssd_state_in(states[1][:, l, 0]))
                init_b = (_hgrn_state_in(states[0][:, l, 1]), _ssd_state_in(states[1][:, l, 1]))
            fo = _fwd_call(x, mod, lp, l, row_block, mod_row0, mod_stride, init_f,
                           dict(acc=acc) if emit_state else None)
            if emit_state:
                acc = tuple(fo[len(FWD_OUT):])
            bo = _bwd_call(x, mod, fo[:len(FWD_OUT)], lp, l, mod_row0, mod_stride, init_b,
                           dict(acc=acc) if emit_state else None)
            if emit_state:
                acc = (bo[1], bo[2])
            x = _ffn_call(bo[0], mod, lp, ln_f2, l, row_block, mod_row0, mod_stride, l == depth - 1)
        return x, acc

    y_prompt, (new_hgrn, new_ssd) = run(x_prompt.astype(F32), x_prompt.shape[1], 0, 0, None, True)
    y_sample, _ = run(x_sample.astype(F32), GRID_W, 1, 1, (state_hgrn, state_ssd), False)
    return (y_prompt.astype(x_prompt.dtype), y_sample.astype(x_sample.dtype),
            new_hgrn.astype(x_prompt.dtype), new_ssd.astype(x_prompt.dtype))
```

```python
import functools

import jax
import jax.numpy as jnp
from jax import lax
from jax.experimental import pallas as pl
from jax.experimental.pallas import tpu as pltpu

F32 = jnp.float32
BF16 = jnp.bfloat16

D_MODEL = 1024
DEPTH = 4
GRID_W = 64
A_HEADS = 4
A_DK = 128
A_DV = 128
A_KW = A_HEADS * A_DK
A_WIDTH = A_HEADS * A_DV
B_HEADS = 16
B_HEADDIM = 64
B_WIDTH = B_HEADS * B_HEADDIM
B_GROUPS = 4
B_STATE = 64
B_GN = B_GROUPS * B_STATE
B_CONV_CH = B_WIDTH + 2 * B_GN
B_GW = B_WIDTH // B_GROUPS
D_FF = 2816
N_MOD = 6
EPS = 1e-6
LOG2E = 1.4426950408889634

LANES = 128
SUBLANES = 8
TILE = 256
MOD_ROWS = 8
MOD_COLS = N_MOD * D_MODEL
MOD_BLOCK = 1536
FF_TILE = 512
FF_BLOCK = 256
FF_DOWN_BLOCKS = D_FF // FF_BLOCK
FILL_COLS = 512
VMEM_LIMIT = 60 * 1024 * 1024

C_Q = 0
C_I = C_Q + A_KW
C_FFW = C_I + A_WIDTH
C_FBW = C_FFW + A_KW
C_GA = C_FBW + A_KW
C_Z = C_GA + A_WIDTH
C_XBC = C_Z + B_WIDTH
C_GATES = C_XBC + B_CONV_CH
C_DTFW = C_GATES + 2 * D_MODEL
C_DTBW = C_DTFW + LANES
IN_PACKED = C_DTBW + LANES

FWD_OUT = ((A_KW, BF16), (A_WIDTH, BF16), (A_KW, F32), (B_CONV_CH, BF16), (LANES, F32), (A_WIDTH, BF16),
           (B_WIDTH, BF16), (2 * D_MODEL, BF16), (A_WIDTH, F32), (B_WIDTH, F32), (A_KW, BF16))

NT_DIMS = (((1,), (1,)), ((), ()))
TN_DIMS = (((0,), (0,)), ((), ()))


def _mm(a, b):
    return jnp.dot(a, b, preferred_element_type=F32)


def _mm_nt(a, b):
    return lax.dot_general(a, b, NT_DIMS, preferred_element_type=F32)


def _mm_tn(a, b):
    return lax.dot_general(a, b, TN_DIMS, preferred_element_type=F32)


def _sigmoid(x):
    return 0.5 * jnp.tanh(0.5 * x) + 0.5


def _silu(x):
    h = 0.5 * x
    return h + h * jnp.tanh(h)


def _softplus(x):
    return jnp.maximum(x, 0.0) + jnp.log1p(jnp.exp(-jnp.abs(x)))


def _rms(x):
    return x * lax.rsqrt(jnp.mean(x * x, axis=-1, keepdims=True) + EPS)


def _iota(shape, dim):
    return lax.broadcasted_iota(jnp.int32, shape, dim)


def _zero_rows(a, period, offset):
    sub = _iota((SUBLANES, a.shape[1]), 0)
    groups = []
    for g in range(a.shape[0] // SUBLANES):
        blk = a[g * SUBLANES:(g + 1) * SUBLANES]
        if (g * SUBLANES) % period == offset - offset % SUBLANES:
            blk = jnp.where(sub == offset % SUBLANES, 0.0, blk)
        groups.append(blk)
    return jnp.concatenate(groups, axis=0)


def _row_neighbours(h, row_block):
    n = h.shape[0]
    return (_zero_rows(pltpu.roll(h, 1, axis=0), row_block, 0),
            _zero_rows(pltpu.roll(h, n - 1, axis=0), row_block, row_block - 1))


def _split_bf16(x, parts):
    out = []
    for i in range(parts):
        if i + 1 < parts:
            p = lax.bitcast_convert_type(lax.bitcast_convert_type(x, jnp.uint32) & jnp.uint32(0xFFFF0000), F32)
            out.append(p.astype(BF16))
            x = x - p
        else:
            out.append(x.astype(BF16))
    return out


def _scan_rows(g, reverse):
    n = g.shape[0]
    row = _iota((n, n), 0)
    col = _iota((n, n), 1)
    tri = jnp.where((col >= row) if reverse else (col <= row), 1.0, 0.0).astype(BF16)
    return _mm(jnp.concatenate([tri, tri, tri], axis=1), jnp.concatenate(_split_bf16(g, 3), axis=0))


def _level_ref(c, read_row, m, off):
    n, w = c.shape
    pieces = [jnp.broadcast_to(read_row(j * 2 * m + off), (2 * m, w)) for j in range(n // (2 * m))]
    return pieces[0] if len(pieces) == 1 else jnp.concatenate(pieces, axis=0)


def _neg_abs(x):
    bits = lax.bitcast_convert_type(x, jnp.uint32) | jnp.uint32(0x80000000)
    return lax.bitcast_convert_type(bits, F32)


def _level_exponent(c, g, read_row, m, reverse):
    n = c.shape[0]
    if m > 2:
        return _neg_abs(c - _level_ref(c, read_row, m, m if reverse else m - 1))
    rk = _iota(c.shape, 0) & (2 * m - 1)
    if m == 1:
        return jnp.where(rk == (0 if reverse else 1), g, 0.0)
    up = pltpu.roll(g, n - 1, axis=0)
    down = pltpu.roll(g, 1, axis=0)
    if reverse:
        return jnp.where(rk == 0, g + up, jnp.where(rk == 1, g, jnp.where(rk == 2, 0.0, down)))
    return jnp.where(rk == 0, up, jnp.where(rk == 1, 0.0, jnp.where(rk == 2, g, g + down)))


def _pair_level(n, reverse):
    t = _iota((n, n), 0)
    s = _iota((n, n), 1)
    x = t ^ s
    lvl = jnp.full((n, n), -1, jnp.int32)
    for l in range(n.bit_length() - 1):
        lvl = lvl + (x >= (1 << l)).astype(jnp.int32)
    return jnp.where((s > t) if reverse else (s < t), lvl, -1)


def _hgrn_gates(fx, lb):
    log_sig = jnp.minimum(fx, 0.0) - jnp.log(1.0 + jnp.exp(-jnp.abs(fx)))
    a = jnp.log(lb)
    b = jnp.log1p(-lb) + log_sig
    log_f = jnp.maximum(a, b) + jnp.log(1.0 + jnp.exp(-jnp.abs(a - b)))
    return log_f, (1.0 - lb) * _sigmoid(-fx)


def _lower_bound(logits, layer):
    e = jnp.exp(logits - jnp.max(logits, axis=0, keepdims=True))
    sm = e / jnp.sum(e, axis=0, keepdims=True)
    acc = jnp.zeros_like(sm[0:1])
    for j in range(1, layer + 1):
        acc = acc + sm[j:j + 1]
    return jnp.maximum(acc, 0.0)


def _interleave(gens):
    results = [None] * len(gens)
    live = list(range(len(gens)))
    while live:
        for idx in list(live):
            try:
                next(gens[idx])
            except StopIteration as stop:
                results[idx] = stop.value
                live.remove(idx)
    return results


def _hgrn_dir(q, v, log2_f, k, st_ref, c_ref, reverse):
    n = q.shape[0]
    hb = n // 2
    top = n.bit_length() - 2
    lvl = _pair_level(hb, reverse)
    c_all = _scan_rows(log2_f, reverse)
    zeros = jnp.zeros((hb, LANES), BF16)
    yield
    outs = []
    for pair in range(A_HEADS // 2):
        sl2 = slice(2 * pair * LANES, (2 * pair + 2) * LANES)
        c2 = c_all[:, sl2]
        c_ref[pair] = c2
        read_row = lambda r, pair=pair: c_ref[pair, pl.ds(r, 1), :]

        def scores(qt, kt):
            rhs = jnp.concatenate([jnp.concatenate([kt[:, :LANES], zeros], axis=1),
                                   jnp.concatenate([zeros, kt[:, LANES:]], axis=1)], axis=0)
            return _mm_nt(qt, rhs)

        s_lo = s_hi = None
        for l in range(top + 1):
            m = 1 << l
            w = jnp.exp2(_level_exponent(c2, log2_f[:, sl2], read_row, m, reverse)).astype(BF16)
            qt = q[:, sl2] * w
            kt = k[:, sl2] * w
            if l < top:
                a = scores(qt[:hb], kt[:hb])
                b = scores(qt[hb:], kt[hb:])
                keep = jnp.concatenate([lvl, lvl], axis=1) == l
                s_lo = jnp.where(keep, a, 0.0 if s_lo is None else s_lo)
                s_hi = jnp.where(keep, b, 0.0 if s_hi is None else s_hi)
            elif reverse:
                s_off = scores(qt[:hb], kt[hb:])
            else:
                s_off = scores(qt[hb:], kt[:hb])
            yield
        c_end2 = read_row(0 if reverse else n - 1)
        for e in range(2):
            h = 2 * pair + e
            sl = slice(h * LANES, (h + 1) * LANES)
            se = slice(e * LANES, (e + 1) * LANES)
            qh, vh, kh = q[:, sl], v[:, sl], k[:, sl]
            c, c_end = c2[:, se], c_end2[:, se]
            if reverse:
                o = jnp.concatenate([_mm(jnp.concatenate([s_lo[:, se], s_off[:, se]], axis=1).astype(BF16), vh),
                                     _mm(s_hi[:, se].astype(BF16), vh[hb:])], axis=0)
            else:
                o = jnp.concatenate([_mm(s_lo[:, se].astype(BF16), vh[:hb]),
                                     _mm(jnp.concatenate([s_off[:, se], s_hi[:, se]], axis=1).astype(BF16), vh)],
                                    axis=0)
            st = st_ref[h]
            o = o + jnp.sum((qh * kh).astype(F32), axis=-1, keepdims=True) * vh.astype(F32)
            o = o + _mm_nt(qh * jnp.exp2(c).astype(BF16), st.astype(BF16))
            st_ref[h] = st * jnp.exp2(c_end) + _mm_tn(vh, kh * jnp.exp2(c_end - c).astype(BF16))
            outs.append(o)
            yield
    return jnp.concatenate(outs, axis=1)


def _expand_heads(a):
    r = a.shape[0]
    lane = _iota((r, LANES), 1)
    slabs = []
    for j in range(B_HEADS // 2):
        lo = jnp.broadcast_to(a[:, 2 * j:2 * j + 1], (r, LANES))
        hi = jnp.broadcast_to(a[:, 2 * j + 1:2 * j + 2], (r, LANES))
        slabs.append(jnp.where(lane < B_HEADDIM, lo, hi))
    return jnp.concatenate(slabs, axis=1)


def _expand_heads_mxu(a, e2):
    a = jnp.where(_iota(a.shape, 1) < B_HEADS, a, 0.0)
    return _mm(jnp.concatenate(_split_bf16(a, 2), axis=1), e2)


def _ssd_dir(xs, bm, cm, dt_raw, dt_bias, a_log, e2, st_ref, reverse):
    n = xs.shape[0]
    dt = _softplus(dt_raw + dt_bias)
    ac = _scan_rows((-LOG2E) * jnp.exp(a_log) * dt, reverse)
    ac_t = (ac - jnp.log2(dt)).T
    hb = n // 2
    row_s = _iota((hb, hb), 0)
    col_s = _iota((hb, hb), 1)
    tri = (col_s >= row_s) if reverse else (col_s <= row_s)
    lane_gn = _iota((n, B_GN), 1)
    lane = _iota((n, LANES), 1)
    lo, hi = slice(0, hb), slice(hb, n)
    yield
    slabs = []
    for g in range(B_GROUPS):
        cb = _mm_nt(jnp.where((lane_gn >> 6) == g, cm, jnp.zeros_like(cm)), bm)
        yield
        for jp in range(2):
            slab = xs[:, (2 * g + jp) * LANES:(2 * g + jp + 1) * LANES]
            zero = jnp.zeros_like(slab)
            slab_e = (jnp.where(lane < B_HEADDIM, slab, zero), jnp.where(lane < B_HEADDIM, zero, slab))
            w_lo, w_hi = [], []
            for e in range(2):
                h = 4 * g + 2 * jp + e
                col = ac[:, h:h + 1]
                row = ac_t[h:h + 1, :]

                def quad(rs, cs, masked):
                    seg = col[rs] - row[:, cs]
                    if masked:
                        seg = jnp.where(tri, seg, -1e30)
                    return (jnp.exp2(seg) * cb[rs, cs]).astype(BF16)

                if reverse:
                    w_lo += [quad(lo, lo, True), quad(lo, hi, False)]
                    w_hi += [quad(hi, hi, True)]
                else:
                    w_lo += [quad(lo, lo, True)]
                    w_hi += [quad(hi, lo, False), quad(hi, hi, True)]
                yield
            if reverse:
                x_lo = jnp.concatenate([slab_e[0], slab_e[1]], axis=0)
                x_hi = jnp.concatenate([slab_e[0][hi], slab_e[1][hi]], axis=0)
            else:
                x_lo = jnp.concatenate([slab_e[0][lo], slab_e[1][lo]], axis=0)
                x_hi = jnp.concatenate([slab_e[0], slab_e[1]], axis=0)
            slabs.append(jnp.concatenate([_mm(jnp.concatenate(w_lo, axis=1), x_lo),
                                          _mm(jnp.concatenate(w_hi, axis=1), x_hi)], axis=0))
    y = jnp.concatenate(slabs, axis=1)
    st = st_ref[...]
    y = y + _mm(cm, st.astype(BF16)) * _expand_heads_mxu(jnp.exp2(ac), e2)
    yield
    a_end = ac[0:1, :] if reverse else ac[n - 1:n, :]
    wgt = _expand_heads_mxu(jnp.exp2(a_end - ac) * dt, e2)
    upd = _mm_tn(bm, xs * wgt.astype(BF16))
    block = (_iota(st.shape, 0) >> 6) == (_iota(st.shape, 1) >> 8)
    st_ref[...] = jnp.where(block, st * _expand_heads(jnp.exp2(a_end)) + upd, 0.0)
    return y


def _load_states(sa_in, sb_in, sa_ref, sb_ref):
    if sa_in is None:
        sa_ref[...] = jnp.zeros_like(sa_ref)
        sb_ref[...] = jnp.zeros_like(sb_ref)
    else:
        sa_ref[...] = sa_in[0]
        sb_ref[...] = jnp.zeros_like(sb_ref)
        for g in range(B_GROUPS):
            sb_ref[g * B_STATE:(g + 1) * B_STATE, g * B_GW:(g + 1) * B_GW] = sb_in[0, g]


def _store_states(sa_out, sb_out, sa_ref, sb_ref):
    for h in range(A_HEADS):
        sa_out[0, 0, 0, h] = sa_ref[h].T
    per = B_HEADS // B_GROUPS
    for pair in range(B_GROUPS // 2):
        t = sb_ref[pair * 2 * B_STATE:(pair + 1) * 2 * B_STATE, pair * 2 * B_GW:(pair + 1) * 2 * B_GW].T
        for gl in range(2):
            for j in range(per):
                r0 = gl * B_GW + j * B_HEADDIM
                sb_out[0, 0, 0, (2 * pair + gl) * per + j] = t[r0:r0 + B_HEADDIM, gl * B_STATE:(gl + 1) * B_STATE]


def _mod_kernel(c_ref, w_ref, b_ref, o_ref):
    o_ref[0] = jnp.dot(_silu(c_ref[...]), w_ref[0], preferred_element_type=F32,
                       precision=lax.Precision.HIGHEST) + b_ref[0]


def _modulation(cvec, w_ada, b_ada):
    depth = w_ada.shape[0]
    return pl.pallas_call(
        _mod_kernel,
        grid=(depth, MOD_COLS // MOD_BLOCK),
        in_specs=[
            pl.BlockSpec((MOD_ROWS, D_MODEL), lambda l, j: (0, 0)),
            pl.BlockSpec((1, D_MODEL, MOD_BLOCK), lambda l, j: (l, 0, j)),
            pl.BlockSpec((1, 1, MOD_BLOCK), lambda l, j: (l, 0, j)),
        ],
        out_specs=pl.BlockSpec((1, MOD_ROWS, MOD_BLOCK), lambda l, j: (l, 0, j)),
        out_shape=jax.ShapeDtypeStruct((depth, MOD_ROWS, MOD_COLS), F32),
        compiler_params=pltpu.CompilerParams(dimension_semantics=("arbitrary", "arbitrary"),
                                             vmem_limit_bytes=VMEM_LIMIT),
        name="adaln_mod",
    )(cvec, w_ada, b_ada.reshape(depth, 1, MOD_COLS))


def _const_spec(shape):
    nd = len(shape)
    return pl.BlockSpec(shape, lambda *_: (0,) * nd, pipeline_mode=pl.Buffered(1))


def _layer_spec(shape, layer):
    nd = len(shape)
    return pl.BlockSpec((1,) + shape, lambda *_: (layer,) + (0,) * nd, pipeline_mode=pl.Buffered(1))


def _tile_kernel_params():
    return pltpu.CompilerParams(dimension_semantics=("arbitrary", "arbitrary"), vmem_limit_bytes=VMEM_LIMIT)


def _state_scratch():
    return [pltpu.VMEM((A_HEADS, A_DV, A_DK), F32), pltpu.VMEM((B_GN, B_WIDTH), F32),
            pltpu.VMEM((A_HEADS // 2, TILE, 2 * LANES), F32)]


def _state_out_shapes(bsz):
    return [jax.ShapeDtypeStruct((bsz, DEPTH, 2, A_HEADS, A_DK, A_DV), F32),
            jax.ShapeDtypeStruct((bsz, DEPTH, 2, B_HEADS, B_HEADDIM, B_STATE), F32)]


def _fwd_kernel(*refs, layer, row_block, has_init, emit_state, n_alias):
    (x_ref, mod_ref, ln_ref, w_ref, lbl_ref, lblb_ref, cw_ref, cb_ref, dtb_ref, alog_ref, dskip_ref,
     e2_ref), refs = refs[:12], refs[12:]
    if has_init:
        (sa_in, sb_in), refs = refs[:2], refs[2:]
    else:
        sa_in = sb_in = None
    refs = refs[n_alias:]
    (q_out, v_out, l2fbw_out, xbc_out, dtbw_out, ga_out, z_out, gates_out, oa_out, yb_out,
     kbw_out), refs = refs[:11], refs[11:]
    if emit_state:
        (sa_out, sb_out), refs = refs[:2], refs[2:]
    sa_ref, sb_ref, c_ref = refs

    i = pl.program_id(1)

    @pl.when(i == 0)
    def _():
        _load_states(sa_in, sb_in, sa_ref, sb_ref)

    x = x_ref[0]
    mod = mod_ref[0]
    shift, scale = mod[:, 0:D_MODEL], mod[:, D_MODEL:2 * D_MODEL]
    u = ((_rms(x) * ln_ref[...]) * (1.0 + scale) + shift).astype(BF16)

    def proj(lo, hi):
        return _mm(u, w_ref[0, :, lo:hi])

    def backward_gates(fx):
        log_f, k = _hgrn_gates(fx, _lower_bound(lblb_ref[...], layer))
        kbw_out[0] = k.astype(BF16)
        return log_f * LOG2E

    pending = [(ga_out, 0, C_GA, A_WIDTH, _silu), (l2fbw_out, 0, C_FBW, A_KW, backward_gates)]
    pending += [(z_out, o, C_Z + o, FILL_COLS, _silu) for o in range(0, B_WIDTH, FILL_COLS)]
    pending += [(gates_out, o, C_GATES + o, FILL_COLS, _sigmoid) for o in range(0, 2 * D_MODEL, FILL_COLS)]
    pending.append((dtbw_out, 0, C_DTBW, LANES, lambda raw: raw))

    def deferred():
        for ref, dst, src, width, fn in pending:
            raw = proj(src, src + width)
            yield
            ref[0, :, dst:dst + width] = fn(raw).astype(ref.dtype)
            yield

    fills = deferred()
    q = proj(C_Q, C_I)
    v = proj(C_I, C_FFW)
    fx = proj(C_FFW, C_FBW)
    xbc = proj(C_XBC, C_GATES)
    dt_raw = proj(C_DTFW, C_DTBW)
    next(fills)
    q = _silu(q).astype(BF16)
    v = v.astype(BF16)
    q_out[0] = q
    v_out[0] = v
    log_f, k = _hgrn_gates(fx, _lower_bound(lbl_ref[...], layer))
    next(fills), next(fills), next(fills)

    prev, nxt = _row_neighbours(xbc, row_block)
    cw = cw_ref[...]
    xbc = _silu(cb_ref[...] + prev * cw[0:1] + xbc * cw[1:2] + nxt * cw[2:3])
    xbc16 = xbc.astype(BF16)
    xbc_out[0] = xbc16

    o_a, y, _ = _interleave([
        _hgrn_dir(q, v, log_f * LOG2E, k.astype(BF16), sa_ref, c_ref, False),
        _ssd_dir(xbc16[:, 0:B_WIDTH], xbc16[:, B_WIDTH:B_WIDTH + B_GN], xbc16[:, B_WIDTH + B_GN:B_CONV_CH],
                 dt_raw, dtb_ref[...], alog_ref[...], e2_ref[...], sb_ref, False),
        fills])
    oa_out[0] = o_a
    yb_out[0] = y + dskip_ref[...] * xbc[:, 0:B_WIDTH]

    if emit_state:
        @pl.when(i == pl.num_programs(1) - 1)
        def _():
            _store_states(sa_out, sb_out, sa_ref, sb_ref)


def _fwd_call(x, mod, lp, layer, row_block, mod_row0, mod_stride, init, emit):
    bsz, length, _ = x.shape
    nt = length // TILE

    def tok(width):
        return pl.BlockSpec((1, TILE, width), lambda b, i: (b, i, 0))

    def per_seq(shape):
        return pl.BlockSpec((1,) + shape, lambda b, i: (b,) + (0,) * len(shape))

    in_specs = [tok(D_MODEL), pl.BlockSpec((1, 1, MOD_COLS), lambda b, i: (mod_row0 + mod_stride * b, 0, 0)),
                _const_spec((1, D_MODEL)), _layer_spec((D_MODEL, IN_PACKED), layer), _const_spec((DEPTH, A_KW)),
                _const_spec((DEPTH, A_KW)), _const_spec((3, B_CONV_CH)), _const_spec((1, B_CONV_CH)), _const_spec((1, LANES)),
                _const_spec((1, LANES)), _const_spec((1, B_WIDTH)), _const_spec((2 * LANES, B_WIDTH))]
    args = [x, mod, lp['ln1'], lp['w_in'], lp['lbl'][0], lp['lbl'][1], lp['conv_w'], lp['conv_b'], lp['dt_bias'][0],
            lp['a_log'][0], lp['d_skip'], lp['expand']]
    if init is not None:
        in_specs += [per_seq((A_HEADS, A_DV, A_DK)), per_seq((B_GROUPS, B_STATE, B_GW))]
        args += list(init)
    out_specs = [tok(w) for w, _ in FWD_OUT]
    out_shape = [jax.ShapeDtypeStruct((bsz, length, w), dt) for w, dt in FWD_OUT]
    aliases, n_alias = {}, 0
    if emit is not None:
        out_specs += [pl.BlockSpec((1, 1, 1, A_HEADS, A_DK, A_DV), lambda b, i: (b, layer, 0, 0, 0, 0)),
                      pl.BlockSpec((1, 1, 1, B_HEADS, B_HEADDIM, B_STATE), lambda b, i: (b, layer, 0, 0, 0, 0))]
        out_shape += _state_out_shapes(bsz)
        if emit['acc'] is not None:
            aliases = {len(args): len(FWD_OUT), len(args) + 1: len(FWD_OUT) + 1}
            in_specs += [pl.BlockSpec(memory_space=pl.ANY), pl.BlockSpec(memory_space=pl.ANY)]
            args += list(emit['acc'])
            n_alias = 2
    return pl.pallas_call(
        functools.partial(_fwd_kernel, layer=layer, row_block=row_block, has_init=init is not None,
                          emit_state=emit is not None, n_alias=n_alias),
        grid=(bsz, nt), in_specs=in_specs, out_specs=out_specs, out_shape=out_shape,
        input_output_aliases=aliases, scratch_shapes=_state_scratch(),
        compiler_params=_tile_kernel_params(),
        name="mixer_fwd",
    )(*args)


def _bwd_kernel(*refs, has_init, emit_state, n_alias):
    (x_ref, mod_ref, q_ref, v_ref, l2fbw_ref, xbc_ref, dtbw_ref, ga_ref, z_ref, gates_ref, oa_ref, yb_ref, kbw_ref,
     dtb_ref, alog_ref, an_ref, bn_ref, wa_ref, wb_ref, wo_ref, e2_ref), refs = refs[:21], refs[21:]
    if has_init:
        (sa_in, sb_in), refs = refs[:2], refs[2:]
    else:
        sa_in = sb_in = None
    refs = refs[n_alias:]
    x_out, refs = refs[0], refs[1:]
    if emit_state:
        (sa_out, sb_out), refs = refs[:2], refs[2:]
    sa_ref, sb_ref, c_ref = refs

    i = pl.program_id(1)

    @pl.when(i == 0)
    def _():
        _load_states(sa_in, sb_in, sa_ref, sb_ref)

    def branch_a():
        o_a = yield from _hgrn_dir(q_ref[0], v_ref[0], l2fbw_ref[0], kbw_ref[0], sa_ref, c_ref, True)
        o_a = oa_ref[0] + o_a
        an = an_ref[...]
        o_a = jnp.concatenate(
            [_rms(o_a[:, h * A_DV:(h + 1) * A_DV]) * an[:, h * A_DV:(h + 1) * A_DV] for h in range(A_HEADS)],
            axis=1)
        o_a = o_a * ga_ref[0].astype(F32)
        yield
        return gates_ref[0, :, 0:D_MODEL].astype(F32) * _mm(o_a.astype(BF16), wa_ref[0])

    xbc = xbc_ref[0]
    merged_a, y_b = _interleave([
        branch_a(),
        _ssd_dir(xbc[:, 0:B_WIDTH], xbc[:, B_WIDTH:B_WIDTH + B_GN], xbc[:, B_WIDTH + B_GN:B_CONV_CH],
                 dtbw_ref[0], dtb_ref[...], alog_ref[...], e2_ref[...], sb_ref, True)])
    y_b = (yb_ref[0] + y_b) * z_ref[0].astype(F32)
    bn = bn_ref[...]
    gw = B_WIDTH // B_GROUPS
    y_b = jnp.concatenate(
        [_rms(y_b[:, g * gw:(g + 1) * gw]) * bn[:, g * gw:(g + 1) * gw] for g in range(B_GROUPS)], axis=1)

    merged = merged_a + gates_ref[0, :, D_MODEL:2 * D_MODEL].astype(F32) * _mm(y_b.astype(BF16), wb_ref[0])
    mix = _mm(merged.astype(BF16), wo_ref[0])
    gate1 = mod_ref[0][:, 2 * D_MODEL:3 * D_MODEL]
    x_out[0] = x_ref[0] + gate1 * mix

    if emit_state:
        @pl.when(i == pl.num_programs(1) - 1)
        def _():
            _store_states(sa_out, sb_out, sa_ref, sb_ref)


def _bwd_call(x, mod, fwd_outs, lp, layer, mod_row0, mod_stride, init, emit):
    bsz, length, _ = x.shape
    nt = length // TILE

    def tok(width):
        return pl.BlockSpec((1, TILE, width), lambda b, i: (b, nt - 1 - i, 0))

    def per_seq(shape):
        return pl.BlockSpec((1,) + shape, lambda b, i: (b,) + (0,) * len(shape))

    in_specs = ([tok(D_MODEL), pl.BlockSpec((1, 1, MOD_COLS), lambda b, i: (mod_row0 + mod_stride * b, 0, 0))]
                + [tok(w) for w, _ in FWD_OUT]
                + [_const_spec((1, LANES)), _const_spec((1, LANES)),
                   _const_spec((1, A_WIDTH)), _const_spec((1, B_WIDTH)), _layer_spec((A_WIDTH, D_MODEL), layer),
                   _layer_spec((B_WIDTH, D_MODEL), layer), _layer_spec((D_MODEL, D_MODEL), layer),
                   _const_spec((2 * LANES, B_WIDTH))])
    args = ([x, mod] + list(fwd_outs)
            + [lp['dt_bias'][1], lp['a_log'][1], lp['a_norm'], lp['b_norm'], lp['w_br_a'],
               lp['w_br_b'], lp['w_out'], lp['expand']])
    if init is not None:
        in_specs += [per_seq((A_HEADS, A_DV, A_DK)), per_seq((B_GROUPS, B_STATE, B_GW))]
        args += list(init)
    out_specs = [tok(D_MODEL)]
    out_shape = [jax.ShapeDtypeStruct((bsz, length, D_MODEL), F32)]
    aliases, n_alias = {}, 0
    if emit is not None:
        out_specs += [pl.BlockSpec((1, 1, 1, A_HEADS, A_DK, A_DV), lambda b, i: (b, layer, 1, 0, 0, 0)),
                      pl.BlockSpec((1, 1, 1, B_HEADS, B_HEADDIM, B_STATE), lambda b, i: (b, layer, 1, 0, 0, 0))]
        out_shape += _state_out_shapes(bsz)
        aliases = {len(args): 1, len(args) + 1: 2}
        in_specs += [pl.BlockSpec(memory_space=pl.ANY), pl.BlockSpec(memory_space=pl.ANY)]
        args += list(emit['acc'])
        n_alias = 2
    return pl.pallas_call(
        functools.partial(_bwd_kernel, has_init=init is not None, emit_state=emit is not None, n_alias=n_alias),
        grid=(bsz, nt), in_specs=in_specs, out_specs=out_specs, out_shape=out_shape,
        input_output_aliases=aliases, scratch_shapes=_state_scratch(),
        compiler_params=_tile_kernel_params(),
        name="mixer_bwd",
    )(*args)


def _ffn_kernel(x_ref, mod_ref, ln_ref, wu_ref, cw_ref, cb_ref, wd_ref, lnf_ref, o_ref, *, row_block, final_norm):
    x = x_ref[0]
    mod = mod_ref[0]
    shift, scale, gate = (mod[:, 3 * D_MODEL:4 * D_MODEL], mod[:, 4 * D_MODEL:5 * D_MODEL],
                          mod[:, 5 * D_MODEL:6 * D_MODEL])
    u = ((_rms(x) * ln_ref[...]) * (1.0 + scale) + shift).astype(BF16)

    def up(j):
        return (_mm(u, wu_ref[0, :, j * FF_BLOCK:(j + 1) * FF_BLOCK]),
                _mm(u, wu_ref[0, :, D_FF + j * FF_BLOCK:D_FF + (j + 1) * FF_BLOCK]))

    def conv(h, lo):
        cw = cw_ref[:, lo:lo + FF_BLOCK]
        prev, nxt = _row_neighbours(h, row_block)
        return cb_ref[:, lo:lo + FF_BLOCK] + prev * cw[0:1] + h * cw[1:2] + nxt * cw[2:3]

    steps = D_FF // FF_BLOCK
    acc = None
    acts = []
    h = up(0)
    for j in range(steps):
        h_next = up(j + 1) if j + 1 < steps else None
        acts.append((_silu(conv(h[0], j * FF_BLOCK)) * conv(h[1], D_FF + j * FF_BLOCK)).astype(BF16))
        if len(acts) == FF_DOWN_BLOCKS or j + 1 == steps:
            lo = (j + 1 - len(acts)) * FF_BLOCK
            part = _mm(acts[0] if len(acts) == 1 else jnp.concatenate(acts, axis=1),
                       wd_ref[0, lo:(j + 1) * FF_BLOCK, :])
            acc = part if acc is None else acc + part
            acts = []
        h = h_next
    y = x + gate * acc
    if final_norm:
        y = _rms(y) * lnf_ref[...]
    o_ref[0] = y


def _ffn_call(x, mod, lp, ln_f, layer, row_block, mod_row0, mod_stride, final_norm):
    shape = x.shape
    if shape[1] % FF_TILE:
        assert mod_stride == 0 and FF_TILE % shape[1] == 0 and shape[1] % row_block == 0
        x = x.reshape(shape[0] * shape[1] // FF_TILE, FF_TILE, D_MODEL)
    bsz, length, _ = x.shape
    nt = length // FF_TILE
    tok = pl.BlockSpec((1, FF_TILE, D_MODEL), lambda b, i: (b, i, 0))
    return pl.pallas_call(
        functools.partial(_ffn_kernel, row_block=row_block, final_norm=final_norm),
        grid=(bsz, nt),
        in_specs=[tok, pl.BlockSpec((1, 1, MOD_COLS), lambda b, i: (mod_row0 + mod_stride * b, 0, 0)),
                  _const_spec((1, D_MODEL)), _layer_spec((D_MODEL, 2 * D_FF), layer), _const_spec((3, 2 * D_FF)),
                  _const_spec((1, 2 * D_FF)), _layer_spec((D_FF, D_MODEL), layer), _const_spec((1, D_MODEL))],
        out_specs=tok,
        out_shape=jax.ShapeDtypeStruct((bsz, length, D_MODEL), F32),
        compiler_params=_tile_kernel_params(),
        name="conv_ffn",
    )(x, mod, lp['ln2'], lp['w_ff_up'], lp['ff_conv_w'], lp['ff_conv_b'], lp['w_ff_down'], ln_f).reshape(shape)


def _pad_heads(a):
    return jnp.pad(a.astype(F32), [(0, 0)] * (a.ndim - 1) + [(0, LANES - a.shape[-1])])


def _pack_w_in(w):
    src_gates = C_XBC + B_CONV_CH + 2 * B_HEADS
    pad = jnp.zeros(w.shape[:-1] + (LANES - B_HEADS,), w.dtype)
    return jnp.concatenate([w[..., :C_GATES], w[..., src_gates:], w[..., C_GATES:C_GATES + B_HEADS], pad,
                            w[..., C_GATES + B_HEADS:src_gates], pad], axis=-1).astype(BF16)


def _hgrn_state_in(s):
    return jnp.swapaxes(s.astype(F32), -1, -2)


def _ssd_state_in(s):
    b = s.shape[0]
    s = s.astype(F32).reshape(b, B_GROUPS, B_HEADS // B_GROUPS, B_HEADDIM, B_STATE)
    return s.transpose(0, 1, 4, 2, 3).reshape(b, B_GROUPS, B_STATE, B_GW)


def kernel(x_prompt, x_sample, c, state_hgrn, state_ssd, c_ctx, w_ada, b_ada, ln1, ln2, ln_f, w_in, lb_logits,
           a_norm, conv_w, conv_b, dt_bias, a_log, d_skip, b_norm, w_br_a, w_br_b, w_out, w_ff_up, ff_conv_w,
           ff_conv_b, w_ff_down):
    depth = w_in.shape[0]
    dec_b = x_sample.shape[0]
    assert depth == DEPTH and dec_b + 1 <= MOD_ROWS
    assert x_prompt.shape[1] == TILE and x_sample.shape[1] % TILE == 0 and TILE % GRID_W == 0

    cvec = jnp.concatenate([c_ctx[None].astype(F32), c.astype(F32),
                            jnp.zeros((MOD_ROWS - 1 - dec_b, D_MODEL), F32)], axis=0)
    mod_all = _modulation(cvec, w_ada.astype(F32), b_ada.astype(F32))

    expand = (jnp.arange(2 * LANES)[:, None] % LANES == jnp.arange(B_WIDTH)[None, :] // B_HEADDIM).astype(BF16)
    stacked = dict(w_in=_pack_w_in(w_in), w_br_a=w_br_a.astype(BF16), w_br_b=w_br_b.astype(BF16),
                   w_out=w_out.astype(BF16), w_ff_up=w_ff_up.astype(BF16), w_ff_down=w_ff_down.astype(BF16))
    layers = []
    for l in range(depth):
        layers.append(dict(
            stacked, ln1=ln1[l][None].astype(F32), ln2=ln2[l][None].astype(F32),
            lbl=(lb_logits[:, 0, :].astype(F32), lb_logits[:, 1, :].astype(F32)),
            a_norm=a_norm[l][None].astype(F32), b_norm=b_norm[l][None].astype(F32),
            conv_w=conv_w[l].astype(F32), conv_b=conv_b[l][None].astype(F32),
            dt_bias=(_pad_heads(dt_bias[l, 0])[None], _pad_heads(dt_bias[l, 1])[None]),
            a_log=(_pad_heads(a_log[l, 0])[None], _pad_heads(a_log[l, 1])[None]),
            d_skip=jnp.repeat(d_skip[l].astype(F32), B_HEADDIM)[None], expand=expand,
            ff_conv_w=ff_conv_w[l].astype(F32), ff_conv_b=ff_conv_b[l][None].astype(F32)))
    ln_f2 = ln_f[None].astype(F32)

    def run(x, row_block, mod_row0, mod_stride, states, emit_state):
        acc = None
        for l in range(depth):
            lp = layers[l]
            mod = mod_all[l].reshape(MOD_ROWS, 1, MOD_COLS)
            init_f = init_b = None
            if states is not None:
                init_f = (_hgrn_state_in(states[0][:, l, 0]), _ssd_state_in(states[1][:, l, 0]))
                init_b = (_hgrn_state_in(states[0][:, l, 1]), _ssd_state_in(states[1][:, l, 1]))
            fo = _fwd_call(x, mod, lp, l, row_block, mod_row0, mod_stride, init_f,
                           dict(acc=acc) if emit_state else None)
            if emit_state:
                acc = tuple(fo[len(FWD_OUT):])
            bo = _bwd_call(x, mod, fo[:len(FWD_OUT)], lp, l, mod_row0, mod_stride, init_b,
                           dict(acc=acc) if emit_state else None)
            if emit_state:
                acc = (bo[1], bo[2])
            x = _ffn_call(bo[0], mod, lp, ln_f2, l, row_block, mod_row0, mod_stride, l == depth - 1)
        return x, acc

    y_prompt, (new_hgrn, new_ssd) = run(x_prompt.astype(F32), x_prompt.shape[1], 0, 0, None, True)
    y_sample, _ = run(x_sample.astype(F32), GRID_W, 1, 1, (state_hgrn, state_ssd), False)
    return (y_prompt.astype(x_prompt.dtype), y_sample.astype(x_sample.dtype),
            new_hgrn.astype(x_prompt.dtype), new_ssd.astype(x_prompt.dtype))
```

```python
import functools

import jax
import jax.numpy as jnp
from jax import lax
from jax.experimental import pallas as pl
from jax.experimental.pallas import tpu as pltpu

F32 = jnp.float32
BF16 = jnp.bfloat16

D_MODEL = 1024
DEPTH = 4
GRID_W = 64
A_HEADS = 4
A_DK = 128
A_DV = 128
A_KW = A_HEADS * A_DK
A_WIDTH = A_HEADS * A_DV
B_HEADS = 16
B_HEADDIM = 64
B_WIDTH = B_HEADS * B_HEADDIM
B_GROUPS = 4
B_STATE = 64
B_GN = B_GROUPS * B_STATE
B_CONV_CH = B_WIDTH + 2 * B_GN
B_GW = B_WIDTH // B_GROUPS
D_FF = 2816
N_MOD = 6
EPS = 1e-6
LOG2E = 1.4426950408889634

LANES = 128
SUBLANES = 8
TILE = 256
MOD_ROWS = 8
MOD_COLS = N_MOD * D_MODEL
MOD_BLOCK = 1536
FF_TILE = 512
FF_BLOCK = 256
FF_DOWN_BLOCKS = D_FF // FF_BLOCK
FILL_COLS = 512
VMEM_LIMIT = 60 * 1024 * 1024

C_Q = 0
C_I = C_Q + A_KW
C_FFW = C_I + A_WIDTH
C_FBW = C_FFW + A_KW
C_GA = C_FBW + A_KW
C_Z = C_GA + A_WIDTH
C_XBC = C_Z + B_WIDTH
C_GATES = C_XBC + B_CONV_CH
C_DTFW = C_GATES + 2 * D_MODEL
C_DTBW = C_DTFW + LANES
IN_PACKED = C_DTBW + LANES

FWD_OUT = ((A_KW, BF16), (A_WIDTH, BF16), (A_KW, F32), (B_CONV_CH, BF16), (LANES, F32), (A_WIDTH, BF16),
           (B_WIDTH, BF16), (2 * D_MODEL, BF16), (A_WIDTH, F32), (B_WIDTH, F32), (A_KW, BF16))

NT_DIMS = (((1,), (1,)), ((), ()))
TN_DIMS = (((0,), (0,)), ((), ()))


def _mm(a, b):
    return jnp.dot(a, b, preferred_element_type=F32)


def _mm_nt(a, b):
    return lax.dot_general(a, b, NT_DIMS, preferred_element_type=F32)


def _mm_tn(a, b):
    return lax.dot_general(a, b, TN_DIMS, preferred_element_type=F32)


def _sigmoid(x):
    return 0.5 * jnp.tanh(0.5 * x) + 0.5


def _silu(x):
    h = 0.5 * x
    return h + h * jnp.tanh(h)


def _softplus(x):
    return jnp.maximum(x, 0.0) + jnp.log1p(jnp.exp(-jnp.abs(x)))


def _rms(x):
    return x * lax.rsqrt(jnp.mean(x * x, axis=-1, keepdims=True) + EPS)


def _iota(shape, dim):
    return lax.broadcasted_iota(jnp.int32, shape, dim)


def _zero_rows(a, period, offset):
    sub = _iota((SUBLANES, a.shape[1]), 0)
    groups = []
    for g in range(a.shape[0] // SUBLANES):
        blk = a[g * SUBLANES:(g + 1) * SUBLANES]
        if (g * SUBLANES) % period == offset - offset % SUBLANES:
            blk = jnp.where(sub == offset % SUBLANES, 0.0, blk)
        groups.append(blk)
    return jnp.concatenate(groups, axis=0)


def _row_neighbours(h, row_block):
    n = h.shape[0]
    return (_zero_rows(pltpu.roll(h, 1, axis=0), row_block, 0),
            _zero_rows(pltpu.roll(h, n - 1, axis=0), row_block, row_block - 1))


def _split_bf16(x, parts):
    out = []
    for i in range(parts):
        if i + 1 < parts:
            p = lax.bitcast_convert_type(lax.bitcast_convert_type(x, jnp.uint32) & jnp.uint32(0xFFFF0000), F32)
            out.append(p.astype(BF16))
            x = x - p
        else:
            out.append(x.astype(BF16))
    return out


def _scan_rows(g, reverse):
    n = g.shape[0]
    row = _iota((n, n), 0)
    col = _iota((n, n), 1)
    tri = jnp.where((col >= row) if reverse else (col <= row), 1.0, 0.0).astype(BF16)
    return _mm(jnp.concatenate([tri, tri, tri], axis=1), jnp.concatenate(_split_bf16(g, 3), axis=0))


def _level_ref(c, read_row, m, off):
    n, w = c.shape
    pieces = [jnp.broadcast_to(read_row(j * 2 * m + off), (2 * m, w)) for j in range(n // (2 * m))]
    return pieces[0] if len(pieces) == 1 else jnp.concatenate(pieces, axis=0)


def _neg_abs(x):
    bits = lax.bitcast_convert_type(x, jnp.uint32) | jnp.uint32(0x80000000)
    return lax.bitcast_convert_type(bits, F32)


def _level_exponent(c, g, read_row, m, reverse):
    n = c.shape[0]
    if m > 2:
        return _neg_abs(c - _level_ref(c, read_row, m, m if reverse else m - 1))
    rk = _iota(c.shape, 0) & (2 * m - 1)
    if m == 1:
        return jnp.where(rk == (0 if reverse else 1), g, 0.0)
    up = pltpu.roll(g, n - 1, axis=0)
    down = pltpu.roll(g, 1, axis=0)
    if reverse:
        return jnp.where(rk == 0, g + up, jnp.where(rk == 1, g, jnp.where(rk == 2, 0.0, down)))
    return jnp.where(rk == 0, up, jnp.where(rk == 1, 0.0, jnp.where(rk == 2, g, g + down)))


def _pair_level(n, reverse):
    t = _iota((n, n), 0)
    s = _iota((n, n), 1)
    x = t ^ s
    lvl = jnp.full((n, n), -1, jnp.int32)
    for l in range(n.bit_length() - 1):
        lvl = lvl + (x >= (1 << l)).astype(jnp.int32)
    return jnp.where((s > t) if reverse else (s < t), lvl, -1)


def _hgrn_gates(fx, lb):
    log_sig = jnp.minimum(fx, 0.0) - jnp.log(1.0 + jnp.exp(-jnp.abs(fx)))
    a = jnp.log(lb)
    b = jnp.log1p(-lb) + log_sig
    log_f = jnp.maximum(a, b) + jnp.log(1.0 + jnp.exp(-jnp.abs(a - b)))
    return log_f, (1.0 - lb) * _sigmoid(-fx)


def _lower_bound(logits, layer):
    e = jnp.exp(logits - jnp.max(logits, axis=0, keepdims=True))
    sm = e / jnp.sum(e, axis=0, keepdims=True)
    acc = jnp.zeros_like(sm[0:1])
    for j in range(1, layer + 1):
        acc = acc + sm[j:j + 1]
    return jnp.maximum(acc, 0.0)


def _interleave(gens):
    results = [None] * len(gens)
    live = list(range(len(gens)))
    while live:
        for idx in list(live):
            try:
                next(gens[idx])
            except StopIteration as stop:
                results[idx] = stop.value
                live.remove(idx)
    return results


def _hgrn_dir(q, v, log2_f, k, st_ref, c_ref, reverse):
    n = q.shape[0]
    hb = n // 2
    top = n.bit_length() - 2
    lvl = _pair_level(hb, reverse)
    c_all = _scan_rows(log2_f, reverse)
    zeros = jnp.zeros((hb, LANES), BF16)
    yield
    outs = []
    for pair in range(A_HEADS // 2):
        sl2 = slice(2 * pair * LANES, (2 * pair + 2) * LANES)
        c2 = c_all[:, sl2]
        c_ref[pair] = c2
        read_row = lambda r, pair=pair: c_ref[pair, pl.ds(r, 1), :]

        def scores(qt, kt):
            rhs = jnp.concatenate([jnp.concatenate([kt[:, :LANES], zeros], axis=1),
                                   jnp.concatenate([zeros, kt[:, LANES:]], axis=1)], axis=0)
            return _mm_nt(qt, rhs)

        s_lo = s_hi = None
        for l in range(top + 1):
            m = 1 << l
            w = jnp.exp2(_level_exponent(c2, log2_f[:, sl2], read_row, m, reverse)).astype(BF16)
            qt = q[:, sl2] * w
            kt = k[:, sl2] * w
            if l < top:
                a = scores(qt[:hb], kt[:hb])
                b = scores(qt[hb:], kt[hb:])
                keep = jnp.concatenate([lvl, lvl], axis=1) == l
                s_lo = jnp.where(keep, a, 0.0 if s_lo is None else s_lo)
                s_hi = jnp.where(keep, b, 0.0 if s_hi is None else s_hi)
            elif reverse:
                s_off = scores(qt[:hb], kt[hb:])
            else:
                s_off = scores(qt[hb:], kt[:hb])
            yield
        c_end2 = read_row(0 if reverse else n - 1)
        for e in range(2):
            h = 2 * pair + e
            sl = slice(h * LANES, (h + 1) * LANES)
            se = slice(e * LANES, (e + 1) * LANES)
            qh, vh, kh = q[:, sl], v[:, sl], k[:, sl]
            c, c_end = c2[:, se], c_end2[:, se]
            if reverse:
                o = jnp.concatenate([_mm(jnp.concatenate([s_lo[:, se], s_off[:, se]], axis=1).astype(BF16), vh),
                                     _mm(s_hi[:, se].astype(BF16), vh[hb:])], axis=0)
            else:
                o = jnp.concatenate([_mm(s_lo[:, se].astype(BF16), vh[:hb]),
                                     _mm(jnp.concatenate([s_off[:, se], s_hi[:, se]], axis=1).astype(BF16), vh)],
                                    axis=0)
            st = st_ref[h]
            o = o + jnp.sum((qh * kh).astype(F32), axis=-1, keepdims=True) * vh.astype(F32)
            o = o + _mm_nt(qh * jnp.exp2(c).astype(BF16), st.astype(BF16))
            st_ref[h] = st * jnp.exp2(c_end) + _mm_tn(vh, kh * jnp.exp2(c_end - c).astype(BF16))
            outs.append(o)
            yield
    return jnp.concatenate(outs, axis=1)


def _expand_heads(a):
    r = a.shape[0]
    lane = _iota((r, LANES), 1)
    slabs = []
    for j in range(B_HEADS // 2):
        lo = jnp.broadcast_to(a[:, 2 * j:2 * j + 1], (r, LANES))
        hi = jnp.broadcast_to(a[:, 2 * j + 1:2 * j + 2], (r, LANES))
        slabs.append(jnp.where(lane < B_HEADDIM, lo, hi))
    return jnp.concatenate(slabs, axis=1)


def _expand_heads_mxu(a, e2):
    a = jnp.where(_iota(a.shape, 1) < B_HEADS, a, 0.0)
    return _mm(jnp.concatenate(_split_bf16(a, 2), axis=1), e2)


def _ssd_dir(xs, bm, cm, dt_raw, dt_bias, a_log, e2, st_ref, reverse):
    n = xs.shape[0]
    dt = _softplus(dt_raw + dt_bias)
    ac = _scan_rows((-LOG2E) * jnp.exp(a_log) * dt, reverse)
    ac_t = (ac - jnp.log2(dt)).T
    hb = n // 2
    row_s = _iota((hb, hb), 0)
    col_s = _iota((hb, hb), 1)
    tri = (col_s >= row_s) if reverse else (col_s <= row_s)
    lane_gn = _iota((n, B_GN), 1)
    lane = _iota((n, LANES), 1)
    lo, hi = slice(0, hb), slice(hb, n)
    yield
    slabs = []
    for g in range(B_GROUPS):
        cb = _mm_nt(jnp.where((lane_gn >> 6) == g, cm, jnp.zeros_like(cm)), bm)
        yield
        for jp in range(2):
            slab = xs[:, (2 * g + jp) * LANES:(2 * g + jp + 1) * LANES]
            zero = jnp.zeros_like(slab)
            slab_e = (jnp.where(lane < B_HEADDIM, slab, zero), jnp.where(lane < B_HEADDIM, zero, slab))
            w_lo, w_hi = [], []
            for e in range(2):
                h = 4 * g + 2 * jp + e
                col = ac[:, h:h + 1]
                row = ac_t[h:h + 1, :]

                def quad(rs, cs, masked):
                    seg = col[rs] - row[:, cs]
                    if masked:
                        seg = jnp.where(tri, seg, -1e30)
                    return (jnp.exp2(seg) * cb[rs, cs]).astype(BF16)

                if reverse:
                    w_lo += [quad(lo, lo, True), quad(lo, hi, False)]
                    w_hi += [quad(hi, hi, True)]
                else:
                    w_lo += [quad(lo, lo, True)]
                    w_hi += [quad(hi, lo, False), quad(hi, hi, True)]
                yield
            if reverse:
                x_lo = jnp.concatenate([slab_e[0], slab_e[1]], axis=0)
                x_hi = jnp.concatenate([slab_e[0][hi], slab_e[1][hi]], axis=0)
            else:
                x_lo = jnp.concatenate([slab_e[0][lo], slab_e[1][lo]], axis=0)
                x_hi = jnp.concatenate([slab_e[0], slab_e[1]], axis=0)
            slabs.append(jnp.concatenate([_mm(jnp.concatenate(w_lo, axis=1), x_lo),
                                          _mm(jnp.concatenate(w_hi, axis=1), x_hi)], axis=0))
    y = jnp.concatenate(slabs, axis=1)
    st = st_ref[...]
    y = y + _mm(cm, st.astype(BF16)) * _expand_heads_mxu(jnp.exp2(ac), e2)
    yield
    a_end = ac[0:1, :] if reverse else ac[n - 1:n, :]
    wgt = _expand_heads_mxu(jnp.exp2(a_end - ac) * dt, e2)
    upd = _mm_tn(bm, xs * wgt.astype(BF16))
    block = (_iota(st.shape, 0) >> 6) == (_iota(st.shape, 1) >> 8)
    st_ref[...] = jnp.where(block, st * _expand_heads(jnp.exp2(a_end)) + upd, 0.0)
    return y


def _load_states(sa_in, sb_in, sa_ref, sb_ref):
    if sa_in is None:
        sa_ref[...] = jnp.zeros_like(sa_ref)
        sb_ref[...] = jnp.zeros_like(sb_ref)
    else:
        sa_ref[...] = sa_in[0]
        sb_ref[...] = jnp.zeros_like(sb_ref)
        for g in range(B_GROUPS):
            sb_ref[g * B_STATE:(g + 1) * B_STATE, g * B_GW:(g + 1) * B_GW] = sb_in[0, g]


def _store_states(sa_out, sb_out, sa_ref, sb_ref):
    for h in range(A_HEADS):
        sa_out[0, 0, 0, h] = sa_ref[h].T
    per = B_HEADS // B_GROUPS
    for pair in range(B_GROUPS // 2):
        t = sb_ref[pair * 2 * B_STATE:(pair + 1) * 2 * B_STATE, pair * 2 * B_GW:(pair + 1) * 2 * B_GW].T
        for gl in range(2):
            for j in range(per):
                r0 = gl * B_GW + j * B_HEADDIM
                sb_out[0, 0, 0, (2 * pair + gl) * per + j] = t[r0:r0 + B_HEADDIM, gl * B_STATE:(gl + 1) * B_STATE]


def _mod_kernel(c_ref, w_ref, b_ref, o_ref):
    o_ref[0] = jnp.dot(_silu(c_ref[...]), w_ref[0], preferred_element_type=F32,
                       precision=lax.Precision.HIGHEST) + b_ref[0]


def _modulation(cvec, w_ada, b_ada):
    depth = w_ada.shape[0]
    return pl.pallas_call(
        _mod_kernel,
        grid=(depth, MOD_COLS // MOD_BLOCK),
        in_specs=[
            pl.BlockSpec((MOD_ROWS, D_MODEL), lambda l, j: (0, 0)),
            pl.BlockSpec((1, D_MODEL, MOD_BLOCK), lambda l, j: (l, 0, j)),
            pl.BlockSpec((1, 1, MOD_BLOCK), lambda l, j: (l, 0, j)),
        ],
        out_specs=pl.BlockSpec((1, MOD_ROWS, MOD_BLOCK), lambda l, j: (l, 0, j)),
        out_shape=jax.ShapeDtypeStruct((depth, MOD_ROWS, MOD_COLS), F32),
        compiler_params=pltpu.CompilerParams(dimension_semantics=("arbitrary", "arbitrary"),
                                             vmem_limit_bytes=VMEM_LIMIT),
        name="adaln_mod",
    )(cvec, w_ada, b_ada.reshape(depth, 1, MOD_COLS))


def _const_spec(shape):
    nd = len(shape)
    return pl.BlockSpec(shape, lambda *_: (0,) * nd, pipeline_mode=pl.Buffered(1))


def _layer_spec(shape, layer):
    nd = len(shape)
    return pl.BlockSpec((1,) + shape, lambda *_: (layer,) + (0,) * nd, pipeline_mode=pl.Buffered(1))


def _tile_kernel_params():
    return pltpu.CompilerParams(dimension_semantics=("arbitrary", "arbitrary"), vmem_limit_bytes=VMEM_LIMIT)


def _state_scratch():
    return [pltpu.VMEM((A_HEADS, A_DV, A_DK), F32), pltpu.VMEM((B_GN, B_WIDTH), F32),
            pltpu.VMEM((A_HEADS // 2, TILE, 2 * LANES), F32)]


def _state_out_shapes(bsz):
    return [jax.ShapeDtypeStruct((bsz, DEPTH, 2, A_HEADS, A_DK, A_DV), F32),
            jax.ShapeDtypeStruct((bsz, DEPTH, 2, B_HEADS, B_HEADDIM, B_STATE), F32)]


def _fwd_kernel(*refs, layer, row_block, has_init, emit_state, n_alias):
    (x_ref, mod_ref, ln_ref, w_ref, lbl_ref, lblb_ref, cw_ref, cb_ref, dtb_ref, alog_ref, dskip_ref,
     e2_ref), refs = refs[:12], refs[12:]
    if has_init:
        (sa_in, sb_in), refs = refs[:2], refs[2:]
    else:
        sa_in = sb_in = None
    refs = refs[n_alias:]
    (q_out, v_out, l2fbw_out, xbc_out, dtbw_out, ga_out, z_out, gates_out, oa_out, yb_out,
     kbw_out), refs = refs[:11], refs[11:]
    if emit_state:
        (sa_out, sb_out), refs = refs[:2], refs[2:]
    sa_ref, sb_ref, c_ref = refs

    i = pl.program_id(1)

    @pl.when(i == 0)
    def _():
        _load_states(sa_in, sb_in, sa_ref, sb_ref)

    x = x_ref[0]
    mod = mod_ref[0]
    shift, scale = mod[:, 0:D_MODEL], mod[:, D_MODEL:2 * D_MODEL]
    u = ((_rms(x) * ln_ref[...]) * (1.0 + scale) + shift).astype(BF16)

    def proj(lo, hi):
        return _mm(u, w_ref[0, :, lo:hi])

    def backward_gates(fx):
        log_f, k = _hgrn_gates(fx, _lower_bound(lblb_ref[...], layer))
        kbw_out[0] = k.astype(BF16)
        return log_f * LOG2E

    pending = [(ga_out, 0, C_GA, A_WIDTH, _silu), (l2fbw_out, 0, C_FBW, A_KW, backward_gates)]
    pending += [(z_out, o, C_Z + o, FILL_COLS, _silu) for o in range(0, B_WIDTH, FILL_COLS)]
    pending += [(gates_out, o, C_GATES + o, FILL_COLS, _sigmoid) for o in range(0, 2 * D_MODEL, FILL_COLS)]
    pending.append((dtbw_out, 0, C_DTBW, LANES, lambda raw: raw))

    def deferred():
        for ref, dst, src, width, fn in pending:
            raw = proj(src, src + width)
            yield
            ref[0, :, dst:dst + width] = fn(raw).astype(ref.dtype)
            yield

    fills = deferred()
    q = proj(C_Q, C_I)
    v = proj(C_I, C_FFW)
    fx = proj(C_FFW, C_FBW)
    xbc = proj(C_XBC, C_GATES)
    dt_raw = proj(C_DTFW, C_DTBW)
    next(fills)
    q = _silu(q).astype(BF16)
    v = v.astype(BF16)
    q_out[0] = q
    v_out[0] = v
    log_f, k = _hgrn_gates(fx, _lower_bound(lbl_ref[...], layer))
    next(fills), next(fills), next(fills)

    prev, nxt = _row_neighbours(xbc, row_block)
    cw = cw_ref[...]
    xbc = _silu(cb_ref[...] + prev * cw[0:1] + xbc * cw[1:2] + nxt * cw[2:3])
    xbc16 = xbc.astype(BF16)
    xbc_out[0] = xbc16

    o_a, y, _ = _interleave([
        _hgrn_dir(q, v, log_f * LOG2E, k.astype(BF16), sa_ref, c_ref, False),
        _ssd_dir(xbc16[:, 0:B_WIDTH], xbc16[:, B_WIDTH:B_WIDTH + B_GN], xbc16[:, B_WIDTH + B_GN:B_CONV_CH],
                 dt_raw, dtb_ref[...], alog_ref[...], e2_ref[...], sb_ref, False),
        fills])
    oa_out[0] = o_a
    yb_out[0] = y + dskip_ref[...] * xbc[:, 0:B_WIDTH]

    if emit_state:
        @pl.when(i == pl.num_programs(1) - 1)
        def _():
            _store_states(sa_out, sb_out, sa_ref, sb_ref)


def _fwd_call(x, mod, lp, layer, row_block, mod_row0, mod_stride, init, emit):
    bsz, length, _ = x.shape
    nt = length // TILE

    def tok(width):
        return pl.BlockSpec((1, TILE, width), lambda b, i: (b, i, 0))

    def per_seq(shape):
        return pl.BlockSpec((1,) + shape, lambda b, i: (b,) + (0,) * len(shape))

    in_specs = [tok(D_MODEL), pl.BlockSpec((1, 1, MOD_COLS), lambda b, i: (mod_row0 + mod_stride * b, 0, 0)),
                _const_spec((1, D_MODEL)), _layer_spec((D_MODEL, IN_PACKED), layer), _const_spec((DEPTH, A_KW)),
                _const_spec((DEPTH, A_KW)), _const_spec((3, B_CONV_CH)), _const_spec((1, B_CONV_CH)), _const_spec((1, LANES)),
                _const_spec((1, LANES)), _const_spec((1, B_WIDTH)), _const_spec((2 * LANES, B_WIDTH))]
    args = [x, mod, lp['ln1'], lp['w_in'], lp['lbl'][0], lp['lbl'][1], lp['conv_w'], lp['conv_b'], lp['dt_bias'][0],
            lp['a_log'][0], lp['d_skip'], lp['expand']]
    if init is not None:
        in_specs += [per_seq((A_HEADS, A_DV, A_DK)), per_seq((B_GROUPS, B_STATE, B_GW))]
        args += list(init)
    out_specs = [tok(w) for w, _ in FWD_OUT]
    out_shape = [jax.ShapeDtypeStruct((bsz, length, w), dt) for w, dt in FWD_OUT]
    aliases, n_alias = {}, 0
    if emit is not None:
        out_specs += [pl.BlockSpec((1, 1, 1, A_HEADS, A_DK, A_DV), lambda b, i: (b, layer, 0, 0, 0, 0)),
                      pl.BlockSpec((1, 1, 1, B_HEADS, B_HEADDIM, B_STATE), lambda b, i: (b, layer, 0, 0, 0, 0))]
        out_shape += _state_out_shapes(bsz)
        if emit['acc'] is not None:
            aliases = {len(args): len(FWD_OUT), len(args) + 1: len(FWD_OUT) + 1}
            in_specs += [pl.BlockSpec(memory_space=pl.ANY), pl.BlockSpec(memory_space=pl.ANY)]
            args += list(emit['acc'])
            n_alias = 2
    return pl.pallas_call(
        functools.partial(_fwd_kernel, layer=layer, row_block=row_block, has_init=init is not None,
                          emit_state=emit is not None, n_alias=n_alias),
        grid=(bsz, nt), in_specs=in_specs, out_specs=out_specs, out_shape=out_shape,
        input_output_aliases=aliases, scratch_shapes=_state_scratch(),
        compiler_params=_tile_kernel_params(),
        name="mixer_fwd",
    )(*args)


def _bwd_kernel(*refs, has_init, emit_state, n_alias):
    (x_ref, mod_ref, q_ref, v_ref, l2fbw_ref, xbc_ref, dtbw_ref, ga_ref, z_ref, gates_ref, oa_ref, yb_ref, kbw_ref,
     dtb_ref, alog_ref, an_ref, bn_ref, wa_ref, wb_ref, wo_ref, e2_ref), refs = refs[:21], refs[21:]
    if has_init:
        (sa_in, sb_in), refs = refs[:2], refs[2:]
    else:
        sa_in = sb_in = None
    refs = refs[n_alias:]
    x_out, refs = refs[0], refs[1:]
    if emit_state:
        (sa_out, sb_out), refs = refs[:2], refs[2:]
    sa_ref, sb_ref, c_ref = refs

    i = pl.program_id(1)

    @pl.when(i == 0)
    def _():
        _load_states(sa_in, sb_in, sa_ref, sb_ref)

    def branch_a():
        o_a = yield from _hgrn_dir(q_ref[0], v_ref[0], l2fbw_ref[0], kbw_ref[0], sa_ref, c_ref, True)
        o_a = oa_ref[0] + o_a
        an = an_ref[...]
        o_a = jnp.concatenate(
            [_rms(o_a[:, h * A_DV:(h + 1) * A_DV]) * an[:, h * A_DV:(h + 1) * A_DV] for h in range(A_HEADS)],
            axis=1)
        o_a = o_a * ga_ref[0].astype(F32)
        yield
        return gates_ref[0, :, 0:D_MODEL].astype(F32) * _mm(o_a.astype(BF16), wa_ref[0])

    xbc = xbc_ref[0]
    merged_a, y_b = _interleave([
        branch_a(),
        _ssd_dir(xbc[:, 0:B_WIDTH], xbc[:, B_WIDTH:B_WIDTH + B_GN], xbc[:, B_WIDTH + B_GN:B_CONV_CH],
                 dtbw_ref[0], dtb_ref[...], alog_ref[...], e2_ref[...], sb_ref, True)])
    y_b = (yb_ref[0] + y_b) * z_ref[0].astype(F32)
    bn = bn_ref[...]
    gw = B_WIDTH // B_GROUPS
    y_b = jnp.concatenate(
        [_rms(y_b[:, g * gw:(g + 1) * gw]) * bn[:, g * gw:(g + 1) * gw] for g in range(B_GROUPS)], axis=1)

    merged = merged_a + gates_ref[0, :, D_MODEL:2 * D_MODEL].astype(F32) * _mm(y_b.astype(BF16), wb_ref[0])
    mix = _mm(merged.astype(BF16), wo_ref[0])
    gate1 = mod_ref[0][:, 2 * D_MODEL:3 * D_MODEL]
    x_out[0] = x_ref[0] + gate1 * mix

    if emit_state:
        @pl.when(i == pl.num_programs(1) - 1)
        def _():
            _store_states(sa_out, sb_out, sa_ref, sb_ref)


def _bwd_call(x, mod, fwd_outs, lp, layer, mod_row0, mod_stride, init, emit):
    bsz, length, _ = x.shape
    nt = length // TILE

    def tok(width):
        return pl.BlockSpec((1, TILE, width), lambda b, i: (b, nt - 1 - i, 0))

    def per_seq(shape):
        return pl.BlockSpec((1,) + shape, lambda b, i: (b,) + (0,) * len(shape))

    in_specs = ([tok(D_MODEL), pl.BlockSpec((1, 1, MOD_COLS), lambda b, i: (mod_row0 + mod_stride * b, 0, 0))]
                + [tok(w) for w, _ in FWD_OUT]
                + [_const_spec((1, LANES)), _const_spec((1, LANES)),
                   _const_spec((1, A_WIDTH)), _const_spec((1, B_WIDTH)), _layer_spec((A_WIDTH, D_MODEL), layer),
                   _layer_spec((B_WIDTH, D_MODEL), layer), _layer_spec((D_MODEL, D_MODEL), layer),
                   _const_spec((2 * LANES, B_WIDTH))])
    args = ([x, mod] + list(fwd_outs)
            + [lp['dt_bias'][1], lp['a_log'][1], lp['a_norm'], lp['b_norm'], lp['w_br_a'],
               lp['w_br_b'], lp['w_out'], lp['expand']])
    if init is not None:
        in_specs += [per_seq((A_HEADS, A_DV, A_DK)), per_seq((B_GROUPS, B_STATE, B_GW))]
        args += list(init)
    out_specs = [tok(D_MODEL)]
    out_shape = [jax.ShapeDtypeStruct((bsz, length, D_MODEL), F32)]
    aliases, n_alias = {}, 0
    if emit is not None:
        out_specs += [pl.BlockSpec((1, 1, 1, A_HEADS, A_DK, A_DV), lambda b, i: (b, layer, 1, 0, 0, 0)),
                      pl.BlockSpec((1, 1, 1, B_HEADS, B_HEADDIM, B_STATE), lambda b, i: (b, layer, 1, 0, 0, 0))]
        out_shape += _state_out_shapes(bsz)
        aliases = {len(args): 1, len(args) + 1: 2}
        in_specs += [pl.BlockSpec(memory_space=pl.ANY), pl.BlockSpec(memory_space=pl.ANY)]
        args += list(emit['acc'])
        n_alias = 2
    return pl.pallas_call(
        functools.partial(_bwd_kernel, has_init=init is not None, emit_state=emit is not None, n_alias=n_alias),
        grid=(bsz, nt), in_specs=in_specs, out_specs=out_specs, out_shape=out_shape,
        input_output_aliases=aliases, scratch_shapes=_state_scratch(),
        compiler_params=_tile_kernel_params(),
        name="mixer_bwd",
    )(*args)


def _ffn_kernel(x_ref, mod_ref, ln_ref, wu_ref, cw_ref, cb_ref, wd_ref, lnf_ref, o_ref, *, row_block, final_norm):
    x = x_ref[0]
    mod = mod_ref[0]
    shift, scale, gate = (mod[:, 3 * D_MODEL:4 * D_MODEL], mod[:, 4 * D_MODEL:5 * D_MODEL],
                          mod[:, 5 * D_MODEL:6 * D_MODEL])
    u = ((_rms(x) * ln_ref[...]) * (1.0 + scale) + shift).astype(BF16)

    def up(j):
        return (_mm(u, wu_ref[0, :, j * FF_BLOCK:(j + 1) * FF_BLOCK]),
                _mm(u, wu_ref[0, :, D_FF + j * FF_BLOCK:D_FF + (j + 1) * FF_BLOCK]))

    def conv(h, lo):
        cw = cw_ref[:, lo:lo + FF_BLOCK]
        prev, nxt = _row_neighbours(h, row_block)
        return cb_ref[:, lo:lo + FF_BLOCK] + prev * cw[0:1] + h * cw[1:2] + nxt * cw[2:3]

    steps = D_FF // FF_BLOCK
    acc = None
    acts = []
    h = up(0)
    for j in range(steps):
        h_next = up(j + 1) if j + 1 < steps else None
        halves = [(_silu(conv(h[0][r:r + TILE], j * FF_BLOCK)) * conv(h[1][r:r + TILE], D_FF + j * FF_BLOCK))
                  .astype(BF16) for r in range(0, FF_TILE, TILE)]
        acts.append(jnp.concatenate(halves, axis=0))
        if len(acts) == FF_DOWN_BLOCKS or j + 1 == steps:
            lo = (j + 1 - len(acts)) * FF_BLOCK
            part = _mm(acts[0] if len(acts) == 1 else jnp.concatenate(acts, axis=1),
                       wd_ref[0, lo:(j + 1) * FF_BLOCK, :])
            acc = part if acc is None else acc + part
            acts = []
        h = h_next
    y = x + gate * acc
    if final_norm:
        y = _rms(y) * lnf_ref[...]
    o_ref[0] = y


def _ffn_call(x, mod, lp, ln_f, layer, row_block, mod_row0, mod_stride, final_norm):
    shape = x.shape
    if shape[1] % FF_TILE:
        assert mod_stride == 0 and FF_TILE % shape[1] == 0 and shape[1] % row_block == 0
        x = x.reshape(shape[0] * shape[1] // FF_TILE, FF_TILE, D_MODEL)
    bsz, length, _ = x.shape
    nt = length // FF_TILE
    tok = pl.BlockSpec((1, FF_TILE, D_MODEL), lambda b, i: (b, i, 0))
    return pl.pallas_call(
        functools.partial(_ffn_kernel, row_block=row_block, final_norm=final_norm),
        grid=(bsz, nt),
        in_specs=[tok, pl.BlockSpec((1, 1, MOD_COLS), lambda b, i: (mod_row0 + mod_stride * b, 0, 0)),
                  _const_spec((1, D_MODEL)), _layer_spec((D_MODEL, 2 * D_FF), layer), _const_spec((3, 2 * D_FF)),
                  _const_spec((1, 2 * D_FF)), _layer_spec((D_FF, D_MODEL), layer), _const_spec((1, D_MODEL))],
        out_specs=tok,
        out_shape=jax.ShapeDtypeStruct((bsz, length, D_MODEL), F32),
        compiler_params=_tile_kernel_params(),
        name="conv_ffn",
    )(x, mod, lp['ln2'], lp['w_ff_up'], lp['ff_conv_w'], lp['ff_conv_b'], lp['w_ff_down'], ln_f).reshape(shape)


def _pad_heads(a):
    return jnp.pad(a.astype(F32), [(0, 0)] * (a.ndim - 1) + [(0, LANES - a.shape[-1])])


def _pack_w_in(w):
    src_gates = C_XBC + B_CONV_CH + 2 * B_HEADS
    pad = jnp.zeros(w.shape[:-1] + (LANES - B_HEADS,), w.dtype)
    return jnp.concatenate([w[..., :C_GATES], w[..., src_gates:], w[..., C_GATES:C_GATES + B_HEADS], pad,
                            w[..., C_GATES + B_HEADS:src_gates], pad], axis=-1).astype(BF16)


def _hgrn_state_in(s):
    return jnp.swapaxes(s.astype(F32), -1, -2)


def _ssd_state_in(s):
    b = s.shape[0]
    s = s.astype(F32).reshape(b, B_GROUPS, B_HEADS // B_GROUPS, B_HEADDIM, B_STATE)
    return s.transpose(0, 1, 4, 2, 3).reshape(b, B_GROUPS, B_STATE, B_GW)


def kernel(x_prompt, x_sample, c, state_hgrn, state_ssd, c_ctx, w_ada, b_ada, ln1, ln2, ln_f, w_in, lb_logits,
           a_norm, conv_w, conv_b, dt_bias, a_log, d_skip, b_norm, w_br_a, w_br_b, w_out, w_ff_up, ff_conv_w,
           ff_conv_b, w_ff_down):
    depth = w_in.shape[0]
    dec_b = x_sample.shape[0]
    assert depth == DEPTH and dec_b + 1 <= MOD_ROWS
    assert x_prompt.shape[1] == TILE and x_sample.shape[1] % TILE == 0 and TILE % GRID_W == 0

    cvec = jnp.concatenate([c_ctx[None].astype(F32), c.astype(F32),
                            jnp.zeros((MOD_ROWS - 1 - dec_b, D_MODEL), F32)], axis=0)
    mod_all = _modulation(cvec, w_ada.astype(F32), b_ada.astype(F32))

    expand = (jnp.arange(2 * LANES)[:, None] % LANES == jnp.arange(B_WIDTH)[None, :] // B_HEADDIM).astype(BF16)
    stacked = dict(w_in=_pack_w_in(w_in), w_br_a=w_br_a.astype(BF16), w_br_b=w_br_b.astype(BF16),
                   w_out=w_out.astype(BF16), w_ff_up=w_ff_up.astype(BF16), w_ff_down=w_ff_down.astype(BF16))
    layers = []
    for l in range(depth):
        layers.append(dict(
            stacked, ln1=ln1[l][None].astype(F32), ln2=ln2[l][None].astype(F32),
            lbl=(lb_logits[:, 0, :].astype(F32), lb_logits[:, 1, :].astype(F32)),
            a_norm=a_norm[l][None].astype(F32), b_norm=b_norm[l][None].astype(F32),
            conv_w=conv_w[l].astype(F32), conv_b=conv_b[l][None].astype(F32),
            dt_bias=(_pad_heads(dt_bias[l, 0])[None], _pad_heads(dt_bias[l, 1])[None]),
            a_log=(_pad_heads(a_log[l, 0])[None], _pad_heads(a_log[l, 1])[None]),
            d_skip=jnp.repeat(d_skip[l].astype(F32), B_HEADDIM)[None], expand=expand,
            ff_conv_w=ff_conv_w[l].astype(F32), ff_conv_b=ff_conv_b[l][None].astype(F32)))
    ln_f2 = ln_f[None].astype(F32)

    def run(x, row_block, mod_row0, mod_stride, states, emit_state):
        acc = None
        for l in range(depth):
            lp = layers[l]
            mod = mod_all[l].reshape(MOD_ROWS, 1, MOD_COLS)
            init_f = init_b = None
            if states is not None:
                init_f = (_hgrn_state_in(states[0][:, l, 0]), _ssd_state_in(states[1][:, l, 0]))
                init_b = (_hgrn_state_in(states[0][:, l, 1]), _ssd_state_in(states[1][:, l, 1]))
            fo = _fwd_call(x, mod, lp, l, row_block, mod_row0, mod_stride, init_f,
                           dict(acc=acc) if emit_state else None)
            if emit_state:
                acc = tuple(fo[len(FWD_OUT):])
            bo = _bwd_call(x, mod, fo[:len(FWD_OUT)], lp, l, mod_row0, mod_stride, init_b,
                           dict(acc=acc) if emit_state else None)
            if emit_state:
                acc = (bo[1], bo[2])
            x = _ffn_call(bo[0], mod, lp, ln_f2, l, row_block, mod_row0, mod_stride, l == depth - 1)
        return x, acc

    y_prompt, (new_hgrn, new_ssd) = run(x_prompt.astype(F32), x_prompt.shape[1], 0, 0, None, True)
    y_sample, _ = run(x_sample.astype(F32), GRID_W, 1, 1, (state_hgrn, state_ssd), False)
    return (y_prompt.astype(x_prompt.dtype), y_sample.astype(x_sample.dtype),
            new_hgrn.astype(x_prompt.dtype), new_ssd.astype(x_prompt.dtype))
```
